```python
import math
import jax, jax.numpy as jnp
from jax import lax
import numpy as np

D_MODEL = 1024
BATCH = 8
SEQ = 2048
DEPTH = 1

CHUNK = 64
Q_BLOCK = 128

SSD_HEADS = 8
SSD_HEAD_DIM = 64
D_SSD = SSD_HEADS * SSD_HEAD_DIM
SSD_GROUPS = 2
SSD_STATE = 128
CONV_WIDTH = 4
D_CONV = D_SSD + 2 * SSD_GROUPS * SSD_STATE

SB_HEADS = 8
SB_HEAD_DIM = 64
D_SB = SB_HEADS * SB_HEAD_DIM

D_MIX = D_SSD + D_SB
D_IN_PROJ = D_SSD + D_CONV + SSD_HEADS + 3 * D_SB

D_FF = int(math.ceil(8 * D_MODEL / 3 / 256) * 256)
EPS = 1e-6

kernel_name = "hymba_ssd_stickbreaking_sandwich_block"


def rms_norm(x, g):
    xf = x.astype(jnp.float32)
    y = xf * lax.rsqrt(jnp.mean(xf * xf, axis=-1, keepdims=True) + EPS)
    return (y * g.astype(jnp.float32)).astype(x.dtype)


def causal_depthwise_conv(u, w, b):
    k = w.shape[0]
    out = lax.conv_general_dilated(
        u, w[:, None, :].astype(u.dtype), window_strides=(1,), padding=[(k - 1, 0)],
        dimension_numbers=("NWC", "WIO", "NWC"), feature_group_count=u.shape[-1])
    return out + b.astype(u.dtype)


def segsum(a):
    t = a.shape[-1]
    cs = jnp.cumsum(a, axis=-1)
    diff = cs[..., :, None] - cs[..., None, :]
    mask = jnp.tril(jnp.ones((t, t), dtype=bool))
    return jnp.where(mask, diff, -jnp.inf)


def ssd_scan(xs, dt, a, bm, cm):
    b_, seq, n_heads, p = xs.shape
    g, n = bm.shape[-2:]
    r = n_heads // g
    c = seq // CHUNK
    x = (xs * dt[..., None]).reshape(b_, c, CHUNK, g, r, p)
    adt = (dt * a).reshape(b_, c, CHUNK, g, r).transpose(0, 3, 4, 1, 2)
    bc = bm.reshape(b_, c, CHUNK, g, n)
    cc = cm.reshape(b_, c, CHUNK, g, n)
    a_cs = jnp.cumsum(adt, axis=-1)
    decay_in = jnp.exp(segsum(adt))
    cb = jnp.einsum("bclgn,bcsgn->bgcls", cc, bc)
    y_diag = jnp.einsum("bgcls,bgrcls,bcsgrp->bclgrp", cb, decay_in, x)
    decay_states = jnp.exp(a_cs[..., -1:] - a_cs)
    chunk_states = jnp.einsum("bclgn,bgrcl,bclgrp->bcgrpn", bc, decay_states, x)
    chunk_decay = jnp.exp(a_cs[..., -1])

    def step(state, inp):
        s_c, d_c = inp
        return state * d_c[..., None, None] + s_c, state

    init = jnp.zeros_like(chunk_states[:, 0])
    _, prev = lax.scan(step, init, (jnp.moveaxis(chunk_states, 1, 0), jnp.moveaxis(chunk_decay, -1, 0)))
    prev = jnp.moveaxis(prev, 0, 1)
    y_off = jnp.einsum("bclgn,bcgrpn,bgrcl->bclgrp", cc, prev, jnp.exp(a_cs))
    return (y_diag + y_off).reshape(b_, seq, n_heads, p)


def stick_breaking_attention(q, k, v):
    seq, d = q.shape[2], q.shape[3]
    scale = 1.0 / math.sqrt(d)
    outs = []
    for i in range(seq // Q_BLOCK):
        start = i * Q_BLOCK
        end = start + Q_BLOCK
        kb = k[:, :, :end]
        vb = v[:, :, :end]
        z = jnp.einsum("bhqd,bhkd->bhqk", q[:, :, start:end], kb).astype(jnp.float32) * scale
        t_idx = start + jnp.arange(Q_BLOCK)
        s_idx = jnp.arange(end)
        strict = s_idx[None, :] < t_idx[:, None]
        log_1mb = jnp.where(strict, jax.nn.log_sigmoid(-z), 0.0)
        after = lax.cumsum(log_1mb, axis=3, reverse=True) - log_1mb
        w = jnp.where(strict, jnp.exp(jax.nn.log_sigmoid(z) + after), 0.0)
        outs.append(jnp.einsum("bhqk,bhkd->bhqd", w.astype(vb.dtype), vb))
    return jnp.concatenate(outs, axis=2)


def setup_inputs(seed: int = 0) -> dict:
    key = jax.random.key(seed)
    ks = jax.random.split(key, 20)
    f32 = jnp.float32

    def gain(k, n):
        return 1.0 + 0.02 * jax.random.normal(k, (DEPTH, n), f32)

    x = jax.random.normal(ks[0], (BATCH, SEQ, D_MODEL), f32)
    w_in = jax.random.normal(ks[1], (DEPTH, D_MODEL, D_IN_PROJ), f32) * D_MODEL ** -0.5
    conv_w = jax.random.normal(ks[2], (DEPTH, CONV_WIDTH, D_CONV), f32) * CONV_WIDTH ** -0.5
    conv_b = 0.02 * jax.random.normal(ks[3], (DEPTH, D_CONV), f32)
    dt0 = jnp.exp(jax.random.uniform(ks[4], (DEPTH, SSD_HEADS), f32, math.log(1e-3), math.log(1e-1)))
    dt_bias = dt0 + jnp.log(-jnp.expm1(-dt0))
    a_log = jnp.log(jax.random.uniform(ks[5], (DEPTH, SSD_HEADS), f32, 1.0, 16.0))
    d_skip = 1.0 + 0.1 * jax.random.normal(ks[6], (DEPTH, SSD_HEADS), f32)
    w_out = jax.random.normal(ks[7], (DEPTH, D_MIX, D_MODEL), f32) * D_MIX ** -0.5
    w_gate = jax.random.normal(ks[8], (DEPTH, D_MODEL, D_FF), f32) * D_MODEL ** -0.5
    w_up = jax.random.normal(ks[9], (DEPTH, D_MODEL, D_FF), f32) * D_MODEL ** -0.5
    w_down = jax.random.normal(ks[10], (DEPTH, D_FF, D_MODEL), f32) * D_FF ** -0.5
    return {
        "x": x,
        "pre_mix_gain": gain(ks[11], D_MODEL),
        "w_in": w_in,
        "conv_w": conv_w,
        "conv_b": conv_b,
        "dt_bias": dt_bias,
        "a_log": a_log,
        "d_skip": d_skip,
        "ssd_norm_gain": gain(ks[12], D_SSD),
        "sb_norm_gain": gain(ks[13], D_SB),
        "w_out": w_out,
        "post_mix_gain": gain(ks[14], D_MODEL),
        "pre_ffn_gain": gain(ks[15], D_MODEL),
        "w_gate": w_gate,
        "w_up": w_up,
        "w_down": w_down,
        "post_ffn_gain": gain(ks[16], D_MODEL),
    }


def reference(x, pre_mix_gain, w_in, conv_w, conv_b, dt_bias, a_log, d_skip, ssd_norm_gain,
              sb_norm_gain, w_out, post_mix_gain, pre_ffn_gain, w_gate, w_up, w_down, post_ffn_gain):
    b_, seq, _ = x.shape
    splits = np.cumsum([D_SSD, D_CONV, SSD_HEADS, D_SB, D_SB]).tolist()
    for layer in range(DEPTH):
        h = rms_norm(x, pre_mix_gain[layer])
        proj = h @ w_in[layer]
        z, xbc, dt_raw, q, k, v = jnp.split(proj, splits, axis=-1)

        xbc = jax.nn.silu(causal_depthwise_conv(xbc, conv_w[layer], conv_b[layer]))
        xs, bm, cm = jnp.split(xbc, [D_SSD, D_SSD + SSD_GROUPS * SSD_STATE], axis=-1)
        xs = xs.astype(jnp.float32).reshape(b_, seq, SSD_HEADS, SSD_HEAD_DIM)
        bm = bm.astype(jnp.float32).reshape(b_, seq, SSD_GROUPS, SSD_STATE)
        cm = cm.astype(jnp.float32).reshape(b_, seq, SSD_GROUPS, SSD_STATE)
        dt = jax.nn.softplus(dt_raw.astype(jnp.float32) + dt_bias[layer].astype(jnp.float32))
        a = -jnp.exp(a_log[layer].astype(jnp.float32))
        y = ssd_scan(xs, dt, a, bm, cm) + d_skip[layer].astype(jnp.float32)[:, None] * xs
        y = y.reshape(b_, seq, D_SSD) * jax.nn.silu(z.astype(jnp.float32))
        y = rms_norm(y.reshape(b_, seq, SSD_GROUPS, D_SSD // SSD_GROUPS),
                     ssd_norm_gain[layer].reshape(SSD_GROUPS, D_SSD // SSD_GROUPS))
        y_ssd = y.reshape(b_, seq, D_SSD).astype(x.dtype)

        def heads(t):
            return t.reshape(b_, seq, SB_HEADS, SB_HEAD_DIM).transpose(0, 2, 1, 3)
        o = stick_breaking_attention(heads(q), heads(k), heads(v)).transpose(0, 2, 1, 3)
        o = rms_norm(o, sb_norm_gain[layer].reshape(SB_HEADS, SB_HEAD_DIM))
        y_sb = o.reshape(b_, seq, D_SB).astype(x.dtype)

        mix = jnp.concatenate([y_ssd, y_sb], axis=-1) @ w_out[layer]
        x = x + rms_norm(mix, post_mix_gain[layer])

        h = rms_norm(x, pre_ffn_gain[layer])
        f = (jax.nn.silu(h @ w_gate[layer]) * (h @ w_up[layer])) @ w_down[layer]
        x = x + rms_norm(f, post_ffn_gain[layer])
    return x
```

```python
import functools
import math

import jax
import jax.numpy as jnp
from jax import lax
from jax.experimental import pallas as pl
from jax.experimental.pallas import tpu as pltpu

F32 = jnp.float32
BF16 = jnp.bfloat16

EPS = 1e-6
LANES = 128

D_MODEL = 1024
N_HEADS = 8
HEAD_DIM = 64
D_HEADS = N_HEADS * HEAD_DIM
N_PAIRS = N_HEADS // 2
SSD_GROUPS = 2
SSD_STATE = 128
CONV_WIDTH = 4
D_CONV = D_HEADS + 2 * SSD_GROUPS * SSD_STATE
D_FF = 2816

C_Z = 0
C_XBC = C_Z + D_HEADS
C_DT = C_XBC + D_CONV
C_Q = C_DT + LANES
C_K = C_Q + D_HEADS
C_V = C_K + D_HEADS
C_END = C_V + D_HEADS

TM_PROJ = 512
T_SSD = 128
T_ATT = 128
TM_FFN = 512
FF_CHUNK = 256
CONV_TAIL = 8

VMEM_LIMIT = 56 * 1024 * 1024


def _dot(a, b):
    return jnp.dot(a, b, preferred_element_type=F32)


def _dot_nt(a, b):
    return lax.dot_general(a, b, (((1,), (1,)), ((), ())), preferred_element_type=F32)


def _dot_tn(a, b):
    return lax.dot_general(a, b, (((0,), (0,)), ((), ())), preferred_element_type=F32)


def _split3(x):
    hi = x.astype(BF16)
    r = x - hi.astype(F32)
    mid = r.astype(BF16)
    lo = (r - mid.astype(F32)).astype(BF16)
    return hi, mid, lo


def _silu(x):
    return x / (1.0 + jnp.exp(-x))


def _softplus(x):
    return jnp.maximum(x, 0.0) + jnp.log1p(jnp.exp(-jnp.abs(x)))


def _inproj_kernel(x_ref, g_ref, w_ref, z_ref, xbc_ref, dt_ref, q_ref, k_ref, v_ref):
    x = x_ref[...]
    ms = jnp.mean(x * x, axis=-1, keepdims=True)
    h = (x * lax.rsqrt(ms + EPS) * g_ref[...]).astype(BF16)
    z_ref[...] = _dot(h, w_ref[:, C_Z:C_XBC])
    xbc_ref[...] = _dot(h, w_ref[:, C_XBC:C_DT])
    dt_ref[...] = _dot(h, w_ref[:, C_DT:C_Q])
    q_ref[...] = _dot(h, w_ref[:, C_Q:C_K]).astype(BF16)
    k_ref[...] = _dot(h, w_ref[:, C_K:C_V]).astype(BF16)
    v_ref[...] = _dot(h, w_ref[:, C_V:C_END]).astype(BF16)


def _inproj(x2, gain, w_all):
    m = x2.shape[0]
    row = lambda i: (i, 0)
    const = lambda i: (0, 0)
    return pl.pallas_call(
        _inproj_kernel,
        grid=(m // TM_PROJ,),
        in_specs=[
            pl.BlockSpec((TM_PROJ, D_MODEL), row),
            pl.BlockSpec((1, D_MODEL), const),
            pl.BlockSpec((D_MODEL, C_END), const, pipeline_mode=pl.Buffered(1)),
        ],
        out_specs=[
            pl.BlockSpec((TM_PROJ, D_HEADS), row),
            pl.BlockSpec((TM_PROJ, D_CONV), row),
            pl.BlockSpec((TM_PROJ, LANES), row),
            pl.BlockSpec((TM_PROJ, D_HEADS), row),
            pl.BlockSpec((TM_PROJ, D_HEADS), row),
            pl.BlockSpec((TM_PROJ, D_HEADS), row),
        ],
        out_shape=[
            jax.ShapeDtypeStruct((m, D_HEADS), F32),
            jax.ShapeDtypeStruct((m, D_CONV), F32),
            jax.ShapeDtypeStruct((m, LANES), F32),
            jax.ShapeDtypeStruct((m, D_HEADS), BF16),
            jax.ShapeDtypeStruct((m, D_HEADS), BF16),
            jax.ShapeDtypeStruct((m, D_HEADS), BF16),
        ],
        compiler_params=pltpu.CompilerParams(
            dimension_semantics=("arbitrary",), vmem_limit_bytes=VMEM_LIMIT),
        name="inproj",
    )(x2, gain, w_all)


def _ssd_kernel(z_ref, xbc_ref, dt_ref, cw_ref, cb_ref, dtb_ref, a_ref, dskip_ref,
                gain_ref, ltri_ref, y_ref, ext_ref, state_ref):
    t = T_SSD
    c = pl.program_id(1)

    @pl.when(c == 0)
    def _():
        ext_ref[0:CONV_TAIL, :] = jnp.zeros((CONV_TAIL, D_CONV), F32)
        state_ref[...] = jnp.zeros(state_ref.shape, F32)

    ext_ref[CONV_TAIL:CONV_TAIL + t, :] = xbc_ref[...]
    conv = cb_ref[...]
    for k in range(CONV_WIDTH):
        off = CONV_TAIL - (CONV_WIDTH - 1) + k
        conv = conv + ext_ref[off:off + t, :] * cw_ref[k:k + 1, :]
    ext_ref[0:CONV_TAIL, :] = ext_ref[t:t + CONV_TAIL, :]
    xa = _silu(conv)

    dtv = _softplus(dt_ref[...] + dtb_ref[...])
    adt = dtv * a_ref[...]
    hi, mid, lo = _split3(adt)
    acs = _dot(ltri_ref[...], jnp.concatenate([hi, mid, lo], axis=0))
    acs_t = acs.T

    lane = lax.broadcasted_iota(jnp.int32, (t, LANES), 1)
    first_head = lane < HEAD_DIM
    li = lax.broadcasted_iota(jnp.int32, (t, t), 0)
    si = lax.broadcasted_iota(jnp.int32, (t, t), 1)
    causal = li >= si

    def col(v, h):
        return jnp.broadcast_to(v[:, h:h + 1], (t, LANES))

    y_blocks = []
    cb_mats = []
    for g in range(SSD_GROUPS):
        bm = xa[:, D_HEADS + g * SSD_STATE:D_HEADS + (g + 1) * SSD_STATE].astype(BF16)
        cm = xa[:, D_HEADS + (SSD_GROUPS + g) * SSD_STATE:
                D_HEADS + (SSD_GROUPS + g + 1) * SSD_STATE].astype(BF16)
        cb_mats.append((bm, cm, _dot_nt(cm, bm)))

    for p in range(N_PAIRS):
        ha, hb = 2 * p, 2 * p + 1
        bm, cm, cbm = cb_mats[p // (N_PAIRS // SSD_GROUPS)]
        x2 = xa[:, p * LANES:(p + 1) * LANES]
        dt2 = jnp.where(first_head, col(dtv, ha), col(dtv, hb))
        acs2 = jnp.where(first_head, col(acs, ha), col(acs, hb))
        xdt2 = x2 * dt2

        def decay(h):
            seg = col(acs, h) - jnp.broadcast_to(acs_t[h:h + 1, :], (t, t))
            return (cbm * jnp.exp(jnp.where(causal, seg, -jnp.inf))).astype(BF16)

        m2 = jnp.concatenate([decay(ha), decay(hb)], axis=1)
        xdt_a = jnp.where(first_head, xdt2, 0.0).astype(BF16)
        xdt_b = jnp.where(first_head, 0.0, xdt2).astype(BF16)
        y_diag = _dot(m2, jnp.concatenate([xdt_a, xdt_b], axis=0))

        prev = state_ref[p]
        y_off = _dot(cm, prev.astype(BF16)) * jnp.exp(acs2)
        last = acs2[t - 1:t, :]
        xs = (xdt2 * jnp.exp(last - acs2)).astype(BF16)
        state_ref[p] = prev * jnp.exp(last) + _dot_tn(bm, xs)

        y_blocks.append(y_diag + y_off + dskip_ref[:, p * LANES:(p + 1) * LANES] * x2)

    per_group = N_PAIRS // SSD_GROUPS
    for g in range(SSD_GROUPS):
        ys = []
        for p in range(g * per_group, (g + 1) * per_group):
            ys.append(y_blocks[p] * _silu(z_ref[:, p * LANES:(p + 1) * LANES]))
        ss = sum(jnp.sum(y * y, axis=-1, keepdims=True) for y in ys)
        inv = lax.rsqrt(ss * (1.0 / (per_group * LANES)) + EPS)
        for j, y in enumerate(ys):
            p = g * per_group + j
            y_ref[:, p * LANES:(p + 1) * LANES] = (
                y * inv * gain_ref[:, p * LANES:(p + 1) * LANES]).astype(BF16)


def _ssd(z, xbc, dt, conv_w, conv_b, dtb, a_row, dskip, gain, ltri3, batch, seq):
    n_chunks = seq // T_SSD
    row = lambda b, c: (b * n_chunks + c, 0)
    const = lambda b, c: (0, 0)
    return pl.pallas_call(
        _ssd_kernel,
        grid=(batch, n_chunks),
        in_specs=[
            pl.BlockSpec((T_SSD, D_HEADS), row),
            pl.BlockSpec((T_SSD, D_CONV), row),
            pl.BlockSpec((T_SSD, LANES), row),
            pl.BlockSpec((CONV_WIDTH, D_CONV), const),
            pl.BlockSpec((1, D_CONV), const),
            pl.BlockSpec((1, LANES), const),
            pl.BlockSpec((1, LANES), const),
            pl.BlockSpec((1, D_HEADS), const),
            pl.BlockSpec((1, D_HEADS), const),
            pl.BlockSpec((T_SSD, 3 * T_SSD), const),
        ],
        out_specs=pl.BlockSpec((T_SSD, D_HEADS), row),
        out_shape=jax.ShapeDtypeStruct((batch * seq, D_HEADS), BF16),
        scratch_shapes=[
            pltpu.VMEM((T_SSD + CONV_TAIL, D_CONV), F32),
            pltpu.VMEM((N_PAIRS, SSD_STATE, LANES), F32),
        ],
        compiler_params=pltpu.CompilerParams(
            dimension_semantics=("arbitrary", "arbitrary"), vmem_limit_bytes=VMEM_LIMIT),
        name="ssd",
    )(z, xbc, dt, conv_w, conv_b, dtb, a_row, dskip, gain, ltri3)


def _attn_kernel(q_ref, k_ref, v_ref, w2_ref, gain_ref, o_ref, vst_ref, r_ref, acc_ref):
    t = T_ATT
    i = pl.program_id(2)
    n_blocks = vst_ref.shape[0]

    lane_v = lax.broadcasted_iota(jnp.int32, (t, LANES), 1)
    first_v = lane_v < HEAD_DIM

    @pl.when(i == 0)
    def _():
        for j in range(n_blocks):
            vb = v_ref[j * t:(j + 1) * t, :]
            zero = jnp.zeros_like(vb)
            vst_ref[j, 0:t, :] = jnp.where(first_v, vb, zero)
            vst_ref[j, t:2 * t, :] = jnp.where(first_v, zero, vb)

    q = q_ref[...]
    qz = jnp.zeros_like(q)
    q2 = jnp.concatenate([jnp.where(first_v, q, qz), jnp.where(first_v, qz, q)], axis=0)

    def tile(j, strict):
        off = pl.multiple_of(j * t, t)
        z = _dot_nt(q2, k_ref[pl.ds(off, t), :])
        sp = jnp.maximum(z, 0.0) + jnp.log(1.0 + jnp.exp(-jnp.abs(z)))
        if strict is not None:
            sp = jnp.where(strict, sp, 0.0)
        l1 = -sp
        hi = l1.astype(BF16)
        lo = (l1 - hi.astype(F32)).astype(BF16)
        res = _dot(jnp.concatenate([hi, lo], axis=1), w2_ref[...])
        r_old = r_ref[...]
        after = res[:, 0:t] + r_old
        r_ref[...] = r_old + res[:, t:2 * t]
        w = jnp.exp(z - sp + after)
        if strict is not None:
            w = jnp.where(strict, w, 0.0)
        w = w.astype(BF16)
        wcat = jnp.concatenate([w[0:t, :], w[t:2 * t, :]], axis=1)
        return _dot(wcat, vst_ref[j])

    ti = lax.broadcasted_iota(jnp.int32, (2 * t, t), 0)
    si = lax.broadcasted_iota(jnp.int32, (2 * t, t), 1)
    strict = si < jnp.where(ti >= t, ti - t, ti)
    r_ref[...] = jnp.zeros(r_ref.shape, F32)
    acc_ref[...] = tile(i, strict)

    def body(step, carry):
        acc_ref[...] += tile(i - 1 - step, None)
        return carry

    lax.fori_loop(0, i, body, 0)

    o = acc_ref[...]
    o2 = o * o
    ss_a = jnp.sum(jnp.where(first_v, o2, 0.0), axis=-1, keepdims=True)
    ss_b = jnp.sum(jnp.where(first_v, 0.0, o2), axis=-1, keepdims=True)
    ms = jnp.where(first_v, ss_a, ss_b) * (1.0 / HEAD_DIM)
    o_ref[...] = (o * lax.rsqrt(ms + EPS) * gain_ref[...]).astype(BF16)


def _attention(q, k, v, w2, gain, batch, seq):
    n_q = seq // T_ATT
    return pl.pallas_call(
        _attn_kernel,
        grid=(batch, N_PAIRS, n_q),
        in_specs=[
            pl.BlockSpec((T_ATT, LANES), lambda b, p, i: (b * n_q + i, p)),
            pl.BlockSpec((seq, LANES), lambda b, p, i: (b, p)),
            pl.BlockSpec((seq, LANES), lambda b, p, i: (b, p)),
            pl.BlockSpec((2 * T_ATT, 2 * T_ATT), lambda b, p, i: (0, 0)),
            pl.BlockSpec((1, LANES), lambda b, p, i: (0, p)),
        ],
        out_specs=pl.BlockSpec((T_ATT, LANES), lambda b, p, i: (b * n_q + i, p)),
        out_shape=jax.ShapeDtypeStruct((batch * seq, D_HEADS), BF16),
        scratch_shapes=[
            pltpu.VMEM((n_q, 2 * T_ATT, LANES), BF16),
            pltpu.VMEM((2 * T_ATT, T_ATT), F32),
            pltpu.VMEM((T_ATT, LANES), F32),
        ],
        compiler_params=pltpu.CompilerParams(
            dimension_semantics=("arbitrary", "arbitrary", "arbitrary"),
            vmem_limit_bytes=VMEM_LIMIT),
        name="sb_attention",
    )(q, k, v, w2, gain)


def _rms(x, g):
    ms = jnp.mean(x * x, axis=-1, keepdims=True)
    return x * lax.rsqrt(ms + EPS) * g


def _ffn_kernel(x_ref, yssd_ref, ysb_ref, wo_ref, g_post_ref, g_pre_ref, wg_ref, wu_ref,
                wd_ref, g_out_ref, o_ref, acc_ref):
    mix = _dot(yssd_ref[...], wo_ref[0:D_HEADS, :]) + _dot(ysb_ref[...], wo_ref[D_HEADS:2 * D_HEADS, :])
    x1 = x_ref[...] + _rms(mix, g_post_ref[...])
    h = _rms(x1, g_pre_ref[...]).astype(BF16)
    n_chunks = wg_ref.shape[0]

    def body(c, carry):
        gate = _dot(h, wg_ref[c])
        up = _dot(h, wu_ref[c])
        act = (_silu(gate) * up).astype(BF16)
        contrib = _dot(act, wd_ref[c])

        @pl.when(c == 0)
        def _():
            acc_ref[...] = contrib

        @pl.when(c > 0)
        def _():
            acc_ref[...] += contrib

        return carry

    lax.fori_loop(0, n_chunks, body, 0)
    o_ref[...] = x1 + _rms(acc_ref[...], g_out_ref[...])


def _out_ffn(x2, y_ssd, y_sb, w_out, g_post, g_pre, wg, wu, wd, g_out):
    m = x2.shape[0]
    n_chunks = wg.shape[0]
    row = lambda i: (i, 0)
    const2 = lambda i: (0, 0)
    const3 = lambda i: (0, 0, 0)
    single = pl.Buffered(1)
    return pl.pallas_call(
        _ffn_kernel,
        grid=(m // TM_FFN,),
        in_specs=[
            pl.BlockSpec((TM_FFN, D_MODEL), row),
            pl.BlockSpec((TM_FFN, D_HEADS), row),
            pl.BlockSpec((TM_FFN, D_HEADS), row),
            pl.BlockSpec((2 * D_HEADS, D_MODEL), const2, pipeline_mode=single),
            pl.BlockSpec((1, D_MODEL), const2),
            pl.BlockSpec((1, D_MODEL), const2),
            pl.BlockSpec((n_chunks, D_MODEL, FF_CHUNK), const3, pipeline_mode=single),
            pl.BlockSpec((n_chunks, D_MODEL, FF_CHUNK), const3, pipeline_mode=single),
            pl.BlockSpec((n_chunks, FF_CHUNK, D_MODEL), const3, pipeline_mode=single),
            pl.BlockSpec((1, D_MODEL), const2),
        ],
        out_specs=pl.BlockSpec((TM_FFN, D_MODEL), row),
        out_shape=jax.ShapeDtypeStruct((m, D_MODEL), F32),
        scratch_shapes=[pltpu.VMEM((TM_FFN, D_MODEL), F32)],
        compiler_params=pltpu.CompilerParams(
            dimension_semantics=("arbitrary",), vmem_limit_bytes=VMEM_LIMIT),
        name="out_ffn",
    )(x2, y_ssd, y_sb, w_out, g_post, g_pre, wg, wu, wd, g_out)


def _expand_heads(v):
    return jnp.repeat(v.astype(F32), HEAD_DIM)[None, :]


def _pad_lanes(v):
    return jnp.pad(v.astype(F32), (0, LANES - v.shape[0]))[None, :]


def _layer(x2, batch, seq, pre_mix_gain, w_in, conv_w, conv_b, dt_bias, a_log, d_skip,
           ssd_norm_gain, sb_norm_gain, w_out, post_mix_gain, pre_ffn_gain, w_gate, w_up,
           w_down, post_ffn_gain):
    o_xbc, o_dt = D_HEADS, D_HEADS + D_CONV
    o_q = o_dt + N_HEADS
    scale = 1.0 / math.sqrt(HEAD_DIM)
    w_all = jnp.concatenate([
        w_in[:, 0:o_dt],
        jnp.pad(w_in[:, o_dt:o_q], ((0, 0), (0, LANES - N_HEADS))),
        w_in[:, o_q:o_q + D_HEADS] * scale,
        w_in[:, o_q + D_HEADS:],
    ], axis=1).astype(BF16)

    z, xbc, dt, q, k, v = _inproj(x2, pre_mix_gain[None, :], w_all)

    tri = jnp.tril(jnp.ones((T_SSD, T_SSD), BF16))
    ltri3 = jnp.concatenate([tri, tri, tri], axis=1)
    a_row = _pad_lanes(-jnp.exp(a_log.astype(F32)))
    y_ssd = _ssd(z, xbc, dt, conv_w, conv_b[None, :], _pad_lanes(dt_bias), a_row,
                 _expand_heads(d_skip), ssd_norm_gain[None, :], ltri3, batch, seq)

    jj = jnp.arange(T_ATT)
    later = (jj[:, None] > jj[None, :]).astype(BF16)
    half = jnp.concatenate([later, jnp.ones((T_ATT, T_ATT), BF16)], axis=1)
    w2 = jnp.concatenate([half, half], axis=0)
    y_sb = _attention(q, k, v, w2, sb_norm_gain[None, :], batch, seq)

    n_chunks = D_FF // FF_CHUNK
    wg = w_gate.astype(BF16).reshape(D_MODEL, n_chunks, FF_CHUNK).transpose(1, 0, 2)
    wu = w_up.astype(BF16).reshape(D_MODEL, n_chunks, FF_CHUNK).transpose(1, 0, 2)
    wd = w_down.astype(BF16).reshape(n_chunks, FF_CHUNK, D_MODEL)
    return _out_ffn(x2, y_ssd, y_sb, w_out.astype(BF16), post_mix_gain[None, :],
                    pre_ffn_gain[None, :], wg, wu, wd, post_ffn_gain[None, :])


def kernel(x, pre_mix_gain, w_in, conv_w, conv_b, dt_bias, a_log, d_skip, ssd_norm_gain,
           sb_norm_gain, w_out, post_mix_gain, pre_ffn_gain, w_gate, w_up, w_down,
           post_ffn_gain):
    batch, seq, d = x.shape
    x2 = x.reshape(batch * seq, d)
    params = (pre_mix_gain, w_in, conv_w, conv_b, dt_bias, a_log, d_skip, ssd_norm_gain,
              sb_norm_gain, w_out, post_mix_gain, pre_ffn_gain, w_gate, w_up, w_down,
              post_ffn_gain)
    for layer in range(pre_mix_gain.shape[0]):
        x2 = _layer(x2, batch, seq, *(p[layer] for p in params))
    return x2.reshape(batch, seq, d)
```

```python
import functools
import math

import jax
import jax.numpy as jnp
from jax import lax
from jax.experimental import pallas as pl
from jax.experimental.pallas import tpu as pltpu

F32 = jnp.float32
BF16 = jnp.bfloat16

EPS = 1e-6
LANES = 128

D_MODEL = 1024
N_HEADS = 8
HEAD_DIM = 64
D_HEADS = N_HEADS * HEAD_DIM
N_PAIRS = N_HEADS // 2
SSD_GROUPS = 2
SSD_STATE = 128
CONV_WIDTH = 4
D_CONV = D_HEADS + 2 * SSD_GROUPS * SSD_STATE
D_FF = 2816

C_Z = 0
C_XBC = C_Z + D_HEADS
C_DT = C_XBC + D_CONV
C_Q = C_DT + LANES
C_K = C_Q + D_HEADS
C_V = C_K + D_HEADS
C_END = C_V + D_HEADS

TM_PROJ = 512
T_SSD = 128
T_ATT = 128
TM_FFN = 512
FF_CHUNK = 256
CONV_TAIL = 8

VMEM_LIMIT = 56 * 1024 * 1024


def _dot(a, b):
    return jnp.dot(a, b, preferred_element_type=F32)


def _dot_nt(a, b):
    return lax.dot_general(a, b, (((1,), (1,)), ((), ())), preferred_element_type=F32)


def _dot_tn(a, b):
    return lax.dot_general(a, b, (((0,), (0,)), ((), ())), preferred_element_type=F32)


def _split3(x):
    hi = x.astype(BF16)
    r = x - hi.astype(F32)
    mid = r.astype(BF16)
    lo = (r - mid.astype(F32)).astype(BF16)
    return hi, mid, lo


def _silu(x):
    return x / (1.0 + jnp.exp(-x))


def _softplus(x):
    return jnp.maximum(x, 0.0) + jnp.log1p(jnp.exp(-jnp.abs(x)))


def _inproj_kernel(x_ref, g_ref, w_ref, z_ref, xbc_ref, dt_ref, q_ref, k_ref, v_ref):
    x = x_ref[...]
    ms = jnp.mean(x * x, axis=-1, keepdims=True)
    h = (x * lax.rsqrt(ms + EPS) * g_ref[...]).astype(BF16)
    z_ref[...] = _dot(h, w_ref[:, C_Z:C_XBC])
    xbc_ref[...] = _dot(h, w_ref[:, C_XBC:C_DT])
    dt_ref[...] = _dot(h, w_ref[:, C_DT:C_Q])
    q_ref[...] = _dot(h, w_ref[:, C_Q:C_K]).astype(BF16)
    k_ref[...] = _dot(h, w_ref[:, C_K:C_V]).astype(BF16)
    v_ref[...] = _dot(h, w_ref[:, C_V:C_END]).astype(BF16)


def _inproj(x2, gain, w_all):
    m = x2.shape[0]
    row = lambda i: (i, 0)
    const = lambda i: (0, 0)
    return pl.pallas_call(
        _inproj_kernel,
        grid=(m // TM_PROJ,),
        in_specs=[
            pl.BlockSpec((TM_PROJ, D_MODEL), row),
            pl.BlockSpec((1, D_MODEL), const),
            pl.BlockSpec((D_MODEL, C_END), const, pipeline_mode=pl.Buffered(1)),
        ],
        out_specs=[
            pl.BlockSpec((TM_PROJ, D_HEADS), row),
            pl.BlockSpec((TM_PROJ, D_CONV), row),
            pl.BlockSpec((TM_PROJ, LANES), row),
            pl.BlockSpec((TM_PROJ, D_HEADS), row),
            pl.BlockSpec((TM_PROJ, D_HEADS), row),
            pl.BlockSpec((TM_PROJ, D_HEADS), row),
        ],
        out_shape=[
            jax.ShapeDtypeStruct((m, D_HEADS), F32),
            jax.ShapeDtypeStruct((m, D_CONV), F32),
            jax.ShapeDtypeStruct((m, LANES), F32),
            jax.ShapeDtypeStruct((m, D_HEADS), BF16),
            jax.ShapeDtypeStruct((m, D_HEADS), BF16),
            jax.ShapeDtypeStruct((m, D_HEADS), BF16),
        ],
        compiler_params=pltpu.CompilerParams(
            dimension_semantics=("arbitrary",), vmem_limit_bytes=VMEM_LIMIT),
        name="inproj",
    )(x2, gain, w_all)


def _ssd_kernel(z_ref, xbc_ref, dt_ref, cw_ref, cb_ref, dtb_ref, a_ref, dskip_ref,
                gain_ref, ltri_ref, y_ref, ext_ref, state_ref):
    t = T_SSD
    c = pl.program_id(1)

    @pl.when(c == 0)
    def _():
        ext_ref[0:CONV_TAIL, :] = jnp.zeros((CONV_TAIL, D_CONV), F32)
        state_ref[...] = jnp.zeros(state_ref.shape, F32)

    ext_ref[CONV_TAIL:CONV_TAIL + t, :] = xbc_ref[...]
    conv = cb_ref[...]
    for k in range(CONV_WIDTH):
        off = CONV_TAIL - (CONV_WIDTH - 1) + k
        conv = conv + ext_ref[off:off + t, :] * cw_ref[k:k + 1, :]
    ext_ref[0:CONV_TAIL, :] = ext_ref[t:t + CONV_TAIL, :]
    xa = _silu(conv)

    dtv = _softplus(dt_ref[...] + dtb_ref[...])
    adt = dtv * a_ref[...]
    hi, mid, lo = _split3(adt)
    acs = _dot(ltri_ref[...], jnp.concatenate([hi, mid, lo], axis=0))
    acs_t = acs.T

    lane = lax.broadcasted_iota(jnp.int32, (t, LANES), 1)
    first_head = lane < HEAD_DIM
    li = lax.broadcasted_iota(jnp.int32, (t, t), 0)
    si = lax.broadcasted_iota(jnp.int32, (t, t), 1)
    causal = li >= si

    def col(v, h):
        return jnp.broadcast_to(v[:, h:h + 1], (t, LANES))

    y_blocks = []
    cb_mats = []
    for g in range(SSD_GROUPS):
        bm = xa[:, D_HEADS + g * SSD_STATE:D_HEADS + (g + 1) * SSD_STATE].astype(BF16)
        cm = xa[:, D_HEADS + (SSD_GROUPS + g) * SSD_STATE:
                D_HEADS + (SSD_GROUPS + g + 1) * SSD_STATE].astype(BF16)
        cb_mats.append((bm, cm, _dot_nt(cm, bm)))

    for p in range(N_PAIRS):
        ha, hb = 2 * p, 2 * p + 1
        bm, cm, cbm = cb_mats[p // (N_PAIRS // SSD_GROUPS)]
        x2 = xa[:, p * LANES:(p + 1) * LANES]
        dt2 = jnp.where(first_head, col(dtv, ha), col(dtv, hb))
        acs2 = jnp.where(first_head, col(acs, ha), col(acs, hb))
        xdt2 = x2 * dt2

        def decay(h):
            seg = col(acs, h) - jnp.broadcast_to(acs_t[h:h + 1, :], (t, t))
            return (cbm * jnp.exp(jnp.where(causal, seg, -jnp.inf))).astype(BF16)

        m2 = jnp.concatenate([decay(ha), decay(hb)], axis=1)
        xdt_a = jnp.where(first_head, xdt2, 0.0).astype(BF16)
        xdt_b = jnp.where(first_head, 0.0, xdt2).astype(BF16)
        y_diag = _dot(m2, jnp.concatenate([xdt_a, xdt_b], axis=0))

        prev = state_ref[p]
        y_off = _dot(cm, prev.astype(BF16)) * jnp.exp(acs2)
        last = acs2[t - 1:t, :]
        xs = (xdt2 * jnp.exp(last - acs2)).astype(BF16)
        state_ref[p] = prev * jnp.exp(last) + _dot_tn(bm, xs)

        y_blocks.append(y_diag + y_off + dskip_ref[:, p * LANES:(p + 1) * LANES] * x2)

    per_group = N_PAIRS // SSD_GROUPS
    for g in range(SSD_GROUPS):
        ys = []
        for p in range(g * per_group, (g + 1) * per_group):
            ys.append(y_blocks[p] * _silu(z_ref[:, p * LANES:(p + 1) * LANES]))
        ss = sum(jnp.sum(y * y, axis=-1, keepdims=True) for y in ys)
        inv = lax.rsqrt(ss * (1.0 / (per_group * LANES)) + EPS)
        for j, y in enumerate(ys):
            p = g * per_group + j
            y_ref[:, p * LANES:(p + 1) * LANES] = (
                y * inv * gain_ref[:, p * LANES:(p + 1) * LANES]).astype(BF16)


def _ssd(z, xbc, dt, conv_w, conv_b, dtb, a_row, dskip, gain, ltri3, batch, seq):
    n_chunks = seq // T_SSD
    row = lambda b, c: (b * n_chunks + c, 0)
    const = lambda b, c: (0, 0)
    return pl.pallas_call(
        _ssd_kernel,
        grid=(batch, n_chunks),
        in_specs=[
            pl.BlockSpec((T_SSD, D_HEADS), row),
            pl.BlockSpec((T_SSD, D_CONV), row),
            pl.BlockSpec((T_SSD, LANES), row),
            pl.BlockSpec((CONV_WIDTH, D_CONV), const),
            pl.BlockSpec((1, D_CONV), const),
            pl.BlockSpec((1, LANES), const),
            pl.BlockSpec((1, LANES), const),
            pl.BlockSpec((1, D_HEADS), const),
            pl.BlockSpec((1, D_HEADS), const),
            pl.BlockSpec((T_SSD, 3 * T_SSD), const),
        ],
        out_specs=pl.BlockSpec((T_SSD, D_HEADS), row),
        out_shape=jax.ShapeDtypeStruct((batch * seq, D_HEADS), BF16),
        scratch_shapes=[
            pltpu.VMEM((T_SSD + CONV_TAIL, D_CONV), F32),
            pltpu.VMEM((N_PAIRS, SSD_STATE, LANES), F32),
        ],
        compiler_params=pltpu.CompilerParams(
            dimension_semantics=("arbitrary", "arbitrary"), vmem_limit_bytes=VMEM_LIMIT),
        name="ssd",
    )(z, xbc, dt, conv_w, conv_b, dtb, a_row, dskip, gain, ltri3)


LOG2E = 1.4426950408889634
ATT_DEPTH = 4


def _attn_kernel(ti_ref, tj_ref, q_ref, k_ref, v_ref, w2_ref, gain_ref, o_ref,
                 q2_ref, vst_ref, r_ref, acc_ref, z_ref, arg_ref):
    t = T_ATT
    n_blocks = q2_ref.shape[0]
    depth = ATT_DEPTH

    lane_v = lax.broadcasted_iota(jnp.int32, (t, LANES), 1)
    first_v = lane_v < HEAD_DIM

    for j in range(n_blocks):
        qb = q_ref[j * t:(j + 1) * t, :]
        vb = v_ref[j * t:(j + 1) * t, :]
        zero = jnp.zeros_like(qb)
        q2_ref[j, 0:t, :] = jnp.where(first_v, qb, zero)
        q2_ref[j, t:2 * t, :] = jnp.where(first_v, zero, qb)
        vst_ref[j, 0:t, :] = jnp.where(first_v, vb, zero)
        vst_ref[j, t:2 * t, :] = jnp.where(first_v, zero, vb)

    ti = lax.broadcasted_iota(jnp.int32, (2 * t, t), 0)
    si = lax.broadcasted_iota(jnp.int32, (2 * t, t), 1)
    strict = si < jnp.where(ti >= t, ti - t, ti)

    def key_rows(j):
        off = j * t
        return pl.ds(off if isinstance(off, int) else pl.multiple_of(off, t), t)

    def stage_scores(tiles, z_buf):
        for u, (i, j) in enumerate(tiles):
            z_buf[u] = _dot_nt(q2_ref[i], k_ref[key_rows(j), :])

    def stage_suffix_sums(tiles, z_buf, diag):
        for u, (i, _) in enumerate(tiles):
            zn = z_buf[u] * (-LOG2E)
            l1 = jnp.minimum(zn, 0.0) - jnp.log2(1.0 + jnp.exp2(-jnp.abs(zn)))
            zl = l1 - zn
            if diag:
                l1 = jnp.where(strict, l1, 0.0)
                zl = jnp.where(strict, zl, -jnp.inf)
            hi = l1.astype(BF16)
            lo = (l1 - hi.astype(F32)).astype(BF16)
            res = _dot(jnp.concatenate([hi, lo], axis=1), w2_ref[...])
            arg = zl + res[:, 0:t]
            tot = res[:, t:2 * t]
            if not diag:
                r_old = r_ref[i]
                arg = arg + r_old
                tot = tot + r_old
            arg_ref[u] = arg
            r_ref[i] = tot

    def stage_values(tiles, diag):
        for u, (i, j) in enumerate(tiles):
            w = jnp.exp2(arg_ref[u]).astype(BF16)
            contrib = _dot(jnp.concatenate([w[0:t, :], w[t:2 * t, :]], axis=1), vst_ref[j])
            if diag:
                acc_ref[i] = contrib
            else:
                acc_ref[i] += contrib

    def sweep(n_tiles, tile_of, diag):
        n_iter = n_tiles // depth
        group = lambda g: [tile_of(g * depth + u) for u in range(depth)]

        def iteration(m, parity, scores=True, suffix=True, values=True):
            if scores:
                stage_scores(group(m), z_ref.at[parity])
            if values:
                stage_values(group(m - 2), diag)
            if suffix:
                stage_suffix_sums(group(m - 1), z_ref.at[1 - parity], diag)

        iteration(0, 0, suffix=False, values=False)
        iteration(1, 1, values=False)

        def body(mm, carry):
            iteration(2 * mm, 0)
            iteration(2 * mm + 1, 1)
            return carry

        lax.fori_loop(1, n_iter // 2, body, 0)
        iteration(n_iter, 0, scores=False)
        iteration(n_iter + 1, 1, scores=False, suffix=False)

    sweep(n_blocks, lambda n: (n, n), True)
    sweep(ti_ref.shape[0], lambda n: (ti_ref[n], tj_ref[n]), False)

    gain = gain_ref[...]
    for i in range(n_blocks):
        o = acc_ref[i]
        o2 = o * o
        ss_a = jnp.sum(jnp.where(first_v, o2, 0.0), axis=-1, keepdims=True)
        ss_b = jnp.sum(jnp.where(first_v, 0.0, o2), axis=-1, keepdims=True)
        ms = jnp.where(first_v, ss_a, ss_b) * (1.0 / HEAD_DIM)
        o_ref[i * t:(i + 1) * t, :] = (o * lax.rsqrt(ms + EPS) * gain).astype(BF16)


def _attention(q, k, v, w2, gain, batch, seq):
    n_q = seq // T_ATT
    tiles = [(i, j) for j in range(n_q - 2, -1, -1) for i in range(j + 1, n_q)]
    for n_tiles in (n_q, len(tiles)):
        assert n_tiles % (2 * ATT_DEPTH) == 0
    ti = jnp.asarray([i for i, _ in tiles], jnp.int32)
    tj = jnp.asarray([j for _, j in tiles], jnp.int32)
    seq_blk = lambda b, p, *_: (b, p)
    return pl.pallas_call(
        _attn_kernel,
        grid_spec=pltpu.PrefetchScalarGridSpec(
            num_scalar_prefetch=2,
            grid=(batch, N_PAIRS),
            in_specs=[
                pl.BlockSpec((seq, LANES), seq_blk),
                pl.BlockSpec((seq, LANES), seq_blk),
                pl.BlockSpec((seq, LANES), seq_blk),
                pl.BlockSpec((2 * T_ATT, 2 * T_ATT), lambda b, p, *_: (0, 0)),
                pl.BlockSpec((1, LANES), lambda b, p, *_: (0, p)),
            ],
            out_specs=pl.BlockSpec((seq, LANES), seq_blk),
            scratch_shapes=[
                pltpu.VMEM((n_q, 2 * T_ATT, LANES), BF16),
                pltpu.VMEM((n_q, 2 * T_ATT, LANES), BF16),
                pltpu.VMEM((n_q, 2 * T_ATT, T_ATT), F32),
                pltpu.VMEM((n_q, T_ATT, LANES), F32),
                pltpu.VMEM((2, ATT_DEPTH, 2 * T_ATT, T_ATT), F32),
                pltpu.VMEM((ATT_DEPTH, 2 * T_ATT, T_ATT), F32),
            ],
        ),
        out_shape=jax.ShapeDtypeStruct((batch * seq, D_HEADS), BF16),
        compiler_params=pltpu.CompilerParams(
            dimension_semantics=("arbitrary", "arbitrary"),
            vmem_limit_bytes=VMEM_LIMIT),
        name="sb_attention",
    )(ti, tj, q, k, v, w2, gain)


def _rms(x, g):
    ms = jnp.mean(x * x, axis=-1, keepdims=True)
    return x * lax.rsqrt(ms + EPS) * g


def _ffn_kernel(x_ref, yssd_ref, ysb_ref, wo_ref, g_post_ref, g_pre_ref, wg_ref, wu_ref,
                wd_ref, g_out_ref, o_ref, acc_ref):
    mix = _dot(yssd_ref[...], wo_ref[0:D_HEADS, :]) + _dot(ysb_ref[...], wo_ref[D_HEADS:2 * D_HEADS, :])
    x1 = x_ref[...] + _rms(mix, g_post_ref[...])
    h = _rms(x1, g_pre_ref[...]).astype(BF16)
    n_chunks = wg_ref.shape[0]

    def body(c, carry):
        gate = _dot(h, wg_ref[c])
        up = _dot(h, wu_ref[c])
        act = (_silu(gate) * up).astype(BF16)
        contrib = _dot(act, wd_ref[c])

        @pl.when(c == 0)
        def _():
            acc_ref[...] = contrib

        @pl.when(c > 0)
        def _():
            acc_ref[...] += contrib

        return carry

    lax.fori_loop(0, n_chunks, body, 0)
    o_ref[...] = x1 + _rms(acc_ref[...], g_out_ref[...])


def _out_ffn(x2, y_ssd, y_sb, w_out, g_post, g_pre, wg, wu, wd, g_out):
    m = x2.shape[0]
    n_chunks = wg.shape[0]
    row = lambda i: (i, 0)
    const2 = lambda i: (0, 0)
    const3 = lambda i: (0, 0, 0)
    single = pl.Buffered(1)
    return pl.pallas_call(
        _ffn_kernel,
        grid=(m // TM_FFN,),
        in_specs=[
            pl.BlockSpec((TM_FFN, D_MODEL), row),
            pl.BlockSpec((TM_FFN, D_HEADS), row),
            pl.BlockSpec((TM_FFN, D_HEADS), row),
            pl.BlockSpec((2 * D_HEADS, D_MODEL), const2, pipeline_mode=single),
            pl.BlockSpec((1, D_MODEL), const2),
            pl.BlockSpec((1, D_MODEL), const2),
            pl.BlockSpec((n_chunks, D_MODEL, FF_CHUNK), const3, pipeline_mode=single),
            pl.BlockSpec((n_chunks, D_MODEL, FF_CHUNK), const3, pipeline_mode=single),
            pl.BlockSpec((n_chunks, FF_CHUNK, D_MODEL), const3, pipeline_mode=single),
            pl.BlockSpec((1, D_MODEL), const2),
        ],
        out_specs=pl.BlockSpec((TM_FFN, D_MODEL), row),
        out_shape=jax.ShapeDtypeStruct((m, D_MODEL), F32),
        scratch_shapes=[pltpu.VMEM((TM_FFN, D_MODEL), F32)],
        compiler_params=pltpu.CompilerParams(
            dimension_semantics=("arbitrary",), vmem_limit_bytes=VMEM_LIMIT),
        name="out_ffn",
    )(x2, y_ssd, y_sb, w_out, g_post, g_pre, wg, wu, wd, g_out)


def _expand_heads(v):
    return jnp.repeat(v.astype(F32), HEAD_DIM)[None, :]


def _pad_lanes(v):
    return jnp.pad(v.astype(F32), (0, LANES - v.shape[0]))[None, :]


def _layer(x2, batch, seq, pre_mix_gain, w_in, conv_w, conv_b, dt_bias, a_log, d_skip,
           ssd_norm_gain, sb_norm_gain, w_out, post_mix_gain, pre_ffn_gain, w_gate, w_up,
           w_down, post_ffn_gain):
    o_xbc, o_dt = D_HEADS, D_HEADS + D_CONV
    o_q = o_dt + N_HEADS
    scale = 1.0 / math.sqrt(HEAD_DIM)
    w_all = jnp.concatenate([
        w_in[:, 0:o_dt],
        jnp.pad(w_in[:, o_dt:o_q], ((0, 0), (0, LANES - N_HEADS))),
        w_in[:, o_q:o_q + D_HEADS] * scale,
        w_in[:, o_q + D_HEADS:],
    ], axis=1).astype(BF16)

    z, xbc, dt, q, k, v = _inproj(x2, pre_mix_gain[None, :], w_all)

    tri = jnp.tril(jnp.ones((T_SSD, T_SSD), BF16))
    ltri3 = jnp.concatenate([tri, tri, tri], axis=1)
    a_row = _pad_lanes(-jnp.exp(a_log.astype(F32)))
    y_ssd = _ssd(z, xbc, dt, conv_w, conv_b[None, :], _pad_lanes(dt_bias), a_row,
                 _expand_heads(d_skip), ssd_norm_gain[None, :], ltri3, batch, seq)

    jj = jnp.arange(T_ATT)
    later = (jj[:, None] > jj[None, :]).astype(BF16)
    half = jnp.concatenate([later, jnp.ones((T_ATT, T_ATT), BF16)], axis=1)
    w2 = jnp.concatenate([half, half], axis=0)
    y_sb = _attention(q, k, v, w2, sb_norm_gain[None, :], batch, seq)

    n_chunks = D_FF // FF_CHUNK
    wg = w_gate.astype(BF16).reshape(D_MODEL, n_chunks, FF_CHUNK).transpose(1, 0, 2)
    wu = w_up.astype(BF16).reshape(D_MODEL, n_chunks, FF_CHUNK).transpose(1, 0, 2)
    wd = w_down.astype(BF16).reshape(n_chunks, FF_CHUNK, D_MODEL)
    return _out_ffn(x2, y_ssd, y_sb, w_out.astype(BF16), post_mix_gain[None, :],
                    pre_ffn_gain[None, :], wg, wu, wd, post_ffn_gain[None, :])


def kernel(x, pre_mix_gain, w_in, conv_w, conv_b, dt_bias, a_log, d_skip, ssd_norm_gain,
           sb_norm_gain, w_out, post_mix_gain, pre_ffn_gain, w_gate, w_up, w_down,
           post_ffn_gain):
    batch, seq, d = x.shape
    x2 = x.reshape(batch * seq, d)
    params = (pre_mix_gain, w_in, conv_w, conv_b, dt_bias, a_log, d_skip, ssd_norm_gain,
              sb_norm_gain, w_out, post_mix_gain, pre_ffn_gain, w_gate, w_up, w_down,
              post_ffn_gain)
    for layer in range(pre_mix_gain.shape[0]):
        x2 = _layer(x2, batch, seq, *(p[layer] for p in params))
    return x2.reshape(batch, seq, d)
```

```python
import functools
import math

import jax
import jax.numpy as jnp
from jax import lax
from jax.experimental import pallas as pl
from jax.experimental.pallas import tpu as pltpu

F32 = jnp.float32
BF16 = jnp.bfloat16

EPS = 1e-6
LANES = 128

D_MODEL = 1024
N_HEADS = 8
HEAD_DIM = 64
D_HEADS = N_HEADS * HEAD_DIM
N_PAIRS = N_HEADS // 2
SSD_GROUPS = 2
SSD_STATE = 128
CONV_WIDTH = 4
D_CONV = D_HEADS + 2 * SSD_GROUPS * SSD_STATE
D_FF = 2816

C_Z = 0
C_XBC = C_Z + D_HEADS
C_DT = C_XBC + D_CONV
C_Q = C_DT + LANES
C_K = C_Q + D_HEADS
C_V = C_K + D_HEADS
C_END = C_V + D_HEADS

TM_PROJ = 512
T_SSD = 128
T_ATT = 128
TM_FFN = 512
SUB_FFN = 256
FF_CHUNK = 256
CONV_TAIL = 8

VMEM_LIMIT = 56 * 1024 * 1024


def _dot(a, b):
    return jnp.dot(a, b, preferred_element_type=F32)


def _dot_nt(a, b):
    return lax.dot_general(a, b, (((1,), (1,)), ((), ())), preferred_element_type=F32)


def _dot_tn(a, b):
    return lax.dot_general(a, b, (((0,), (0,)), ((), ())), preferred_element_type=F32)


def _split3(x):
    hi = x.astype(BF16)
    r = x - hi.astype(F32)
    mid = r.astype(BF16)
    lo = (r - mid.astype(F32)).astype(BF16)
    return hi, mid, lo


def _silu(x):
    return x / (1.0 + jnp.exp(-x))


def _softplus(x):
    return jnp.maximum(x, 0.0) + jnp.log1p(jnp.exp(-jnp.abs(x)))


def _inproj_kernel(x_ref, g_ref, w_ref, z_ref, xbc_ref, dt_ref, q_ref, k_ref, v_ref):
    x = x_ref[...]
    ms = jnp.mean(x * x, axis=-1, keepdims=True)
    h = (x * lax.rsqrt(ms + EPS) * g_ref[...]).astype(BF16)
    z_ref[...] = _dot(h, w_ref[:, C_Z:C_XBC])
    xbc_ref[...] = _dot(h, w_ref[:, C_XBC:C_DT])
    dt_ref[...] = _dot(h, w_ref[:, C_DT:C_Q])
    q_ref[...] = _dot(h, w_ref[:, C_Q:C_K]).astype(BF16)
    k_ref[...] = _dot(h, w_ref[:, C_K:C_V]).astype(BF16)
    v_ref[...] = _dot(h, w_ref[:, C_V:C_END]).astype(BF16)


def _inproj(x2, gain, w_all):
    m = x2.shape[0]
    row = lambda i: (i, 0)
    const = lambda i: (0, 0)
    return pl.pallas_call(
        _inproj_kernel,
        grid=(m // TM_PROJ,),
        in_specs=[
            pl.BlockSpec((TM_PROJ, D_MODEL), row),
            pl.BlockSpec((1, D_MODEL), const),
            pl.BlockSpec((D_MODEL, C_END), const, pipeline_mode=pl.Buffered(1)),
        ],
        out_specs=[
            pl.BlockSpec((TM_PROJ, D_HEADS), row),
            pl.BlockSpec((TM_PROJ, D_CONV), row),
            pl.BlockSpec((TM_PROJ, LANES), row),
            pl.BlockSpec((TM_PROJ, D_HEADS), row),
            pl.BlockSpec((TM_PROJ, D_HEADS), row),
            pl.BlockSpec((TM_PROJ, D_HEADS), row),
        ],
        out_shape=[
            jax.ShapeDtypeStruct((m, D_HEADS), F32),
            jax.ShapeDtypeStruct((m, D_CONV), F32),
            jax.ShapeDtypeStruct((m, LANES), F32),
            jax.ShapeDtypeStruct((m, D_HEADS), BF16),
            jax.ShapeDtypeStruct((m, D_HEADS), BF16),
            jax.ShapeDtypeStruct((m, D_HEADS), BF16),
        ],
        compiler_params=pltpu.CompilerParams(
            dimension_semantics=("arbitrary",), vmem_limit_bytes=VMEM_LIMIT),
        name="inproj",
    )(x2, gain, w_all)


def _ssd_kernel(z_ref, xbc_ref, dt_ref, cw_ref, cb_ref, dtb_ref, a_ref, dskip_ref,
                gain_ref, ltri_ref, y_ref, ext_ref, state_ref):
    t = T_SSD
    c = pl.program_id(1)

    @pl.when(c == 0)
    def _():
        ext_ref[0:CONV_TAIL, :] = jnp.zeros((CONV_TAIL, D_CONV), F32)
        state_ref[...] = jnp.zeros(state_ref.shape, F32)

    ext_ref[CONV_TAIL:CONV_TAIL + t, :] = xbc_ref[...]
    conv = cb_ref[...]
    for k in range(CONV_WIDTH):
        off = CONV_TAIL - (CONV_WIDTH - 1) + k
        conv = conv + ext_ref[off:off + t, :] * cw_ref[k:k + 1, :]
    ext_ref[0:CONV_TAIL, :] = ext_ref[t:t + CONV_TAIL, :]
    xa = _silu(conv)

    dtv = _softplus(dt_ref[...] + dtb_ref[...])
    adt = dtv * a_ref[...]
    hi, mid, lo = _split3(adt)
    acs = _dot(ltri_ref[...], jnp.concatenate([hi, mid, lo], axis=0))
    acs_t = acs.T

    lane = lax.broadcasted_iota(jnp.int32, (t, LANES), 1)
    first_head = lane < HEAD_DIM
    li = lax.broadcasted_iota(jnp.int32, (t, t), 0)
    si = lax.broadcasted_iota(jnp.int32, (t, t), 1)
    causal = li >= si

    def col(v, h):
        return jnp.broadcast_to(v[:, h:h + 1], (t, LANES))

    y_blocks = []
    cb_mats = []
    for g in range(SSD_GROUPS):
        bm = xa[:, D_HEADS + g * SSD_STATE:D_HEADS + (g + 1) * SSD_STATE].astype(BF16)
        cm = xa[:, D_HEADS + (SSD_GROUPS + g) * SSD_STATE:
                D_HEADS + (SSD_GROUPS + g + 1) * SSD_STATE].astype(BF16)
        cb_mats.append((bm, cm, _dot_nt(cm, bm)))

    for p in range(N_PAIRS):
        ha, hb = 2 * p, 2 * p + 1
        bm, cm, cbm = cb_mats[p // (N_PAIRS // SSD_GROUPS)]
        x2 = xa[:, p * LANES:(p + 1) * LANES]
        dt2 = jnp.where(first_head, col(dtv, ha), col(dtv, hb))
        acs2 = jnp.where(first_head, col(acs, ha), col(acs, hb))
        xdt2 = x2 * dt2

        def decay(h):
            seg = col(acs, h) - jnp.broadcast_to(acs_t[h:h + 1, :], (t, t))
            return (cbm * jnp.exp(jnp.where(causal, seg, -jnp.inf))).astype(BF16)

        m2 = jnp.concatenate([decay(ha), decay(hb)], axis=1)
        xdt_a = jnp.where(first_head, xdt2, 0.0).astype(BF16)
        xdt_b = jnp.where(first_head, 0.0, xdt2).astype(BF16)
        y_diag = _dot(m2, jnp.concatenate([xdt_a, xdt_b], axis=0))

        prev = state_ref[p]
        y_off = _dot(cm, prev.astype(BF16)) * jnp.exp(acs2)
        last = acs2[t - 1:t, :]
        xs = (xdt2 * jnp.exp(last - acs2)).astype(BF16)
        state_ref[p] = prev * jnp.exp(last) + _dot_tn(bm, xs)

        y_blocks.append(y_diag + y_off + dskip_ref[:, p * LANES:(p + 1) * LANES] * x2)

    per_group = N_PAIRS // SSD_GROUPS
    for g in range(SSD_GROUPS):
        ys = []
        for p in range(g * per_group, (g + 1) * per_group):
            ys.append(y_blocks[p] * _silu(z_ref[:, p * LANES:(p + 1) * LANES]))
        ss = sum(jnp.sum(y * y, axis=-1, keepdims=True) for y in ys)
        inv = lax.rsqrt(ss * (1.0 / (per_group * LANES)) + EPS)
        for j, y in enumerate(ys):
            p = g * per_group + j
            y_ref[:, p * LANES:(p + 1) * LANES] = (
                y * inv * gain_ref[:, p * LANES:(p + 1) * LANES]).astype(BF16)


def _ssd(z, xbc, dt, conv_w, conv_b, dtb, a_row, dskip, gain, ltri3, batch, seq):
    n_chunks = seq // T_SSD
    row = lambda b, c: (b * n_chunks + c, 0)
    const = lambda b, c: (0, 0)
    return pl.pallas_call(
        _ssd_kernel,
        grid=(batch, n_chunks),
        in_specs=[
            pl.BlockSpec((T_SSD, D_HEADS), row),
            pl.BlockSpec((T_SSD, D_CONV), row),
            pl.BlockSpec((T_SSD, LANES), row),
            pl.BlockSpec((CONV_WIDTH, D_CONV), const),
            pl.BlockSpec((1, D_CONV), const),
            pl.BlockSpec((1, LANES), const),
            pl.BlockSpec((1, LANES), const),
            pl.BlockSpec((1, D_HEADS), const),
            pl.BlockSpec((1, D_HEADS), const),
            pl.BlockSpec((T_SSD, 3 * T_SSD), const),
        ],
        out_specs=pl.BlockSpec((T_SSD, D_HEADS), row),
        out_shape=jax.ShapeDtypeStruct((batch * seq, D_HEADS), BF16),
        scratch_shapes=[
            pltpu.VMEM((T_SSD + CONV_TAIL, D_CONV), F32),
            pltpu.VMEM((N_PAIRS, SSD_STATE, LANES), F32),
        ],
        compiler_params=pltpu.CompilerParams(
            dimension_semantics=("arbitrary", "arbitrary"), vmem_limit_bytes=VMEM_LIMIT),
        name="ssd",
    )(z, xbc, dt, conv_w, conv_b, dtb, a_row, dskip, gain, ltri3)


LOG2E = 1.4426950408889634
ATT_DEPTH = 4


def _attn_kernel(ti_ref, tj_ref, q_ref, k_ref, v_ref, w2_ref, gain_ref, o_ref,
                 q2_ref, vst_ref, r_ref, acc_ref, z_ref, arg_ref):
    t = T_ATT
    n_blocks = q2_ref.shape[0]
    depth = ATT_DEPTH

    lane_v = lax.broadcasted_iota(jnp.int32, (t, LANES), 1)
    first_v = lane_v < HEAD_DIM

    for j in range(n_blocks):
        qb = q_ref[j * t:(j + 1) * t, :]
        vb = v_ref[j * t:(j + 1) * t, :]
        zero = jnp.zeros_like(qb)
        q2_ref[j, 0:t, :] = jnp.where(first_v, qb, zero)
        q2_ref[j, t:2 * t, :] = jnp.where(first_v, zero, qb)
        vst_ref[j, 0:t, :] = jnp.where(first_v, vb, zero)
        vst_ref[j, t:2 * t, :] = jnp.where(first_v, zero, vb)

    ti = lax.broadcasted_iota(jnp.int32, (2 * t, t), 0)
    si = lax.broadcasted_iota(jnp.int32, (2 * t, t), 1)
    strict = si < jnp.where(ti >= t, ti - t, ti)

    def key_rows(j):
        off = j * t
        return pl.ds(off if isinstance(off, int) else pl.multiple_of(off, t), t)

    def stage_scores(tiles, z_buf):
        for u, (i, j) in enumerate(tiles):
            z_buf[u] = _dot_nt(q2_ref[i], k_ref[key_rows(j), :])

    def stage_suffix_sums(tiles, z_buf, diag):
        for u, (i, _) in enumerate(tiles):
            zn = z_buf[u] * (-LOG2E)
            l1 = jnp.minimum(zn, 0.0) - jnp.log2(1.0 + jnp.exp2(-jnp.abs(zn)))
            zl = l1 - zn
            if diag:
                l1 = jnp.where(strict, l1, 0.0)
                zl = jnp.where(strict, zl, -jnp.inf)
            hi = l1.astype(BF16)
            lo = (l1 - hi.astype(F32)).astype(BF16)
            res = _dot(jnp.concatenate([hi, lo], axis=1), w2_ref[...])
            arg = zl + res[:, 0:t]
            tot = res[:, t:2 * t]
            if not diag:
                r_old = r_ref[i]
                arg = arg + r_old
                tot = tot + r_old
            arg_ref[u] = arg
            r_ref[i] = tot

    def stage_values(tiles, diag):
        for u, (i, j) in enumerate(tiles):
            w = jnp.exp2(arg_ref[u]).astype(BF16)
            contrib = _dot(jnp.concatenate([w[0:t, :], w[t:2 * t, :]], axis=1), vst_ref[j])
            if diag:
                acc_ref[i] = contrib
            else:
                acc_ref[i] += contrib

    def sweep(n_tiles, tile_of, diag):
        n_iter = n_tiles // depth
        group = lambda g: [tile_of(g * depth + u) for u in range(depth)]

        def iteration(m, parity, scores=True, suffix=True, values=True):
            if scores:
                stage_scores(group(m), z_ref.at[parity])
            if values:
                stage_values(group(m - 2), diag)
            if suffix:
                stage_suffix_sums(group(m - 1), z_ref.at[1 - parity], diag)

        iteration(0, 0, suffix=False, values=False)
        iteration(1, 1, values=False)

        def body(mm, carry):
            iteration(2 * mm, 0)
            iteration(2 * mm + 1, 1)
            return carry

        lax.fori_loop(1, n_iter // 2, body, 0)
        iteration(n_iter, 0, scores=False)
        iteration(n_iter + 1, 1, scores=False, suffix=False)

    sweep(n_blocks, lambda n: (n, n), True)
    sweep(ti_ref.shape[0], lambda n: (ti_ref[n], tj_ref[n]), False)

    gain = gain_ref[...]
    for i in range(n_blocks):
        o = acc_ref[i]
        o2 = o * o
        ss_a = jnp.sum(jnp.where(first_v, o2, 0.0), axis=-1, keepdims=True)
        ss_b = jnp.sum(jnp.where(first_v, 0.0, o2), axis=-1, keepdims=True)
        ms = jnp.where(first_v, ss_a, ss_b) * (1.0 / HEAD_DIM)
        o_ref[i * t:(i + 1) * t, :] = (o * lax.rsqrt(ms + EPS) * gain).astype(BF16)


def _attention(q, k, v, w2, gain, batch, seq):
    n_q = seq // T_ATT
    tiles = [(i, j) for j in range(n_q - 2, -1, -1) for i in range(j + 1, n_q)]
    for n_tiles in (n_q, len(tiles)):
        assert n_tiles % (2 * ATT_DEPTH) == 0
    ti = jnp.asarray([i for i, _ in tiles], jnp.int32)
    tj = jnp.asarray([j for _, j in tiles], jnp.int32)
    seq_blk = lambda b, p, *_: (b, p)
    return pl.pallas_call(
        _attn_kernel,
        grid_spec=pltpu.PrefetchScalarGridSpec(
            num_scalar_prefetch=2,
            grid=(batch, N_PAIRS),
            in_specs=[
                pl.BlockSpec((seq, LANES), seq_blk),
                pl.BlockSpec((seq, LANES), seq_blk),
                pl.BlockSpec((seq, LANES), seq_blk),
                pl.BlockSpec((2 * T_ATT, 2 * T_ATT), lambda b, p, *_: (0, 0)),
                pl.BlockSpec((1, LANES), lambda b, p, *_: (0, p)),
            ],
            out_specs=pl.BlockSpec((seq, LANES), seq_blk),
            scratch_shapes=[
                pltpu.VMEM((n_q, 2 * T_ATT, LANES), BF16),
                pltpu.VMEM((n_q, 2 * T_ATT, LANES), BF16),
                pltpu.VMEM((n_q, 2 * T_ATT, T_ATT), F32),
                pltpu.VMEM((n_q, T_ATT, LANES), F32),
                pltpu.VMEM((2, ATT_DEPTH, 2 * T_ATT, T_ATT), F32),
                pltpu.VMEM((ATT_DEPTH, 2 * T_ATT, T_ATT), F32),
            ],
        ),
        out_shape=jax.ShapeDtypeStruct((batch * seq, D_HEADS), BF16),
        compiler_params=pltpu.CompilerParams(
            dimension_semantics=("arbitrary", "arbitrary"),
            vmem_limit_bytes=VMEM_LIMIT),
        name="sb_attention",
    )(ti, tj, q, k, v, w2, gain)


def _rms(x, g):
    ms = jnp.mean(x * x, axis=-1, keepdims=True)
    return x * lax.rsqrt(ms + EPS) * g


def _ffn_kernel(x_ref, yssd_ref, ysb_ref, wo_ref, g_post_ref, g_pre_ref, wg_ref, wu_ref,
                wd_ref, g_out_ref, o_ref, x1_ref, h_ref, act_ref):
    n_chunks = wg_ref.shape[0]
    subs = [pl.ds(s * SUB_FFN, SUB_FFN) for s in range(TM_FFN // SUB_FFN)]
    for rows in subs:
        mix = (_dot(yssd_ref[rows, :], wo_ref[0:D_HEADS, :])
               + _dot(ysb_ref[rows, :], wo_ref[D_HEADS:2 * D_HEADS, :]))
        x1 = x_ref[rows, :] + _rms(mix, g_post_ref[...])
        x1_ref[rows, :] = x1
        h_ref[rows, :] = _rms(x1, g_pre_ref[...]).astype(BF16)
    for rows in subs:
        for c in range(n_chunks):
            h = h_ref[rows, :]
            gate = _dot(h, wg_ref[c])
            up = _dot(h, wu_ref[c])
            act_ref[rows, c * FF_CHUNK:(c + 1) * FF_CHUNK] = (_silu(gate) * up).astype(BF16)
    for rows in subs:
        f = _dot(act_ref[rows, :], wd_ref[...])
        o_ref[rows, :] = x1_ref[rows, :] + _rms(f, g_out_ref[...])


def _out_ffn(x2, y_ssd, y_sb, w_out, g_post, g_pre, wg, wu, wd, g_out):
    m = x2.shape[0]
    n_chunks = wg.shape[0]
    row = lambda i: (i, 0)
    const2 = lambda i: (0, 0)
    const3 = lambda i: (0, 0, 0)
    single = pl.Buffered(1)
    return pl.pallas_call(
        _ffn_kernel,
        grid=(m // TM_FFN,),
        in_specs=[
            pl.BlockSpec((TM_FFN, D_MODEL), row),
            pl.BlockSpec((TM_FFN, D_HEADS), row),
            pl.BlockSpec((TM_FFN, D_HEADS), row),
            pl.BlockSpec((2 * D_HEADS, D_MODEL), const2, pipeline_mode=single),
            pl.BlockSpec((1, D_MODEL), const2),
            pl.BlockSpec((1, D_MODEL), const2),
            pl.BlockSpec((n_chunks, D_MODEL, FF_CHUNK), const3, pipeline_mode=single),
            pl.BlockSpec((n_chunks, D_MODEL, FF_CHUNK), const3, pipeline_mode=single),
            pl.BlockSpec((D_FF, D_MODEL), const2, pipeline_mode=single),
            pl.BlockSpec((1, D_MODEL), const2),
        ],
        out_specs=pl.BlockSpec((TM_FFN, D_MODEL), row),
        out_shape=jax.ShapeDtypeStruct((m, D_MODEL), F32),
        scratch_shapes=[
            pltpu.VMEM((TM_FFN, D_MODEL), F32),
            pltpu.VMEM((TM_FFN, D_MODEL), BF16),
            pltpu.VMEM((TM_FFN, D_FF), BF16),
        ],
        compiler_params=pltpu.CompilerParams(
            dimension_semantics=("arbitrary",), vmem_limit_bytes=VMEM_LIMIT),
        name="out_ffn",
    )(x2, y_ssd, y_sb, w_out, g_post, g_pre, wg, wu, wd, g_out)


def _expand_heads(v):
    return jnp.repeat(v.astype(F32), HEAD_DIM)[None, :]


def _pad_lanes(v):
    return jnp.pad(v.astype(F32), (0, LANES - v.shape[0]))[None, :]


def _layer(x2, batch, seq, pre_mix_gain, w_in, conv_w, conv_b, dt_bias, a_log, d_skip,
           ssd_norm_gain, sb_norm_gain, w_out, post_mix_gain, pre_ffn_gain, w_gate, w_up,
           w_down, post_ffn_gain):
    o_xbc, o_dt = D_HEADS, D_HEADS + D_CONV
    o_q = o_dt + N_HEADS
    scale = 1.0 / math.sqrt(HEAD_DIM)
    w_all = jnp.concatenate([
        w_in[:, 0:o_dt],
        jnp.pad(w_in[:, o_dt:o_q], ((0, 0), (0, LANES - N_HEADS))),
        w_in[:, o_q:o_q + D_HEADS] * scale,
        w_in[:, o_q + D_HEADS:],
    ], axis=1).astype(BF16)

    z, xbc, dt, q, k, v = _inproj(x2, pre_mix_gain[None, :], w_all)

    tri = jnp.tril(jnp.ones((T_SSD, T_SSD), BF16))
    ltri3 = jnp.concatenate([tri, tri, tri], axis=1)
    a_row = _pad_lanes(-jnp.exp(a_log.astype(F32)))
    y_ssd = _ssd(z, xbc, dt, conv_w, conv_b[None, :], _pad_lanes(dt_bias), a_row,
                 _expand_heads(d_skip), ssd_norm_gain[None, :], ltri3, batch, seq)

    jj = jnp.arange(T_ATT)
    later = (jj[:, None] > jj[None, :]).astype(BF16)
    half = jnp.concatenate([later, jnp.ones((T_ATT, T_ATT), BF16)], axis=1)
    w2 = jnp.concatenate([half, half], axis=0)
    y_sb = _attention(q, k, v, w2, sb_norm_gain[None, :], batch, seq)

    n_chunks = D_FF // FF_CHUNK
    wg = w_gate.astype(BF16).reshape(D_MODEL, n_chunks, FF_CHUNK).transpose(1, 0, 2)
    wu = w_up.astype(BF16).reshape(D_MODEL, n_chunks, FF_CHUNK).transpose(1, 0, 2)
    wd = w_down.astype(BF16)
    return _out_ffn(x2, y_ssd, y_sb, w_out.astype(BF16), post_mix_gain[None, :],
                    pre_ffn_gain[None, :], wg, wu, wd, post_ffn_gain[None, :])


def kernel(x, pre_mix_gain, w_in, conv_w, conv_b, dt_bias, a_log, d_skip, ssd_norm_gain,
           sb_norm_gain, w_out, post_mix_gain, pre_ffn_gain, w_gate, w_up, w_down,
           post_ffn_gain):
    batch, seq, d = x.shape
    x2 = x.reshape(batch * seq, d)
    params = (pre_mix_gain, w_in, conv_w, conv_b, dt_bias, a_log, d_skip, ssd_norm_gain,
              sb_norm_gain, w_out, post_mix_gain, pre_ffn_gain, w_gate, w_up, w_down,
              post_ffn_gain)
    for layer in range(pre_mix_gain.shape[0]):
        x2 = _layer(x2, batch, seq, *(p[layer] for p in params))
    return x2.reshape(batch, seq, d)
```

```python
import functools
import math

import jax
import jax.numpy as jnp
from jax import lax
from jax.experimental import pallas as pl
from jax.experimental.pallas import tpu as pltpu

F32 = jnp.float32
BF16 = jnp.bfloat16

EPS = 1e-6
LANES = 128

D_MODEL = 1024
N_HEADS = 8
HEAD_DIM = 64
D_HEADS = N_HEADS * HEAD_DIM
N_PAIRS = N_HEADS // 2
SSD_GROUPS = 2
SSD_STATE = 128
CONV_WIDTH = 4
D_CONV = D_HEADS + 2 * SSD_GROUPS * SSD_STATE
D_FF = 2816

C_Z = 0
C_XBC = C_Z + D_HEADS
C_DT = C_XBC + D_CONV
C_Q = C_DT + LANES
C_K = C_Q + D_HEADS
C_V = C_K + D_HEADS
C_END = C_V + D_HEADS

TM_PROJ = 512
T_SSD = 128
T_ATT = 128
TM_FFN = 512
SUB_FFN = 256
FF_CHUNK = 256
CONV_TAIL = 8

VMEM_LIMIT = 56 * 1024 * 1024


def _dot(a, b):
    return jnp.dot(a, b, preferred_element_type=F32)


def _dot_nt(a, b):
    return lax.dot_general(a, b, (((1,), (1,)), ((), ())), preferred_element_type=F32)


def _dot_tn(a, b):
    return lax.dot_general(a, b, (((0,), (0,)), ((), ())), preferred_element_type=F32)


def _split3(x):
    hi = x.astype(BF16)
    r = x - hi.astype(F32)
    mid = r.astype(BF16)
    lo = (r - mid.astype(F32)).astype(BF16)
    return hi, mid, lo


def _silu(x):
    return x / (1.0 + jnp.exp(-x))


def _softplus(x):
    return jnp.maximum(x, 0.0) + jnp.log1p(jnp.exp(-jnp.abs(x)))


def _inproj_kernel(x_ref, g_ref, w_ref, z_ref, xbc_ref, dt_ref, q_ref, k_ref, v_ref):
    x = x_ref[...]
    ms = jnp.mean(x * x, axis=-1, keepdims=True)
    h = (x * lax.rsqrt(ms + EPS) * g_ref[...]).astype(BF16)
    z_ref[...] = _dot(h, w_ref[:, C_Z:C_XBC])
    xbc_ref[...] = _dot(h, w_ref[:, C_XBC:C_DT])
    dt_ref[...] = _dot(h, w_ref[:, C_DT:C_Q])
    q_ref[...] = _dot(h, w_ref[:, C_Q:C_K]).astype(BF16)
    k_ref[...] = _dot(h, w_ref[:, C_K:C_V]).astype(BF16)
    v_ref[...] = _dot(h, w_ref[:, C_V:C_END]).astype(BF16)


def _inproj(x2, gain, w_all):
    m = x2.shape[0]
    row = lambda i: (i, 0)
    const = lambda i: (0, 0)
    return pl.pallas_call(
        _inproj_kernel,
        grid=(m // TM_PROJ,),
        in_specs=[
            pl.BlockSpec((TM_PROJ, D_MODEL), row),
            pl.BlockSpec((1, D_MODEL), const),
            pl.BlockSpec((D_MODEL, C_END), const, pipeline_mode=pl.Buffered(1)),
        ],
        out_specs=[
            pl.BlockSpec((TM_PROJ, D_HEADS), row),
            pl.BlockSpec((TM_PROJ, D_CONV), row),
            pl.BlockSpec((TM_PROJ, LANES), row),
            pl.BlockSpec((TM_PROJ, D_HEADS), row),
            pl.BlockSpec((TM_PROJ, D_HEADS), row),
            pl.BlockSpec((TM_PROJ, D_HEADS), row),
        ],
        out_shape=[
            jax.ShapeDtypeStruct((m, D_HEADS), F32),
            jax.ShapeDtypeStruct((m, D_CONV), F32),
            jax.ShapeDtypeStruct((m, LANES), F32),
            jax.ShapeDtypeStruct((m, D_HEADS), BF16),
            jax.ShapeDtypeStruct((m, D_HEADS), BF16),
            jax.ShapeDtypeStruct((m, D_HEADS), BF16),
        ],
        compiler_params=pltpu.CompilerParams(
            dimension_semantics=("arbitrary",), vmem_limit_bytes=VMEM_LIMIT),
        name="inproj",
    )(x2, gain, w_all)


def _ssd_kernel(z_ref, xbc_ref, dt_ref, cw_ref, cb_ref, dtb_ref, a_ref, dskip_ref,
                gain_ref, ltri_ref, y_ref, ext_ref, state_ref):
    t = T_SSD
    c = pl.program_id(1)

    @pl.when(c == 0)
    def _():
        ext_ref[0:CONV_TAIL, :] = jnp.zeros((CONV_TAIL, D_CONV), F32)
        state_ref[...] = jnp.zeros(state_ref.shape, F32)

    ext_ref[CONV_TAIL:CONV_TAIL + t, :] = xbc_ref[...]
    conv = cb_ref[...]
    for k in range(CONV_WIDTH):
        off = CONV_TAIL - (CONV_WIDTH - 1) + k
        conv = conv + ext_ref[off:off + t, :] * cw_ref[k:k + 1, :]
    ext_ref[0:CONV_TAIL, :] = ext_ref[t:t + CONV_TAIL, :]
    xa = _silu(conv)

    dtv = _softplus(dt_ref[...] + dtb_ref[...])
    adt = dtv * a_ref[...]
    hi, mid, lo = _split3(adt)
    acs = _dot(ltri_ref[...], jnp.concatenate([hi, mid, lo], axis=0))
    acs_t = acs.T

    lane = lax.broadcasted_iota(jnp.int32, (t, LANES), 1)
    first_head = lane < HEAD_DIM
    li = lax.broadcasted_iota(jnp.int32, (t, t), 0)
    si = lax.broadcasted_iota(jnp.int32, (t, t), 1)
    causal = li >= si

    def col(v, h):
        return jnp.broadcast_to(v[:, h:h + 1], (t, LANES))

    y_blocks = []
    cb_mats = []
    for g in range(SSD_GROUPS):
        bm = xa[:, D_HEADS + g * SSD_STATE:D_HEADS + (g + 1) * SSD_STATE].astype(BF16)
        cm = xa[:, D_HEADS + (SSD_GROUPS + g) * SSD_STATE:
                D_HEADS + (SSD_GROUPS + g + 1) * SSD_STATE].astype(BF16)
        cb_mats.append((bm, cm, _dot_nt(cm, bm)))

    for p in range(N_PAIRS):
        ha, hb = 2 * p, 2 * p + 1
        bm, cm, cbm = cb_mats[p // (N_PAIRS // SSD_GROUPS)]
        x2 = xa[:, p * LANES:(p + 1) * LANES]
        dt2 = jnp.where(first_head, col(dtv, ha), col(dtv, hb))
        acs2 = jnp.where(first_head, col(acs, ha), col(acs, hb))
        xdt2 = x2 * dt2

        def decay(h):
            seg = col(acs, h) - jnp.broadcast_to(acs_t[h:h + 1, :], (t, t))
            return (cbm * jnp.exp(jnp.where(causal, seg, -jnp.inf))).astype(BF16)

        m2 = jnp.concatenate([decay(ha), decay(hb)], axis=1)
        xdt_a = jnp.where(first_head, xdt2, 0.0).astype(BF16)
        xdt_b = jnp.where(first_head, 0.0, xdt2).astype(BF16)
        y_diag = _dot(m2, jnp.concatenate([xdt_a, xdt_b], axis=0))

        prev = state_ref[p]
        y_off = _dot(cm, prev.astype(BF16)) * jnp.exp(acs2)
        last = acs2[t - 1:t, :]
        xs = (xdt2 * jnp.exp(last - acs2)).astype(BF16)
        state_ref[p] = prev * jnp.exp(last) + _dot_tn(bm, xs)

        y_blocks.append(y_diag + y_off + dskip_ref[:, p * LANES:(p + 1) * LANES] * x2)

    per_group = N_PAIRS // SSD_GROUPS
    for g in range(SSD_GROUPS):
        ys = []
        for p in range(g * per_group, (g + 1) * per_group):
            ys.append(y_blocks[p] * _silu(z_ref[:, p * LANES:(p + 1) * LANES]))
        ss = sum(jnp.sum(y * y, axis=-1, keepdims=True) for y in ys)
        inv = lax.rsqrt(ss * (1.0 / (per_group * LANES)) + EPS)
        for j, y in enumerate(ys):
            p = g * per_group + j
            y_ref[:, p * LANES:(p + 1) * LANES] = (
                y * inv * gain_ref[:, p * LANES:(p + 1) * LANES]).astype(BF16)


def _ssd(z, xbc, dt, conv_w, conv_b, dtb, a_row, dskip, gain, ltri3, batch, seq):
    n_chunks = seq // T_SSD
    row = lambda b, c: (b * n_chunks + c, 0)
    const = lambda b, c: (0, 0)
    return pl.pallas_call(
        _ssd_kernel,
        grid=(batch, n_chunks),
        in_specs=[
            pl.BlockSpec((T_SSD, D_HEADS), row),
            pl.BlockSpec((T_SSD, D_CONV), row),
            pl.BlockSpec((T_SSD, LANES), row),
            pl.BlockSpec((CONV_WIDTH, D_CONV), const),
            pl.BlockSpec((1, D_CONV), const),
            pl.BlockSpec((1, LANES), const),
            pl.BlockSpec((1, LANES), const),
            pl.BlockSpec((1, D_HEADS), const),
            pl.BlockSpec((1, D_HEADS), const),
            pl.BlockSpec((T_SSD, 3 * T_SSD), const),
        ],
        out_specs=pl.BlockSpec((T_SSD, D_HEADS), row),
        out_shape=jax.ShapeDtypeStruct((batch * seq, D_HEADS), BF16),
        scratch_shapes=[
            pltpu.VMEM((T_SSD + CONV_TAIL, D_CONV), F32),
            pltpu.VMEM((N_PAIRS, SSD_STATE, LANES), F32),
        ],
        compiler_params=pltpu.CompilerParams(
            dimension_semantics=("arbitrary", "arbitrary"), vmem_limit_bytes=VMEM_LIMIT),
        name="ssd",
    )(z, xbc, dt, conv_w, conv_b, dtb, a_row, dskip, gain, ltri3)


LOG2E = 1.4426950408889634
ATT_DEPTH = 4
ATT_NEAR_DIAGONALS = 3
ATT_UNDERFLOW_LOG2 = -160.0


def _attn_kernel(ni_ref, nj_ref, fi_ref, fj_ref, q_ref, k_ref, v_ref, w2_ref, gain_ref, o_ref,
                 q2_ref, vst_ref, r_ref, acc_ref, z_ref, arg_ref):
    t = T_ATT
    n_blocks = q2_ref.shape[0] - 1
    depth = ATT_DEPTH

    lane_v = lax.broadcasted_iota(jnp.int32, (t, LANES), 1)
    first_v = lane_v < HEAD_DIM

    for j in range(n_blocks):
        qb = q_ref[j * t:(j + 1) * t, :]
        vb = v_ref[j * t:(j + 1) * t, :]
        zero = jnp.zeros_like(qb)
        q2_ref[j, 0:t, :] = jnp.where(first_v, qb, zero)
        q2_ref[j, t:2 * t, :] = jnp.where(first_v, zero, qb)
        vst_ref[j, 0:t, :] = jnp.where(first_v, vb, zero)
        vst_ref[j, t:2 * t, :] = jnp.where(first_v, zero, vb)
    q2_ref[n_blocks] = jnp.zeros(q2_ref.shape[1:], BF16)
    r_ref[n_blocks] = jnp.zeros(r_ref.shape[1:], F32)
    acc_ref[n_blocks] = jnp.zeros(acc_ref.shape[1:], F32)

    ti = lax.broadcasted_iota(jnp.int32, (2 * t, t), 0)
    si = lax.broadcasted_iota(jnp.int32, (2 * t, t), 1)
    strict = si < jnp.where(ti >= t, ti - t, ti)

    def key_rows(j):
        off = j * t
        return pl.ds(off if isinstance(off, int) else pl.multiple_of(off, t), t)

    def stage_scores(tiles, z_buf):
        for u, (i, j) in enumerate(tiles):
            z_buf[u] = _dot_nt(q2_ref[i], k_ref[key_rows(j), :])

    def stage_suffix_sums(tiles, z_buf, diag):
        for u, (i, _) in enumerate(tiles):
            zn = z_buf[u] * (-LOG2E)
            l1 = jnp.minimum(zn, 0.0) - jnp.log2(1.0 + jnp.exp2(-jnp.abs(zn)))
            zl = l1 - zn
            if diag:
                l1 = jnp.where(strict, l1, 0.0)
                zl = jnp.where(strict, zl, -jnp.inf)
            hi = l1.astype(BF16)
            lo = (l1 - hi.astype(F32)).astype(BF16)
            res = _dot(jnp.concatenate([hi, lo], axis=1), w2_ref[...])
            arg = zl + res[:, 0:t]
            tot = res[:, t:2 * t]
            if not diag:
                r_old = r_ref[i]
                arg = arg + r_old
                tot = tot + r_old
            arg_ref[u] = arg
            r_ref[i] = tot

    def stage_values(tiles, diag):
        for u, (i, j) in enumerate(tiles):
            w = jnp.exp2(arg_ref[u]).astype(BF16)
            contrib = _dot(jnp.concatenate([w[0:t, :], w[t:2 * t, :]], axis=1), vst_ref[j])
            if diag:
                acc_ref[i] = contrib
            else:
                acc_ref[i] += contrib

    def sweep(i_ref, j_ref, n_diag_groups):
        n_groups = i_ref.shape[0] // depth
        group = lambda g: [(i_ref[g * depth + u], j_ref[g * depth + u]) for u in range(depth)]
        n_static = n_diag_groups + 2
        assert n_static % 2 == 0 and n_groups % 2 == 0 and n_groups >= n_static

        def iteration(m, parity, static):
            if not static or m < n_groups:
                stage_scores(group(m), z_ref.at[parity])
            if not static or 0 <= m - 2 < n_groups:
                stage_values(group(m - 2), static and m - 2 < n_diag_groups)
            if not static or 0 <= m - 1 < n_groups:
                stage_suffix_sums(group(m - 1), z_ref.at[1 - parity],
                                  static and m - 1 < n_diag_groups)

        for m in range(n_static):
            iteration(m, m % 2, True)

        def body(mm, carry):
            iteration(2 * mm, 0, False)
            iteration(2 * mm + 1, 1, False)
            return carry

        lax.fori_loop(n_static // 2, n_groups // 2, body, 0)
        for m in (n_groups, n_groups + 1):
            iteration(m, m % 2, True)

    sweep(ni_ref, nj_ref, pl.cdiv(n_blocks, depth))

    far_blocks = range(ATT_NEAR_DIAGONALS, n_blocks)
    r_max = functools.reduce(jnp.maximum, [r_ref[i] for i in far_blocks])

    @pl.when(jnp.max(r_max) >= ATT_UNDERFLOW_LOG2)
    def _():
        sweep(fi_ref, fj_ref, 0)

    gain = gain_ref[...]
    for i in range(n_blocks):
        o = acc_ref[i]
        o2 = o * o
        ss_a = jnp.sum(jnp.where(first_v, o2, 0.0), axis=-1, keepdims=True)
        ss_b = jnp.sum(jnp.where(first_v, 0.0, o2), axis=-1, keepdims=True)
        ms = jnp.where(first_v, ss_a, ss_b) * (1.0 / HEAD_DIM)
        o_ref[i * t:(i + 1) * t, :] = (o * lax.rsqrt(ms + EPS) * gain).astype(BF16)


def _attention(q, k, v, w2, gain, batch, seq):
    n_q = seq // T_ATT
    assert n_q % ATT_DEPTH == 0

    def tile_list(diagonals):
        tiles = [(i, i - d) for d in diagonals for i in range(d, n_q)]
        tiles += [(n_q, 0)] * (-len(tiles) % (2 * ATT_DEPTH))
        return (jnp.asarray([i for i, _ in tiles], jnp.int32),
                jnp.asarray([j for _, j in tiles], jnp.int32))

    near = tile_list(range(ATT_NEAR_DIAGONALS))
    far = tile_list(range(ATT_NEAR_DIAGONALS, n_q))
    seq_blk = lambda b, p, *_: (b, p)
    return pl.pallas_call(
        _attn_kernel,
        grid_spec=pltpu.PrefetchScalarGridSpec(
            num_scalar_prefetch=4,
            grid=(batch, N_PAIRS),
            in_specs=[
                pl.BlockSpec((seq, LANES), seq_blk),
                pl.BlockSpec((seq, LANES), seq_blk),
                pl.BlockSpec((seq, LANES), seq_blk),
                pl.BlockSpec((2 * T_ATT, 2 * T_ATT), lambda b, p, *_: (0, 0)),
                pl.BlockSpec((1, LANES), lambda b, p, *_: (0, p)),
            ],
            out_specs=pl.BlockSpec((seq, LANES), seq_blk),
            scratch_shapes=[
                pltpu.VMEM((n_q + 1, 2 * T_ATT, LANES), BF16),
                pltpu.VMEM((n_q, 2 * T_ATT, LANES), BF16),
                pltpu.VMEM((n_q + 1, 2 * T_ATT, T_ATT), F32),
                pltpu.VMEM((n_q + 1, T_ATT, LANES), F32),
                pltpu.VMEM((2, ATT_DEPTH, 2 * T_ATT, T_ATT), F32),
                pltpu.VMEM((ATT_DEPTH, 2 * T_ATT, T_ATT), F32),
            ],
        ),
        out_shape=jax.ShapeDtypeStruct((batch * seq, D_HEADS), BF16),
        compiler_params=pltpu.CompilerParams(
            dimension_semantics=("arbitrary", "arbitrary"),
            vmem_limit_bytes=VMEM_LIMIT),
        name="sb_attention",
    )(*near, *far, q, k, v, w2, gain)


def _rms(x, g):
    ms = jnp.mean(x * x, axis=-1, keepdims=True)
    return x * lax.rsqrt(ms + EPS) * g


def _ffn_kernel(x_ref, yssd_ref, ysb_ref, wo_ref, g_post_ref, g_pre_ref, wg_ref, wu_ref,
                wd_ref, g_out_ref, o_ref, x1_ref, h_ref, act_ref):
    n_chunks = wg_ref.shape[0]
    subs = [pl.ds(s * SUB_FFN, SUB_FFN) for s in range(TM_FFN // SUB_FFN)]
    for rows in subs:
        mix = (_dot(yssd_ref[rows, :], wo_ref[0:D_HEADS, :])
               + _dot(ysb_ref[rows, :], wo_ref[D_HEADS:2 * D_HEADS, :]))
        x1 = x_ref[rows, :] + _rms(mix, g_post_ref[...])
        x1_ref[rows, :] = x1
        h_ref[rows, :] = _rms(x1, g_pre_ref[...]).astype(BF16)
    for rows in subs:
        for c in range(n_chunks):
            h = h_ref[rows, :]
            gate = _dot(h, wg_ref[c])
            up = _dot(h, wu_ref[c])
            act_ref[rows, c * FF_CHUNK:(c + 1) * FF_CHUNK] = (_silu(gate) * up).astype(BF16)
    for rows in subs:
        f = _dot(act_ref[rows, :], wd_ref[...])
        o_ref[rows, :] = x1_ref[rows, :] + _rms(f, g_out_ref[...])


def _out_ffn(x2, y_ssd, y_sb, w_out, g_post, g_pre, wg, wu, wd, g_out):
    m = x2.shape[0]
    n_chunks = wg.shape[0]
    row = lambda i: (i, 0)
    const2 = lambda i: (0, 0)
    const3 = lambda i: (0, 0, 0)
    single = pl.Buffered(1)
    return pl.pallas_call(
        _ffn_kernel,
        grid=(m // TM_FFN,),
        in_specs=[
            pl.BlockSpec((TM_FFN, D_MODEL), row),
            pl.BlockSpec((TM_FFN, D_HEADS), row),
            pl.BlockSpec((TM_FFN, D_HEADS), row),
            pl.BlockSpec((2 * D_HEADS, D_MODEL), const2, pipeline_mode=single),
            pl.BlockSpec((1, D_MODEL), const2),
            pl.BlockSpec((1, D_MODEL), const2),
            pl.BlockSpec((n_chunks, D_MODEL, FF_CHUNK), const3, pipeline_mode=single),
            pl.BlockSpec((n_chunks, D_MODEL, FF_CHUNK), const3, pipeline_mode=single),
            pl.BlockSpec((D_FF, D_MODEL), const2, pipeline_mode=single),
            pl.BlockSpec((1, D_MODEL), const2),
        ],
        out_specs=pl.BlockSpec((TM_FFN, D_MODEL), row),
        out_shape=jax.ShapeDtypeStruct((m, D_MODEL), F32),
        scratch_shapes=[
            pltpu.VMEM((TM_FFN, D_MODEL), F32),
            pltpu.VMEM((TM_FFN, D_MODEL), BF16),
            pltpu.VMEM((TM_FFN, D_FF), BF16),
        ],
        compiler_params=pltpu.CompilerParams(
            dimension_semantics=("arbitrary",), vmem_limit_bytes=VMEM_LIMIT),
        name="out_ffn",
    )(x2, y_ssd, y_sb, w_out, g_post, g_pre, wg, wu, wd, g_out)


def _expand_heads(v):
    return jnp.repeat(v.astype(F32), HEAD_DIM)[None, :]


def _pad_lanes(v):
    return jnp.pad(v.astype(F32), (0, LANES - v.shape[0]))[None, :]


def _layer(x2, batch, seq, pre_mix_gain, w_in, conv_w, conv_b, dt_bias, a_log, d_skip,
           ssd_norm_gain, sb_norm_gain, w_out, post_mix_gain, pre_ffn_gain, w_gate, w_up,
           w_down, post_ffn_gain):
    o_xbc, o_dt = D_HEADS, D_HEADS + D_CONV
    o_q = o_dt + N_HEADS
    scale = 1.0 / math.sqrt(HEAD_DIM)
    w_all = jnp.concatenate([
        w_in[:, 0:o_dt],
        jnp.pad(w_in[:, o_dt:o_q], ((0, 0), (0, LANES - N_HEADS))),
        w_in[:, o_q:o_q + D_HEADS] * scale,
        w_in[:, o_q + D_HEADS:],
    ], axis=1).astype(BF16)

    z, xbc, dt, q, k, v = _inproj(x2, pre_mix_gain[None, :], w_all)

    tri = jnp.tril(jnp.ones((T_SSD, T_SSD), BF16))
    ltri3 = jnp.concatenate([tri, tri, tri], axis=1)
    a_row = _pad_lanes(-jnp.exp(a_log.astype(F32)))
    y_ssd = _ssd(z, xbc, dt, conv_w, conv_b[None, :], _pad_lanes(dt_bias), a_row,
                 _expand_heads(d_skip), ssd_norm_gain[None, :], ltri3, batch, seq)

    jj = jnp.arange(T_ATT)
    later = (jj[:, None] > jj[None, :]).astype(BF16)
    half = jnp.concatenate([later, jnp.ones((T_ATT, T_ATT), BF16)], axis=1)
    w2 = jnp.concatenate([half, half], axis=0)
    y_sb = _attention(q, k, v, w2, sb_norm_gain[None, :], batch, seq)

    n_chunks = D_FF // FF_CHUNK
    wg = w_gate.astype(BF16).reshape(D_MODEL, n_chunks, FF_CHUNK).transpose(1, 0, 2)
    wu = w_up.astype(BF16).reshape(D_MODEL, n_chunks, FF_CHUNK).transpose(1, 0, 2)
    wd = w_down.astype(BF16)
    return _out_ffn(x2, y_ssd, y_sb, w_out.astype(BF16), post_mix_gain[None, :],
                    pre_ffn_gain[None, :], wg, wu, wd, post_ffn_gain[None, :])


def kernel(x, pre_mix_gain, w_in, conv_w, conv_b, dt_bias, a_log, d_skip, ssd_norm_gain,
           sb_norm_gain, w_out, post_mix_gain, pre_ffn_gain, w_gate, w_up, w_down,
           post_ffn_gain):
    batch, seq, d = x.shape
    x2 = x.reshape(batch * seq, d)
    params = (pre_mix_gain, w_in, conv_w, conv_b, dt_bias, a_log, d_skip, ssd_norm_gain,
              sb_norm_gain, w_out, post_mix_gain, pre_ffn_gain, w_gate, w_up, w_down,
              post_ffn_gain)
    for layer in range(pre_mix_gain.shape[0]):
        x2 = _layer(x2, batch, seq, *(p[layer] for p in params))
    return x2.reshape(batch, seq, d)
```

```python
import functools
import math

import jax
import jax.numpy as jnp
from jax import lax
from jax.experimental import pallas as pl
from jax.experimental.pallas import tpu as pltpu

F32 = jnp.float32
BF16 = jnp.bfloat16

EPS = 1e-6
LANES = 128

D_MODEL = 1024
N_HEADS = 8
HEAD_DIM = 64
D_HEADS = N_HEADS * HEAD_DIM
N_PAIRS = N_HEADS // 2
SSD_GROUPS = 2
SSD_STATE = 128
CONV_WIDTH = 4
D_CONV = D_HEADS + 2 * SSD_GROUPS * SSD_STATE
D_FF = 2816

C_Z = 0
C_XBC = C_Z + D_HEADS
C_DT = C_XBC + D_CONV
C_Q = C_DT + LANES
C_K = C_Q + D_HEADS
C_V = C_K + D_HEADS
C_END = C_V + D_HEADS

TM_PROJ = 512
T_SSD = 128
T_ATT = 128
TM_FFN = 512
SUB_FFN = 256
FF_CHUNK = 256
CONV_TAIL = 8

VMEM_LIMIT = 56 * 1024 * 1024


def _dot(a, b):
    return jnp.dot(a, b, preferred_element_type=F32)


def _dot_nt(a, b):
    return lax.dot_general(a, b, (((1,), (1,)), ((), ())), preferred_element_type=F32)


def _dot_tn(a, b):
    return lax.dot_general(a, b, (((0,), (0,)), ((), ())), preferred_element_type=F32)


def _split3(x):
    hi = x.astype(BF16)
    r = x - hi.astype(F32)
    mid = r.astype(BF16)
    lo = (r - mid.astype(F32)).astype(BF16)
    return hi, mid, lo


def _silu(x):
    return x / (1.0 + jnp.exp(-x))


def _softplus(x):
    e = jnp.exp(-jnp.abs(x))
    u = 1.0 + e
    tiny = u == 1.0
    log1p_e = jnp.where(tiny, e, jnp.log(u) * (e / jnp.where(tiny, 1.0, u - 1.0)))
    return jnp.maximum(x, 0.0) + log1p_e


def _inproj_kernel(x_ref, g_ref, w_ref, dtb_ref, sz_ref, xbc_ref, dt_ref, q_ref, k_ref, v_ref,
                   h_ref):
    x = x_ref[...]
    ms = jnp.mean(x * x, axis=-1, keepdims=True)
    h_ref[...] = (x * lax.rsqrt(ms + EPS) * g_ref[...]).astype(BF16)

    def proj(c0, c1):
        return _dot(h_ref[...], w_ref[:, c0:c1])

    sz_ref[...] = _silu(proj(C_Z, C_XBC))
    xbc_ref[...] = proj(C_XBC, C_DT)
    dt_ref[...] = _softplus(proj(C_DT, C_Q) + dtb_ref[...])
    q_ref[...] = proj(C_Q, C_K).astype(BF16)
    k_ref[...] = proj(C_K, C_V).astype(BF16)
    v_ref[...] = proj(C_V, C_END).astype(BF16)


def _inproj(x2, gain, w_all, dtb):
    m = x2.shape[0]
    row = lambda i: (i, 0)
    const = lambda i: (0, 0)
    return pl.pallas_call(
        _inproj_kernel,
        grid=(m // TM_PROJ,),
        in_specs=[
            pl.BlockSpec((TM_PROJ, D_MODEL), row),
            pl.BlockSpec((1, D_MODEL), const),
            pl.BlockSpec((D_MODEL, C_END), const, pipeline_mode=pl.Buffered(1)),
            pl.BlockSpec((1, LANES), const),
        ],
        out_specs=[
            pl.BlockSpec((TM_PROJ, D_HEADS), row),
            pl.BlockSpec((TM_PROJ, D_CONV), row),
            pl.BlockSpec((TM_PROJ, LANES), row),
            pl.BlockSpec((TM_PROJ, D_HEADS), row),
            pl.BlockSpec((TM_PROJ, D_HEADS), row),
            pl.BlockSpec((TM_PROJ, D_HEADS), row),
        ],
        out_shape=[
            jax.ShapeDtypeStruct((m, D_HEADS), F32),
            jax.ShapeDtypeStruct((m, D_CONV), F32),
            jax.ShapeDtypeStruct((m, LANES), F32),
            jax.ShapeDtypeStruct((m, D_HEADS), BF16),
            jax.ShapeDtypeStruct((m, D_HEADS), BF16),
            jax.ShapeDtypeStruct((m, D_HEADS), BF16),
        ],
        scratch_shapes=[pltpu.VMEM((TM_PROJ, D_MODEL), BF16)],
        compiler_params=pltpu.CompilerParams(
            dimension_semantics=("arbitrary",), vmem_limit_bytes=VMEM_LIMIT),
        name="inproj",
    )(x2, gain, w_all, dtb)


def _ssd_kernel(sz_ref, xbc_ref, dt_ref, cw_ref, cb_ref, a_ref, dskip_ref, gain_ref, utri_ref,
                y_ref, ext_ref, state_ref):
    t = T_SSD

    @pl.when(pl.program_id(1) == 0)
    def _():
        ext_ref[0:CONV_TAIL, :] = jnp.zeros((CONV_TAIL, D_CONV), F32)
        state_ref[...] = jnp.zeros(state_ref.shape, F32)

    ext_ref[CONV_TAIL:CONV_TAIL + t, :] = xbc_ref[...]
    conv = cb_ref[...]
    for k in range(CONV_WIDTH):
        off = CONV_TAIL - (CONV_WIDTH - 1) + k
        conv = conv + ext_ref[off:off + t, :] * cw_ref[k:k + 1, :]
    ext_ref[0:CONV_TAIL, :] = ext_ref[t:t + CONV_TAIL, :]
    xa = _silu(conv)

    dtv = dt_ref[...]
    adt = dtv * a_ref[...]
    utri = utri_ref[...]
    acs_t = sum(_dot(part, utri) for part in _split3(adt.T))
    acs = acs_t.T

    lane = lax.broadcasted_iota(jnp.int32, (t, LANES), 1)
    first_head = lane < HEAD_DIM
    li = lax.broadcasted_iota(jnp.int32, (t, t), 0)
    si = lax.broadcasted_iota(jnp.int32, (t, t), 1)
    causal = li >= si

    def col(v, h):
        return jnp.broadcast_to(v[:, h:h + 1], (t, LANES))

    y_blocks = []
    cb_mats = []
    for g in range(SSD_GROUPS):
        bm = xa[:, D_HEADS + g * SSD_STATE:D_HEADS + (g + 1) * SSD_STATE].astype(BF16)
        cm = xa[:, D_HEADS + (SSD_GROUPS + g) * SSD_STATE:
                D_HEADS + (SSD_GROUPS + g + 1) * SSD_STATE].astype(BF16)
        cb_mats.append((bm, cm, _dot_nt(cm, bm)))

    for p in range(N_PAIRS):
        ha, hb = 2 * p, 2 * p + 1
        bm, cm, cbm = cb_mats[p // (N_PAIRS // SSD_GROUPS)]
        x2 = xa[:, p * LANES:(p + 1) * LANES]
        dt2 = jnp.where(first_head, col(dtv, ha), col(dtv, hb))
        acs2 = jnp.where(first_head, col(acs, ha), col(acs, hb))
        xdt2 = x2 * dt2

        def decay(h):
            seg = col(acs, h) - jnp.broadcast_to(acs_t[h:h + 1, :], (t, t))
            return (cbm * jnp.exp(jnp.where(causal, seg, -jnp.inf))).astype(BF16)

        m2 = jnp.concatenate([decay(ha), decay(hb)], axis=1)
        xdt_a = jnp.where(first_head, xdt2, 0.0).astype(BF16)
        xdt_b = jnp.where(first_head, 0.0, xdt2).astype(BF16)
        y_diag = _dot(m2, jnp.concatenate([xdt_a, xdt_b], axis=0))

        prev = state_ref[p]
        y_off = _dot(cm, prev.astype(BF16)) * jnp.exp(acs2)
        last = acs2[t - 1:t, :]
        xs = (xdt2 * jnp.exp(last - acs2)).astype(BF16)
        state_ref[p] = prev * jnp.exp(last) + _dot_tn(bm, xs)

        y_blocks.append(y_diag + y_off + dskip_ref[:, p * LANES:(p + 1) * LANES] * x2)

    per_group = N_PAIRS // SSD_GROUPS
    for g in range(SSD_GROUPS):
        ys = []
        for p in range(g * per_group, (g + 1) * per_group):
            ys.append(y_blocks[p] * sz_ref[:, p * LANES:(p + 1) * LANES])
        ss = sum(jnp.sum(y * y, axis=-1, keepdims=True) for y in ys)
        inv = lax.rsqrt(ss * (1.0 / (per_group * LANES)) + EPS)
        for j, y in enumerate(ys):
            p = g * per_group + j
            y_ref[:, p * LANES:(p + 1) * LANES] = (
                y * inv * gain_ref[:, p * LANES:(p + 1) * LANES]).astype(BF16)


def _ssd(sz, xbc, dt, conv_w, conv_b, a_row, dskip, gain, utri, batch, seq):
    n_chunks = seq // T_SSD
    row = lambda b, c: (b * n_chunks + c, 0)
    const = lambda b, c: (0, 0)
    return pl.pallas_call(
        _ssd_kernel,
        grid=(batch, n_chunks),
        in_specs=[
            pl.BlockSpec((T_SSD, D_HEADS), row),
            pl.BlockSpec((T_SSD, D_CONV), row),
            pl.BlockSpec((T_SSD, LANES), row),
            pl.BlockSpec((CONV_WIDTH, D_CONV), const),
            pl.BlockSpec((1, D_CONV), const),
            pl.BlockSpec((1, LANES), const),
            pl.BlockSpec((1, D_HEADS), const),
            pl.BlockSpec((1, D_HEADS), const),
            pl.BlockSpec((T_SSD, T_SSD), const),
        ],
        out_specs=pl.BlockSpec((T_SSD, D_HEADS), row),
        out_shape=jax.ShapeDtypeStruct((batch * seq, D_HEADS), BF16),
        scratch_shapes=[
            pltpu.VMEM((T_SSD + CONV_TAIL, D_CONV), F32),
            pltpu.VMEM((N_PAIRS, SSD_STATE, LANES), F32),
        ],
        compiler_params=pltpu.CompilerParams(
            dimension_semantics=("arbitrary", "arbitrary"), vmem_limit_bytes=VMEM_LIMIT),
        name="ssd",
    )(sz, xbc, dt, conv_w, conv_b, a_row, dskip, gain, utri)


LOG2E = 1.4426950408889634
ATT_DEPTH = 4
ATT_NEAR_DIAGONALS = 3
ATT_UNDERFLOW_LOG2 = -160.0


def _attn_kernel(ni_ref, nj_ref, fi_ref, fj_ref, q_ref, k_ref, v_ref, w2_ref, gain_ref, o_ref,
                 q2_ref, vst_ref, r_ref, acc_ref, z_ref, arg_ref):
    t = T_ATT
    n_blocks = q2_ref.shape[0] - 1
    depth = ATT_DEPTH

    lane_v = lax.broadcasted_iota(jnp.int32, (t, LANES), 1)
    first_v = lane_v < HEAD_DIM

    for j in range(n_blocks):
        qb = q_ref[j * t:(j + 1) * t, :]
        vb = v_ref[j * t:(j + 1) * t, :]
        zero = jnp.zeros_like(qb)
        q2_ref[j, 0:t, :] = jnp.where(first_v, qb, zero)
        q2_ref[j, t:2 * t, :] = jnp.where(first_v, zero, qb)
        vst_ref[j, 0:t, :] = jnp.where(first_v, vb, zero)
        vst_ref[j, t:2 * t, :] = jnp.where(first_v, zero, vb)
    q2_ref[n_blocks] = jnp.zeros(q2_ref.shape[1:], BF16)
    r_ref[n_blocks] = jnp.zeros(r_ref.shape[1:], F32)
    acc_ref[n_blocks] = jnp.zeros(acc_ref.shape[1:], F32)

    ti = lax.broadcasted_iota(jnp.int32, (2 * t, t), 0)
    si = lax.broadcasted_iota(jnp.int32, (2 * t, t), 1)
    strict = si < jnp.where(ti >= t, ti - t, ti)

    def key_rows(j):
        off = j * t
        return pl.ds(off if isinstance(off, int) else pl.multiple_of(off, t), t)

    def stage_scores(tiles, z_buf):
        for u, (i, j) in enumerate(tiles):
            z_buf[u] = _dot_nt(q2_ref[i], k_ref[key_rows(j), :])

    def stage_suffix_sums(tiles, z_buf, diag):
        for u, (i, _) in enumerate(tiles):
            zn = z_buf[u] * (-LOG2E)
            l1 = jnp.minimum(zn, 0.0) - jnp.log2(1.0 + jnp.exp2(-jnp.abs(zn)))
            zl = l1 - zn
            if diag:
                l1 = jnp.where(strict, l1, 0.0)
                zl = jnp.where(strict, zl, -jnp.inf)
            hi = l1.astype(BF16)
            lo = (l1 - hi.astype(F32)).astype(BF16)
            res = _dot(jnp.concatenate([hi, lo], axis=1), w2_ref[...])
            arg = zl + res[:, 0:t]
            tot = res[:, t:2 * t]
            if not diag:
                r_old = r_ref[i]
                arg = arg + r_old
                tot = tot + r_old
            arg_ref[u] = arg
            r_ref[i] = tot

    def stage_values(tiles, diag):
        for u, (i, j) in enumerate(tiles):
            w = jnp.exp2(arg_ref[u]).astype(BF16)
            contrib = _dot(jnp.concatenate([w[0:t, :], w[t:2 * t, :]], axis=1), vst_ref[j])
            if diag:
                acc_ref[i] = contrib
            else:
                acc_ref[i] += contrib

    def sweep(i_ref, j_ref, n_diag_groups):
        n_groups = i_ref.shape[0] // depth
        group = lambda g: [(i_ref[g * depth + u], j_ref[g * depth + u]) for u in range(depth)]
        n_static = n_diag_groups + 2
        assert n_static % 2 == 0 and n_groups % 2 == 0 and n_groups >= n_static

        def iteration(m, parity, static):
            if not static or m < n_groups:
                stage_scores(group(m), z_ref.at[parity])
            if not static or 0 <= m - 2 < n_groups:
                stage_values(group(m - 2), static and m - 2 < n_diag_groups)
            if not static or 0 <= m - 1 < n_groups:
                stage_suffix_sums(group(m - 1), z_ref.at[1 - parity],
                                  static and m - 1 < n_diag_groups)

        for m in range(n_static):
            iteration(m, m % 2, True)

        def body(mm, carry):
            iteration(2 * mm, 0, False)
            iteration(2 * mm + 1, 1, False)
            return carry

        lax.fori_loop(n_static // 2, n_groups // 2, body, 0)
        for m in (n_groups, n_groups + 1):
            iteration(m, m % 2, True)

    sweep(ni_ref, nj_ref, pl.cdiv(n_blocks, depth))

    far_blocks = range(ATT_NEAR_DIAGONALS, n_blocks)
    r_max = functools.reduce(jnp.maximum, [r_ref[i] for i in far_blocks])

    @pl.when(jnp.max(r_max) >= ATT_UNDERFLOW_LOG2)
    def _():
        sweep(fi_ref, fj_ref, 0)

    gain = gain_ref[...]
    for i in range(n_blocks):
        o = acc_ref[i]
        o2 = o * o
        ss_a = jnp.sum(jnp.where(first_v, o2, 0.0), axis=-1, keepdims=True)
        ss_b = jnp.sum(jnp.where(first_v, 0.0, o2), axis=-1, keepdims=True)
        ms = jnp.where(first_v, ss_a, ss_b) * (1.0 / HEAD_DIM)
        o_ref[i * t:(i + 1) * t, :] = (o * lax.rsqrt(ms + EPS) * gain).astype(BF16)


def _attention(q, k, v, w2, gain, batch, seq):
    n_q = seq // T_ATT
    assert n_q % ATT_DEPTH == 0

    def tile_list(diagonals):
        tiles = [(i, i - d) for d in diagonals for i in range(d, n_q)]
        tiles += [(n_q, 0)] * (-len(tiles) % (2 * ATT_DEPTH))
        return (jnp.asarray([i for i, _ in tiles], jnp.int32),
                jnp.asarray([j for _, j in tiles], jnp.int32))

    near = tile_list(range(ATT_NEAR_DIAGONALS))
    far = tile_list(range(ATT_NEAR_DIAGONALS, n_q))
    seq_blk = lambda b, p, *_: (b, p)
    return pl.pallas_call(
        _attn_kernel,
        grid_spec=pltpu.PrefetchScalarGridSpec(
            num_scalar_prefetch=4,
            grid=(batch, N_PAIRS),
            in_specs=[
                pl.BlockSpec((seq, LANES), seq_blk),
                pl.BlockSpec((seq, LANES), seq_blk),
                pl.BlockSpec((seq, LANES), seq_blk),
                pl.BlockSpec((2 * T_ATT, 2 * T_ATT), lambda b, p, *_: (0, 0)),
                pl.BlockSpec((1, LANES), lambda b, p, *_: (0, p)),
            ],
            out_specs=pl.BlockSpec((seq, LANES), seq_blk),
            scratch_shapes=[
                pltpu.VMEM((n_q + 1, 2 * T_ATT, LANES), BF16),
                pltpu.VMEM((n_q, 2 * T_ATT, LANES), BF16),
                pltpu.VMEM((n_q + 1, 2 * T_ATT, T_ATT), F32),
                pltpu.VMEM((n_q + 1, T_ATT, LANES), F32),
                pltpu.VMEM((2, ATT_DEPTH, 2 * T_ATT, T_ATT), F32),
                pltpu.VMEM((ATT_DEPTH, 2 * T_ATT, T_ATT), F32),
            ],
        ),
        out_shape=jax.ShapeDtypeStruct((batch * seq, D_HEADS), BF16),
        compiler_params=pltpu.CompilerParams(
            dimension_semantics=("arbitrary", "arbitrary"),
            vmem_limit_bytes=VMEM_LIMIT),
        name="sb_attention",
    )(*near, *far, q, k, v, w2, gain)


def _rms(x, g):
    ms = jnp.mean(x * x, axis=-1, keepdims=True)
    return x * lax.rsqrt(ms + EPS) * g


def _ffn_kernel(x_ref, yssd_ref, ysb_ref, wo_ref, g_post_ref, g_pre_ref, wg_ref, wu_ref,
                wd_ref, g_out_ref, o_ref, x1_ref, h_ref, act_ref):
    chunks = [slice(c * FF_CHUNK, (c + 1) * FF_CHUNK) for c in range(D_FF // FF_CHUNK)]
    subs = [pl.ds(s * SUB_FFN, SUB_FFN) for s in range(TM_FFN // SUB_FFN)]
    for rows in subs:
        mix = (_dot(yssd_ref[rows, :], wo_ref[0:D_HEADS, :])
               + _dot(ysb_ref[rows, :], wo_ref[D_HEADS:2 * D_HEADS, :]))
        x1 = x_ref[rows, :] + _rms(mix, g_post_ref[...])
        x1_ref[rows, :] = x1
        h_ref[rows, :] = _rms(x1, g_pre_ref[...]).astype(BF16)
    for rows in subs:
        for cols in chunks:
            h = h_ref[rows, :]
            gate = _dot(h, wg_ref[:, cols])
            up = _dot(h, wu_ref[:, cols])
            act_ref[rows, cols] = (_silu(gate) * up).astype(BF16)
    for rows in subs:
        f = _dot(act_ref[rows, :], wd_ref[...])
        o_ref[rows, :] = x1_ref[rows, :] + _rms(f, g_out_ref[...])


def _out_ffn(x2, y_ssd, y_sb, w_out, g_post, g_pre, wg, wu, wd, g_out):
    m = x2.shape[0]
    row = lambda i: (i, 0)
    const2 = lambda i: (0, 0)
    single = pl.Buffered(1)
    return pl.pallas_call(
        _ffn_kernel,
        grid=(m // TM_FFN,),
        in_specs=[
            pl.BlockSpec((TM_FFN, D_MODEL), row),
            pl.BlockSpec((TM_FFN, D_HEADS), row),
            pl.BlockSpec((TM_FFN, D_HEADS), row),
            pl.BlockSpec((2 * D_HEADS, D_MODEL), const2, pipeline_mode=single),
            pl.BlockSpec((1, D_MODEL), const2),
            pl.BlockSpec((1, D_MODEL), const2),
            pl.BlockSpec((D_MODEL, D_FF), const2, pipeline_mode=single),
            pl.BlockSpec((D_MODEL, D_FF), const2, pipeline_mode=single),
            pl.BlockSpec((D_FF, D_MODEL), const2, pipeline_mode=single),
            pl.BlockSpec((1, D_MODEL), const2),
        ],
        out_specs=pl.BlockSpec((TM_FFN, D_MODEL), row),
        out_shape=jax.ShapeDtypeStruct((m, D_MODEL), F32),
        scratch_shapes=[
            pltpu.VMEM((TM_FFN, D_MODEL), F32),
            pltpu.VMEM((TM_FFN, D_MODEL), BF16),
            pltpu.VMEM((TM_FFN, D_FF), BF16),
        ],
        compiler_params=pltpu.CompilerParams(
            dimension_semantics=("arbitrary",), vmem_limit_bytes=VMEM_LIMIT),
        name="out_ffn",
    )(x2, y_ssd, y_sb, w_out, g_post, g_pre, wg, wu, wd, g_out)


def _expand_heads(v):
    return jnp.repeat(v.astype(F32), HEAD_DIM)[None, :]


def _pad_lanes(v):
    return jnp.pad(v.astype(F32), (0, LANES - v.shape[0]))[None, :]


def _layer(x2, batch, seq, pre_mix_gain, w_in, conv_w, conv_b, dt_bias, a_log, d_skip,
           ssd_norm_gain, sb_norm_gain, w_out, post_mix_gain, pre_ffn_gain, w_gate, w_up,
           w_down, post_ffn_gain):
    o_xbc, o_dt = D_HEADS, D_HEADS + D_CONV
    o_q = o_dt + N_HEADS
    scale = 1.0 / math.sqrt(HEAD_DIM)
    w_all = jnp.concatenate([
        w_in[:, 0:o_dt],
        jnp.pad(w_in[:, o_dt:o_q], ((0, 0), (0, LANES - N_HEADS))),
        w_in[:, o_q:o_q + D_HEADS] * scale,
        w_in[:, o_q + D_HEADS:],
    ], axis=1).astype(BF16)

    sz, xbc, dt, q, k, v = _inproj(x2, pre_mix_gain[None, :], w_all, _pad_lanes(dt_bias))

    utri = jnp.triu(jnp.ones((T_SSD, T_SSD), BF16))
    a_row = _pad_lanes(-jnp.exp(a_log.astype(F32)))
    y_ssd = _ssd(sz, xbc, dt, conv_w, conv_b[None, :], a_row, _expand_heads(d_skip),
                 ssd_norm_gain[None, :], utri, batch, seq)

    jj = jnp.arange(T_ATT)
    later = (jj[:, None] > jj[None, :]).astype(BF16)
    half = jnp.concatenate([later, jnp.ones((T_ATT, T_ATT), BF16)], axis=1)
    w2 = jnp.concatenate([half, half], axis=0)
    y_sb = _attention(q, k, v, w2, sb_norm_gain[None, :], batch, seq)

    return _out_ffn(x2, y_ssd, y_sb, w_out.astype(BF16), post_mix_gain[None, :],
                    pre_ffn_gain[None, :], w_gate.astype(BF16), w_up.astype(BF16),
                    w_down.astype(BF16), post_ffn_gain[None, :])


def kernel(x, pre_mix_gain, w_in, conv_w, conv_b, dt_bias, a_log, d_skip, ssd_norm_gain,
           sb_norm_gain, w_out, post_mix_gain, pre_ffn_gain, w_gate, w_up, w_down,
           post_ffn_gain):
    batch, seq, d = x.shape
    x2 = x.reshape(batch * seq, d)
    params = (pre_mix_gain, w_in, conv_w, conv_b, dt_bias, a_log, d_skip, ssd_norm_gain,
              sb_norm_gain, w_out, post_mix_gain, pre_ffn_gain, w_gate, w_up, w_down,
              post_ffn_gain)
    for layer in range(pre_mix_gain.shape[0]):
        x2 = _layer(x2, batch, seq, *(p[layer] for p in params))
    return x2.reshape(batch, seq, d)
```

```python
import functools
import itertools
import math

import jax
import jax.numpy as jnp
from jax import lax
from jax.experimental import pallas as pl
from jax.experimental.pallas import tpu as pltpu

F32 = jnp.float32
BF16 = jnp.bfloat16

EPS = 1e-6
LANES = 128

D_MODEL = 1024
N_HEADS = 8
HEAD_DIM = 64
D_HEADS = N_HEADS * HEAD_DIM
N_PAIRS = N_HEADS // 2
SSD_GROUPS = 2
SSD_STATE = 128
CONV_WIDTH = 4
D_CONV = D_HEADS + 2 * SSD_GROUPS * SSD_STATE
D_FF = 2816

C_Z = 0
C_XBC = C_Z + D_HEADS
C_DT = C_XBC + D_CONV
C_Q = C_DT + LANES
C_K = C_Q + D_HEADS
C_V = C_K + D_HEADS
C_END = C_V + D_HEADS

TM_PROJ = 512
PROJ_PIECE = 256
T_SSD = 128
SSD_SEGMENTS = 2 + N_PAIRS
T_ATT = 128
TM_FFN = 512
SUB_FFN = 256
FF_CHUNK = 256
CONV_TAIL = 8

VMEM_LIMIT = 56 * 1024 * 1024


def _dot(a, b):
    return jnp.dot(a, b, preferred_element_type=F32)


def _dot_nt(a, b):
    return lax.dot_general(a, b, (((1,), (1,)), ((), ())), preferred_element_type=F32)


def _dot_tn(a, b):
    return lax.dot_general(a, b, (((0,), (0,)), ((), ())), preferred_element_type=F32)


def _split3(x):
    hi = x.astype(BF16)
    r = x - hi.astype(F32)
    mid = r.astype(BF16)
    lo = (r - mid.astype(F32)).astype(BF16)
    return hi, mid, lo


def _silu(x):
    return x / (1.0 + jnp.exp(-x))


def _softplus(x):
    e = jnp.exp(-jnp.abs(x))
    u = 1.0 + e
    tiny = u == 1.0
    log1p_e = jnp.where(tiny, e, jnp.log(u) * (e / jnp.where(tiny, 1.0, u - 1.0)))
    return jnp.maximum(x, 0.0) + log1p_e


def _projection_pieces(x_ref, g_ref, w_ref, dtb_ref, h_ref, sz_ref, xbc_ref, dt_ref, q_ref,
                       k_ref, v_ref):
    x = x_ref[...]
    ms = jnp.mean(x * x, axis=-1, keepdims=True)
    h_ref[...] = (x * lax.rsqrt(ms + EPS) * g_ref[...]).astype(BF16)
    to_bf16 = lambda y: y.astype(BF16)
    segments = [(sz_ref, C_Z, C_XBC, _silu), (xbc_ref, C_XBC, C_DT, lambda y: y),
                (dt_ref, C_DT, C_Q, lambda y: _softplus(y + dtb_ref[...])),
                (q_ref, C_Q, C_K, to_bf16), (k_ref, C_K, C_V, to_bf16),
                (v_ref, C_V, C_END, to_bf16)]

    def piece(out_ref, c0, lo, hi, post):
        def run():
            out_ref[:, lo:hi] = post(_dot(h_ref[...], w_ref[:, c0 + lo:c0 + hi]))
        return run

    return [piece(out_ref, c0, lo, min(lo + PROJ_PIECE, c1 - c0), post)
            for out_ref, c0, c1, post in segments for lo in range(0, c1 - c0, PROJ_PIECE)]


def _proj_ssd_kernel(blocks_per_seq, x_ref, g_ref, w_ref, dtb_ref, cw_ref, cb_ref, a_ref,
                     dskip_ref, gain_ref, utri_ref, q_ref, k_ref, v_ref, y_ref,
                     h_ref, sz_ref, xbc_ref, dt_ref, ext_ref, state_ref):
    g = pl.program_id(0)

    @pl.when(g == 0)
    def _():
        sz_ref[1] = jnp.zeros(sz_ref.shape[1:], F32)
        xbc_ref[1] = jnp.zeros(xbc_ref.shape[1:], F32)
        dt_ref[1] = jnp.zeros(dt_ref.shape[1:], F32)

    @pl.when(lax.rem(g + blocks_per_seq - 1, blocks_per_seq) == 0)
    def _():
        ext_ref[0:CONV_TAIL, :] = jnp.zeros((CONV_TAIL, D_CONV), F32)
        state_ref[...] = jnp.zeros(state_ref.shape, F32)

    def step(new, old):
        pieces = _projection_pieces(x_ref, g_ref, w_ref, dtb_ref, h_ref, sz_ref.at[new],
                                    xbc_ref.at[new], dt_ref.at[new], q_ref, k_ref, v_ref)
        chunks = [pl.ds(c * T_SSD, T_SSD) for c in range(TM_PROJ // T_SSD)]
        segments = itertools.chain.from_iterable(
            _ssd_chunk(sz_ref.at[old, rows], xbc_ref.at[old, rows], dt_ref.at[old, rows],
                       cw_ref, cb_ref, a_ref, dskip_ref, gain_ref, utri_ref, y_ref.at[rows],
                       ext_ref, state_ref) for rows in chunks)
        n_segments = len(chunks) * SSD_SEGMENTS
        assert len(pieces) <= n_segments
        slot = {(n * n_segments) // len(pieces): run for n, run in enumerate(pieces)}
        for s in range(n_segments):
            if s in slot:
                slot[s]()
            next(segments)

    @pl.when(lax.rem(g, 2) == 0)
    def _():
        step(0, 1)

    @pl.when(lax.rem(g, 2) == 1)
    def _():
        step(1, 0)


def _proj_ssd(x2, gain, w_all, dtb, conv_w, conv_b, a_row, dskip, ssd_gain, utri, seq):
    m = x2.shape[0]
    n_blocks = m // TM_PROJ
    assert seq % TM_PROJ == 0 and TM_PROJ % T_SSD == 0
    cur = lambda g: (jnp.minimum(g, n_blocks - 1), 0)
    prev = lambda g: (jnp.maximum(g - 1, 0), 0)
    const = lambda g: (0, 0)
    qkv_spec = pl.BlockSpec((TM_PROJ, D_HEADS), cur)
    qkv_shape = jax.ShapeDtypeStruct((m, D_HEADS), BF16)
    return pl.pallas_call(
        functools.partial(_proj_ssd_kernel, seq // TM_PROJ),
        grid=(n_blocks + 1,),
        in_specs=[
            pl.BlockSpec((TM_PROJ, D_MODEL), cur),
            pl.BlockSpec((1, D_MODEL), const),
            pl.BlockSpec((D_MODEL, C_END), const, pipeline_mode=pl.Buffered(1)),
            pl.BlockSpec((1, LANES), const),
            pl.BlockSpec((CONV_WIDTH, D_CONV), const),
            pl.BlockSpec((1, D_CONV), const),
            pl.BlockSpec((1, LANES), const),
            pl.BlockSpec((1, D_HEADS), const),
            pl.BlockSpec((1, D_HEADS), const),
            pl.BlockSpec((T_SSD, T_SSD), const),
        ],
        out_specs=[qkv_spec, qkv_spec, qkv_spec, pl.BlockSpec((TM_PROJ, D_HEADS), prev)],
        out_shape=[qkv_shape, qkv_shape, qkv_shape, qkv_shape],
        scratch_shapes=[
            pltpu.VMEM((TM_PROJ, D_MODEL), BF16),
            pltpu.VMEM((2, TM_PROJ, D_HEADS), F32),
            pltpu.VMEM((2, TM_PROJ, D_CONV), F32),
            pltpu.VMEM((2, TM_PROJ, LANES), F32),
            pltpu.VMEM((T_SSD + CONV_TAIL, D_CONV), F32),
            pltpu.VMEM((N_PAIRS, SSD_STATE, LANES), F32),
        ],
        compiler_params=pltpu.CompilerParams(
            dimension_semantics=("arbitrary",), vmem_limit_bytes=VMEM_LIMIT),
        name="proj_ssd",
    )(x2, gain, w_all, dtb, conv_w, conv_b, a_row, dskip, ssd_gain, utri)


def _ssd_chunk(sz_ref, xbc_ref, dt_ref, cw_ref, cb_ref, a_ref, dskip_ref, gain_ref, utri_ref,
               y_ref, ext_ref, state_ref):
    t = T_SSD

    ext_ref[CONV_TAIL:CONV_TAIL + t, :] = xbc_ref[...]
    conv = cb_ref[...]
    for k in range(CONV_WIDTH):
        off = CONV_TAIL - (CONV_WIDTH - 1) + k
        conv = conv + ext_ref[off:off + t, :] * cw_ref[k:k + 1, :]
    ext_ref[0:CONV_TAIL, :] = ext_ref[t:t + CONV_TAIL, :]
    xa = _silu(conv)
    yield

    dtv = dt_ref[...]
    adt = dtv * a_ref[...]
    utri = utri_ref[...]
    acs_t = sum(_dot(part, utri) for part in _split3(adt.T))
    acs = acs_t.T

    lane = lax.broadcasted_iota(jnp.int32, (t, LANES), 1)
    first_head = lane < HEAD_DIM
    li = lax.broadcasted_iota(jnp.int32, (t, t), 0)
    si = lax.broadcasted_iota(jnp.int32, (t, t), 1)
    causal = li >= si

    def col(v, h):
        return jnp.broadcast_to(v[:, h:h + 1], (t, LANES))

    y_blocks = []
    cb_mats = []
    for g in range(SSD_GROUPS):
        bm = xa[:, D_HEADS + g * SSD_STATE:D_HEADS + (g + 1) * SSD_STATE].astype(BF16)
        cm = xa[:, D_HEADS + (SSD_GROUPS + g) * SSD_STATE:
                D_HEADS + (SSD_GROUPS + g + 1) * SSD_STATE].astype(BF16)
        cb_mats.append((bm, cm, _dot_nt(cm, bm)))
    yield

    for p in range(N_PAIRS):
        ha, hb = 2 * p, 2 * p + 1
        bm, cm, cbm = cb_mats[p // (N_PAIRS // SSD_GROUPS)]
        x2 = xa[:, p * LANES:(p + 1) * LANES]
        dt2 = jnp.where(first_head, col(dtv, ha), col(dtv, hb))
        acs2 = jnp.where(first_head, col(acs, ha), col(acs, hb))
        xdt2 = x2 * dt2

        def decay(h):
            seg = col(acs, h) - jnp.broadcast_to(acs_t[h:h + 1, :], (t, t))
            return (cbm * jnp.exp(jnp.where(causal, seg, -jnp.inf))).astype(BF16)

        m2 = jnp.concatenate([decay(ha), decay(hb)], axis=1)
        xdt_a = jnp.where(first_head, xdt2, 0.0).astype(BF16)
        xdt_b = jnp.where(first_head, 0.0, xdt2).astype(BF16)
        y_diag = _dot(m2, jnp.concatenate([xdt_a, xdt_b], axis=0))

        prev = state_ref[p]
        y_off = _dot(cm, prev.astype(BF16)) * jnp.exp(acs2)
        last = acs2[t - 1:t, :]
        xs = (xdt2 * jnp.exp(last - acs2)).astype(BF16)
        state_ref[p] = prev * jnp.exp(last) + _dot_tn(bm, xs)

        y_blocks.append(y_diag + y_off + dskip_ref[:, p * LANES:(p + 1) * LANES] * x2)
        if p + 1 < N_PAIRS:
            yield

    per_group = N_PAIRS // SSD_GROUPS
    for g in range(SSD_GROUPS):
        ys = []
        for p in range(g * per_group, (g + 1) * per_group):
            ys.append(y_blocks[p] * sz_ref[:, p * LANES:(p + 1) * LANES])
        ss = sum(jnp.sum(y * y, axis=-1, keepdims=True) for y in ys)
        inv = lax.rsqrt(ss * (1.0 / (per_group * LANES)) + EPS)
        for j, y in enumerate(ys):
            p = g * per_group + j
            y_ref[:, p * LANES:(p + 1) * LANES] = (
                y * inv * gain_ref[:, p * LANES:(p + 1) * LANES]).astype(BF16)
    yield


LOG2E = 1.4426950408889634
ATT_DEPTH = 4
ATT_NEAR_DIAGONALS = 3
ATT_UNDERFLOW_LOG2 = -160.0


def _attn_kernel(ni_ref, nj_ref, fi_ref, fj_ref, q_ref, k_ref, v_ref, w2_ref, gain_ref, o_ref,
                 q2_ref, vst_ref, r_ref, acc_ref, z_ref, arg_ref):
    t = T_ATT
    n_blocks = q2_ref.shape[0] - 1
    depth = ATT_DEPTH

    lane_v = lax.broadcasted_iota(jnp.int32, (t, LANES), 1)
    first_v = lane_v < HEAD_DIM

    for j in range(n_blocks):
        qb = q_ref[j * t:(j + 1) * t, :]
        vb = v_ref[j * t:(j + 1) * t, :]
        zero = jnp.zeros_like(qb)
        q2_ref[j, 0:t, :] = jnp.where(first_v, qb, zero)
        q2_ref[j, t:2 * t, :] = jnp.where(first_v, zero, qb)
        vst_ref[j, 0:t, :] = jnp.where(first_v, vb, zero)
        vst_ref[j, t:2 * t, :] = jnp.where(first_v, zero, vb)
    q2_ref[n_blocks] = jnp.zeros(q2_ref.shape[1:], BF16)
    r_ref[n_blocks] = jnp.zeros(r_ref.shape[1:], F32)
    acc_ref[n_blocks] = jnp.zeros(acc_ref.shape[1:], F32)

    ti = lax.broadcasted_iota(jnp.int32, (2 * t, t), 0)
    si = lax.broadcasted_iota(jnp.int32, (2 * t, t), 1)
    strict = si < jnp.where(ti >= t, ti - t, ti)

    def key_rows(j):
        off = j * t
        return pl.ds(off if isinstance(off, int) else pl.multiple_of(off, t), t)

    def stage_scores(tiles, z_buf):
        for u, (i, j) in enumerate(tiles):
            z_buf[u] = _dot_nt(q2_ref[i], k_ref[key_rows(j), :])

    def stage_suffix_sums(tiles, z_buf, diag):
        for u, (i, _) in enumerate(tiles):
            zn = z_buf[u] * (-LOG2E)
            l1 = jnp.minimum(zn, 0.0) - jnp.log2(1.0 + jnp.exp2(-jnp.abs(zn)))
            zl = l1 - zn
            if diag:
                l1 = jnp.where(strict, l1, 0.0)
                zl = jnp.where(strict, zl, -jnp.inf)
            hi = l1.astype(BF16)
            lo = (l1 - hi.astype(F32)).astype(BF16)
            res = _dot(jnp.concatenate([hi, lo], axis=1), w2_ref[...])
            arg = zl + res[:, 0:t]
            tot = res[:, t:2 * t]
            if not diag:
                r_old = r_ref[i]
                arg = arg + r_old
                tot = tot + r_old
            arg_ref[u] = arg
            r_ref[i] = tot

    def stage_values(tiles, diag):
        for u, (i, j) in enumerate(tiles):
            w = jnp.exp2(arg_ref[u]).astype(BF16)
            contrib = _dot(jnp.concatenate([w[0:t, :], w[t:2 * t, :]], axis=1), vst_ref[j])
            if diag:
                acc_ref[i] = contrib
            else:
                acc_ref[i] += contrib

    def sweep(i_ref, j_ref, n_diag_groups):
        n_groups = i_ref.shape[0] // depth
        group = lambda g: [(i_ref[g * depth + u], j_ref[g * depth + u]) for u in range(depth)]
        n_static = n_diag_groups + 2
        assert n_static % 2 == 0 and n_groups % 2 == 0 and n_groups >= n_static

        def iteration(m, parity, static):
            if not static or m < n_groups:
                stage_scores(group(m), z_ref.at[parity])
            if not static or 0 <= m - 2 < n_groups:
                stage_values(group(m - 2), static and m - 2 < n_diag_groups)
            if not static or 0 <= m - 1 < n_groups:
                stage_suffix_sums(group(m - 1), z_ref.at[1 - parity],
                                  static and m - 1 < n_diag_groups)

        for m in range(n_static):
            iteration(m, m % 2, True)

        def body(mm, carry):
            iteration(2 * mm, 0, False)
            iteration(2 * mm + 1, 1, False)
            return carry

        lax.fori_loop(n_static // 2, n_groups // 2, body, 0)
        for m in (n_groups, n_groups + 1):
            iteration(m, m % 2, True)

    sweep(ni_ref, nj_ref, pl.cdiv(n_blocks, depth))

    far_blocks = range(ATT_NEAR_DIAGONALS, n_blocks)
    r_max = functools.reduce(jnp.maximum, [r_ref[i] for i in far_blocks])

    @pl.when(jnp.max(r_max) >= ATT_UNDERFLOW_LOG2)
    def _():
        sweep(fi_ref, fj_ref, 0)

    gain = gain_ref[...]
    for i in range(n_blocks):
        o = acc_ref[i]
        o2 = o * o
        ss_a = jnp.sum(jnp.where(first_v, o2, 0.0), axis=-1, keepdims=True)
        ss_b = jnp.sum(jnp.where(first_v, 0.0, o2), axis=-1, keepdims=True)
        ms = jnp.where(first_v, ss_a, ss_b) * (1.0 / HEAD_DIM)
        o_ref[i * t:(i + 1) * t, :] = (o * lax.rsqrt(ms + EPS) * gain).astype(BF16)


def _attention(q, k, v, w2, gain, batch, seq):
    n_q = seq // T_ATT
    assert n_q % ATT_DEPTH == 0

    def tile_list(diagonals):
        tiles = [(i, i - d) for d in diagonals for i in range(d, n_q)]
        tiles += [(n_q, 0)] * (-len(tiles) % (2 * ATT_DEPTH))
        return (jnp.asarray([i for i, _ in tiles], jnp.int32),
                jnp.asarray([j for _, j in tiles], jnp.int32))

    near = tile_list(range(ATT_NEAR_DIAGONALS))
    far = tile_list(range(ATT_NEAR_DIAGONALS, n_q))
    seq_blk = lambda b, p, *_: (b, p)
    return pl.pallas_call(
        _attn_kernel,
        grid_spec=pltpu.PrefetchScalarGridSpec(
            num_scalar_prefetch=4,
            grid=(batch, N_PAIRS),
            in_specs=[
                pl.BlockSpec((seq, LANES), seq_blk),
                pl.BlockSpec((seq, LANES), seq_blk),
                pl.BlockSpec((seq, LANES), seq_blk),
                pl.BlockSpec((2 * T_ATT, 2 * T_ATT), lambda b, p, *_: (0, 0)),
                pl.BlockSpec((1, LANES), lambda b, p, *_: (0, p)),
            ],
            out_specs=pl.BlockSpec((seq, LANES), seq_blk),
            scratch_shapes=[
                pltpu.VMEM((n_q + 1, 2 * T_ATT, LANES), BF16),
                pltpu.VMEM((n_q, 2 * T_ATT, LANES), BF16),
                pltpu.VMEM((n_q + 1, 2 * T_ATT, T_ATT), F32),
                pltpu.VMEM((n_q + 1, T_ATT, LANES), F32),
                pltpu.VMEM((2, ATT_DEPTH, 2 * T_ATT, T_ATT), F32),
                pltpu.VMEM((ATT_DEPTH, 2 * T_ATT, T_ATT), F32),
            ],
        ),
        out_shape=jax.ShapeDtypeStruct((batch * seq, D_HEADS), BF16),
        compiler_params=pltpu.CompilerParams(
            dimension_semantics=("arbitrary", "arbitrary"),
            vmem_limit_bytes=VMEM_LIMIT),
        name="sb_attention",
    )(*near, *far, q, k, v, w2, gain)


def _rms(x, g):
    ms = jnp.mean(x * x, axis=-1, keepdims=True)
    return x * lax.rsqrt(ms + EPS) * g


def _ffn_kernel(x_ref, yssd_ref, ysb_ref, wo_ref, g_post_ref, g_pre_ref, wg_ref, wu_ref,
                wd_ref, g_out_ref, o_ref, x1_ref, h_ref, act_ref):
    chunks = [slice(c * FF_CHUNK, (c + 1) * FF_CHUNK) for c in range(D_FF // FF_CHUNK)]
    subs = [pl.ds(s * SUB_FFN, SUB_FFN) for s in range(TM_FFN // SUB_FFN)]
    for rows in subs:
        mix = (_dot(yssd_ref[rows, :], wo_ref[0:D_HEADS, :])
               + _dot(ysb_ref[rows, :], wo_ref[D_HEADS:2 * D_HEADS, :]))
        x1 = x_ref[rows, :] + _rms(mix, g_post_ref[...])
        x1_ref[rows, :] = x1
        h_ref[rows, :] = _rms(x1, g_pre_ref[...]).astype(BF16)
    for rows in subs:
        for cols in chunks:
            h = h_ref[rows, :]
            gate = _dot(h, wg_ref[:, cols])
            up = _dot(h, wu_ref[:, cols])
            act_ref[rows, cols] = (_silu(gate) * up).astype(BF16)
    for rows in subs:
        f = _dot(act_ref[rows, :], wd_ref[...])
        o_ref[rows, :] = x1_ref[rows, :] + _rms(f, g_out_ref[...])


def _out_ffn(x2, y_ssd, y_sb, w_out, g_post, g_pre, wg, wu, wd, g_out):
    m = x2.shape[0]
    row = lambda i: (i, 0)
    const2 = lambda i: (0, 0)
    single = pl.Buffered(1)
    return pl.pallas_call(
        _ffn_kernel,
        grid=(m // TM_FFN,),
        in_specs=[
            pl.BlockSpec((TM_FFN, D_MODEL), row),
            pl.BlockSpec((TM_FFN, D_HEADS), row),
            pl.BlockSpec((TM_FFN, D_HEADS), row),
            pl.BlockSpec((2 * D_HEADS, D_MODEL), const2, pipeline_mode=single),
            pl.BlockSpec((1, D_MODEL), const2),
            pl.BlockSpec((1, D_MODEL), const2),
            pl.BlockSpec((D_MODEL, D_FF), const2, pipeline_mode=single),
            pl.BlockSpec((D_MODEL, D_FF), const2, pipeline_mode=single),
            pl.BlockSpec((D_FF, D_MODEL), const2, pipeline_mode=single),
            pl.BlockSpec((1, D_MODEL), const2),
        ],
        out_specs=pl.BlockSpec((TM_FFN, D_MODEL), row),
        out_shape=jax.ShapeDtypeStruct((m, D_MODEL), F32),
        scratch_shapes=[
            pltpu.VMEM((TM_FFN, D_MODEL), F32),
            pltpu.VMEM((TM_FFN, D_MODEL), BF16),
            pltpu.VMEM((TM_FFN, D_FF), BF16),
        ],
        compiler_params=pltpu.CompilerParams(
            dimension_semantics=("arbitrary",), vmem_limit_bytes=VMEM_LIMIT),
        name="out_ffn",
    )(x2, y_ssd, y_sb, w_out, g_post, g_pre, wg, wu, wd, g_out)


def _expand_heads(v):
    return jnp.repeat(v.astype(F32), HEAD_DIM)[None, :]


def _pad_lanes(v):
    return jnp.pad(v.astype(F32), (0, LANES - v.shape[0]))[None, :]


def _layer(x2, batch, seq, pre_mix_gain, w_in, conv_w, conv_b, dt_bias, a_log, d_skip,
           ssd_norm_gain, sb_norm_gain, w_out, post_mix_gain, pre_ffn_gain, w_gate, w_up,
           w_down, post_ffn_gain):
    o_xbc, o_dt = D_HEADS, D_HEADS + D_CONV
    o_q = o_dt + N_HEADS
    scale = 1.0 / math.sqrt(HEAD_DIM)
    w_all = jnp.concatenate([
        w_in[:, 0:o_dt],
        jnp.pad(w_in[:, o_dt:o_q], ((0, 0), (0, LANES - N_HEADS))),
        w_in[:, o_q:o_q + D_HEADS] * scale,
        w_in[:, o_q + D_HEADS:],
    ], axis=1).astype(BF16)

    utri = jnp.triu(jnp.ones((T_SSD, T_SSD), BF16))
    a_row = _pad_lanes(-jnp.exp(a_log.astype(F32)))
    q, k, v, y_ssd = _proj_ssd(x2, pre_mix_gain[None, :], w_all, _pad_lanes(dt_bias), conv_w,
                               conv_b[None, :], a_row, _expand_heads(d_skip),
                               ssd_norm_gain[None, :], utri, seq)

    jj = jnp.arange(T_ATT)
    later = (jj[:, None] > jj[None, :]).astype(BF16)
    half = jnp.concatenate([later, jnp.ones((T_ATT, T_ATT), BF16)], axis=1)
    w2 = jnp.concatenate([half, half], axis=0)
    y_sb = _attention(q, k, v, w2, sb_norm_gain[None, :], batch, seq)

    return _out_ffn(x2, y_ssd, y_sb, w_out.astype(BF16), post_mix_gain[None, :],
                    pre_ffn_gain[None, :], w_gate.astype(BF16), w_up.astype(BF16),
                    w_down.astype(BF16), post_ffn_gain[None, :])


def kernel(x, pre_mix_gain, w_in, conv_w, conv_b, dt_bias, a_log, d_skip, ssd_norm_gain,
           sb_norm_gain, w_out, post_mix_gain, pre_ffn_gain, w_gate, w_up, w_down,
           post_ffn_gain):
    batch, seq, d = x.shape
    x2 = x.reshape(batch * seq, d)
    params = (pre_mix_gain, w_in, conv_w, conv_b, dt_bias, a_log, d_skip, ssd_norm_gain,
              sb_norm_gain, w_out, post_mix_gain, pre_ffn_gain, w_gate, w_up, w_down,
              post_ffn_gain)
    for layer in range(pre_mix_gain.shape[0]):
        x2 = _layer(x2, batch, seq, *(p[layer] for p in params))
    return x2.reshape(batch, seq, d)
```

```python
import functools
import itertools
import math

import jax
import jax.numpy as jnp
from jax import lax
from jax.experimental import pallas as pl
from jax.experimental.pallas import tpu as pltpu

F32 = jnp.float32
BF16 = jnp.bfloat16

EPS = 1e-6
LANES = 128

D_MODEL = 1024
N_HEADS = 8
HEAD_DIM = 64
D_HEADS = N_HEADS * HEAD_DIM
N_PAIRS = N_HEADS // 2
SSD_GROUPS = 2
SSD_STATE = 128
CONV_WIDTH = 4
D_CONV = D_HEADS + 2 * SSD_GROUPS * SSD_STATE
D_FF = 2816

C_Z = 0
C_XBC = C_Z + D_HEADS
C_DT = C_XBC + D_CONV
C_Q = C_DT + LANES
C_K = C_Q + D_HEADS
C_V = C_K + D_HEADS
C_END = C_V + D_HEADS

TM_PROJ = 512
PROJ_PIECE = 256
T_SSD = 128
SSD_SEGMENTS = 2 + N_PAIRS
T_ATT = 128
TM_FFN = 512
SUB_FFN = 256
FF_CHUNK = 256
CONV_TAIL = 8

VMEM_LIMIT = 56 * 1024 * 1024


def _dot(a, b):
    return jnp.dot(a, b, preferred_element_type=F32)


def _dot_nt(a, b):
    return lax.dot_general(a, b, (((1,), (1,)), ((), ())), preferred_element_type=F32)


def _dot_tn(a, b):
    return lax.dot_general(a, b, (((0,), (0,)), ((), ())), preferred_element_type=F32)


def _split3(x):
    hi = x.astype(BF16)
    r = x - hi.astype(F32)
    mid = r.astype(BF16)
    lo = (r - mid.astype(F32)).astype(BF16)
    return hi, mid, lo


def _silu(x):
    return x / (1.0 + jnp.exp(-x))


def _softplus(x):
    e = jnp.exp(-jnp.abs(x))
    u = 1.0 + e
    tiny = u == 1.0
    log1p_e = jnp.where(tiny, e, jnp.log(u) * (e / jnp.where(tiny, 1.0, u - 1.0)))
    return jnp.maximum(x, 0.0) + log1p_e


def _projection_pieces(x_ref, g_ref, w_ref, dtb_ref, h_ref, sz_ref, xbc_ref, dt_ref, q_ref,
                       k_ref, v_ref):
    x = x_ref[...]
    ms = jnp.mean(x * x, axis=-1, keepdims=True)
    h_ref[...] = (x * lax.rsqrt(ms + EPS) * g_ref[...]).astype(BF16)
    to_bf16 = lambda y: y.astype(BF16)
    segments = [(sz_ref, C_Z, C_XBC, _silu), (xbc_ref, C_XBC, C_DT, lambda y: y),
                (dt_ref, C_DT, C_Q, lambda y: _softplus(y + dtb_ref[...])),
                (q_ref, C_Q, C_K, to_bf16), (k_ref, C_K, C_V, to_bf16),
                (v_ref, C_V, C_END, to_bf16)]

    def piece(out_ref, c0, lo, hi, post):
        def run():
            out_ref[:, lo:hi] = post(_dot(h_ref[...], w_ref[:, c0 + lo:c0 + hi]))
        return run

    return [piece(out_ref, c0, lo, min(lo + PROJ_PIECE, c1 - c0), post)
            for out_ref, c0, c1, post in segments for lo in range(0, c1 - c0, PROJ_PIECE)]


def _proj_ssd_kernel(blocks_per_seq, x_ref, g_ref, w_ref, dtb_ref, cw_ref, cb_ref, a_ref,
                     dskip_ref, gain_ref, utri_ref, q_ref, k_ref, v_ref, y_ref,
                     h_ref, sz_ref, xbc_ref, dt_ref, ext_ref, state_ref):
    g = pl.program_id(0)

    @pl.when(g == 0)
    def _():
        sz_ref[1] = jnp.zeros(sz_ref.shape[1:], F32)
        xbc_ref[1] = jnp.zeros(xbc_ref.shape[1:], F32)
        dt_ref[1] = jnp.zeros(dt_ref.shape[1:], F32)

    @pl.when(lax.rem(g + blocks_per_seq - 1, blocks_per_seq) == 0)
    def _():
        ext_ref[0:CONV_TAIL, :] = jnp.zeros((CONV_TAIL, D_CONV), F32)
        state_ref[...] = jnp.zeros(state_ref.shape, F32)

    def step(new, old):
        pieces = _projection_pieces(x_ref, g_ref, w_ref, dtb_ref, h_ref, sz_ref.at[new],
                                    xbc_ref.at[new], dt_ref.at[new], q_ref, k_ref, v_ref)
        chunks = [pl.ds(c * T_SSD, T_SSD) for c in range(TM_PROJ // T_SSD)]
        segments = itertools.chain.from_iterable(
            _ssd_chunk(sz_ref.at[old, rows], xbc_ref.at[old, rows], dt_ref.at[old, rows],
                       cw_ref, cb_ref, a_ref, dskip_ref, gain_ref, utri_ref, y_ref.at[rows],
                       ext_ref, state_ref) for rows in chunks)
        n_segments = len(chunks) * SSD_SEGMENTS
        assert len(pieces) <= n_segments
        slot = {(n * n_segments) // len(pieces): run for n, run in enumerate(pieces)}
        for s in range(n_segments):
            if s in slot:
                slot[s]()
            next(segments)

    @pl.when(lax.rem(g, 2) == 0)
    def _():
        step(0, 1)

    @pl.when(lax.rem(g, 2) == 1)
    def _():
        step(1, 0)


def _proj_ssd(x2, gain, w_all, dtb, conv_w, conv_b, a_row, dskip, ssd_gain, utri, seq):
    m = x2.shape[0]
    n_blocks = m // TM_PROJ
    assert seq % TM_PROJ == 0 and TM_PROJ % T_SSD == 0
    cur = lambda g: (jnp.minimum(g, n_blocks - 1), 0)
    prev = lambda g: (jnp.maximum(g - 1, 0), 0)
    const = lambda g: (0, 0)
    qkv_spec = pl.BlockSpec((TM_PROJ, D_HEADS), cur)
    qkv_shape = jax.ShapeDtypeStruct((m, D_HEADS), BF16)
    return pl.pallas_call(
        functools.partial(_proj_ssd_kernel, seq // TM_PROJ),
        grid=(n_blocks + 1,),
        in_specs=[
            pl.BlockSpec((TM_PROJ, D_MODEL), cur),
            pl.BlockSpec((1, D_MODEL), const),
            pl.BlockSpec((D_MODEL, C_END), const, pipeline_mode=pl.Buffered(1)),
            pl.BlockSpec((1, LANES), const),
            pl.BlockSpec((CONV_WIDTH, D_CONV), const),
            pl.BlockSpec((1, D_CONV), const),
            pl.BlockSpec((1, LANES), const),
            pl.BlockSpec((1, D_HEADS), const),
            pl.BlockSpec((1, D_HEADS), const),
            pl.BlockSpec((T_SSD, T_SSD), const),
        ],
        out_specs=[qkv_spec, qkv_spec, qkv_spec, pl.BlockSpec((TM_PROJ, D_HEADS), prev)],
        out_shape=[qkv_shape, qkv_shape, qkv_shape, qkv_shape],
        scratch_shapes=[
            pltpu.VMEM((TM_PROJ, D_MODEL), BF16),
            pltpu.VMEM((2, TM_PROJ, D_HEADS), F32),
            pltpu.VMEM((2, TM_PROJ, D_CONV), F32),
            pltpu.VMEM((2, TM_PROJ, LANES), F32),
            pltpu.VMEM((T_SSD + CONV_TAIL, D_CONV), F32),
            pltpu.VMEM((N_PAIRS, SSD_STATE, LANES), F32),
        ],
        compiler_params=pltpu.CompilerParams(
            dimension_semantics=("arbitrary",), vmem_limit_bytes=VMEM_LIMIT),
        name="proj_ssd",
    )(x2, gain, w_all, dtb, conv_w, conv_b, a_row, dskip, ssd_gain, utri)


def _ssd_chunk(sz_ref, xbc_ref, dt_ref, cw_ref, cb_ref, a_ref, dskip_ref, gain_ref, utri_ref,
               y_ref, ext_ref, state_ref):
    t = T_SSD

    ext_ref[CONV_TAIL:CONV_TAIL + t, :] = xbc_ref[...]
    conv = cb_ref[...]
    for k in range(CONV_WIDTH):
        off = CONV_TAIL - (CONV_WIDTH - 1) + k
        conv = conv + ext_ref[off:off + t, :] * cw_ref[k:k + 1, :]
    ext_ref[0:CONV_TAIL, :] = ext_ref[t:t + CONV_TAIL, :]
    xa = _silu(conv)
    yield

    dtv = dt_ref[...]
    adt = dtv * a_ref[...]
    utri = utri_ref[...]
    acs_t = sum(_dot(part, utri) for part in _split3(adt.T))
    acs = acs_t.T

    lane = lax.broadcasted_iota(jnp.int32, (t, LANES), 1)
    first_head = lane < HEAD_DIM
    li = lax.broadcasted_iota(jnp.int32, (t, t), 0)
    si = lax.broadcasted_iota(jnp.int32, (t, t), 1)
    causal = li >= si

    def col(v, h):
        return jnp.broadcast_to(v[:, h:h + 1], (t, LANES))

    y_blocks = []
    cb_mats = []
    for g in range(SSD_GROUPS):
        bm = xa[:, D_HEADS + g * SSD_STATE:D_HEADS + (g + 1) * SSD_STATE].astype(BF16)
        cm = xa[:, D_HEADS + (SSD_GROUPS + g) * SSD_STATE:
                D_HEADS + (SSD_GROUPS + g + 1) * SSD_STATE].astype(BF16)
        cb_mats.append((bm, cm, _dot_nt(cm, bm)))
    yield

    for p in range(N_PAIRS):
        ha, hb = 2 * p, 2 * p + 1
        bm, cm, cbm = cb_mats[p // (N_PAIRS // SSD_GROUPS)]
        x2 = xa[:, p * LANES:(p + 1) * LANES]
        dt2 = jnp.where(first_head, col(dtv, ha), col(dtv, hb))
        acs2 = jnp.where(first_head, col(acs, ha), col(acs, hb))
        xdt2 = x2 * dt2

        def decay(h):
            seg = col(acs, h) - jnp.broadcast_to(acs_t[h:h + 1, :], (t, t))
            return (cbm * jnp.exp(jnp.where(causal, seg, -jnp.inf))).astype(BF16)

        m2 = jnp.concatenate([decay(ha), decay(hb)], axis=1)
        xdt_a = jnp.where(first_head, xdt2, 0.0).astype(BF16)
        xdt_b = jnp.where(first_head, 0.0, xdt2).astype(BF16)
        y_diag = _dot(m2, jnp.concatenate([xdt_a, xdt_b], axis=0))

        prev = state_ref[p]
        y_off = _dot(cm, prev.astype(BF16)) * jnp.exp(acs2)
        last = acs2[t - 1:t, :]
        xs = (xdt2 * jnp.exp(last - acs2)).astype(BF16)
        state_ref[p] = prev * jnp.exp(last) + _dot_tn(bm, xs)

        y_blocks.append(y_diag + y_off + dskip_ref[:, p * LANES:(p + 1) * LANES] * x2)
        if p + 1 < N_PAIRS:
            yield

    per_group = N_PAIRS // SSD_GROUPS
    for g in range(SSD_GROUPS):
        ys = []
        for p in range(g * per_group, (g + 1) * per_group):
            ys.append(y_blocks[p] * sz_ref[:, p * LANES:(p + 1) * LANES])
        ss = sum(jnp.sum(y * y, axis=-1, keepdims=True) for y in ys)
        inv = lax.rsqrt(ss * (1.0 / (per_group * LANES)) + EPS)
        for j, y in enumerate(ys):
            p = g * per_group + j
            y_ref[:, p * LANES:(p + 1) * LANES] = (
                y * inv * gain_ref[:, p * LANES:(p + 1) * LANES]).astype(BF16)
    yield


LOG2E = 1.4426950408889634
ATT_DEPTH = 4
ATT_NEAR_DIAGONALS = 3
ATT_UNDERFLOW_LOG2 = -160.0


def _attention_block(ni_ref, nj_ref, fi_ref, fj_ref, q_ref, k_ref, v_ref, w2_ref, gain_ref,
                     o_ref, q2_ref, kst_ref, vst_ref, r_ref, acc_ref, z_ref, arg_ref, fillers):
    t = T_ATT
    n_blocks = q2_ref.shape[0] - 1
    depth = ATT_DEPTH
    fill_setup, fill_near, fill_final = fillers if fillers is not None else ((), None, ())

    lane_v = lax.broadcasted_iota(jnp.int32, (t, LANES), 1)
    first_v = lane_v < HEAD_DIM

    for run in fill_setup:
        run()

    for j in range(n_blocks):
        kb = k_ref[j * t:(j + 1) * t, :]
        vb = v_ref[j * t:(j + 1) * t, :]
        zero = jnp.zeros_like(kb)
        q2_ref[j] = q_ref[j * t:(j + 1) * t, :]
        kst_ref[j, 0:t, :] = jnp.where(first_v, kb, zero)
        kst_ref[j, t:2 * t, :] = jnp.where(first_v, zero, kb)
        vst_ref[j, 0:t, :] = jnp.where(first_v, vb, zero)
        vst_ref[j, t:2 * t, :] = jnp.where(first_v, zero, vb)
    q2_ref[n_blocks] = jnp.zeros(q2_ref.shape[1:], BF16)
    r_ref[n_blocks] = jnp.zeros(r_ref.shape[1:], F32)
    acc_ref[n_blocks] = jnp.zeros(acc_ref.shape[1:], F32)

    ti = lax.broadcasted_iota(jnp.int32, (t, 2 * t), 0)
    si = lax.broadcasted_iota(jnp.int32, (t, 2 * t), 1)
    strict = jnp.where(si >= t, si - t, si) < ti

    def stage_scores(tiles, z_buf):
        for u, (i, j) in enumerate(tiles):
            z_buf[u] = _dot_nt(q2_ref[i], kst_ref[j])

    def stage_suffix_sums(tiles, z_buf, diag):
        for u, (i, _) in enumerate(tiles):
            zn = z_buf[u] * (-LOG2E)
            l1 = jnp.minimum(zn, 0.0) - jnp.log2(1.0 + jnp.exp2(-jnp.abs(zn)))
            if diag:
                l1 = jnp.where(strict, l1, 0.0)
            hi = l1.astype(BF16)
            lo = (l1 - hi.astype(F32)).astype(BF16)
            res = [_dot(jnp.concatenate([hi[:, h * t:(h + 1) * t], lo[:, h * t:(h + 1) * t]],
                                        axis=1), w2_ref[...]) for h in range(2)]
            arg = jnp.concatenate([r[:, 0:t] for r in res], axis=1) - zn
            tot = jnp.concatenate([r[:, t:2 * t] for r in res], axis=1)
            if diag:
                arg = jnp.where(strict, arg, -jnp.inf)
            else:
                r_old = r_ref[i]
                arg = arg + r_old
                tot = tot + r_old
            arg_ref[u] = arg
            r_ref[i] = tot

    def stage_values(tiles, diag):
        for u, (i, j) in enumerate(tiles):
            contrib = _dot(jnp.exp2(arg_ref[u]).astype(BF16), vst_ref[j])
            if diag:
                acc_ref[i] = contrib
            else:
                acc_ref[i] += contrib

    def sweep(i_ref, j_ref, n_diag_groups, fillers=None):
        n_groups = i_ref.shape[0] // depth
        group = lambda g: [(i_ref[g * depth + u], j_ref[g * depth + u]) for u in range(depth)]

        def iteration(m, parity, static):
            stages = []
            if not static or m < n_groups:
                stages.append(lambda: stage_scores(group(m), z_ref.at[parity]))
            if not static or 0 <= m - 2 < n_groups:
                stages.append(lambda: stage_values(group(m - 2), static and m - 2 < n_diag_groups))
            if not static or 0 <= m - 1 < n_groups:
                stages.append(lambda: stage_suffix_sums(group(m - 1), z_ref.at[1 - parity],
                                                        static and m - 1 < n_diag_groups))
            return stages

        if fillers is not None:
            stages = [s for m in range(n_groups + 2) for s in iteration(m, m % 2, True)]
            assert len(fillers) <= len(stages)
            slot = {(n * len(stages)) // len(fillers): f for n, f in enumerate(fillers)}
            for n, stage in enumerate(stages):
                if n in slot:
                    slot[n]()
                stage()
            return

        n_static = n_diag_groups + 2
        assert n_static % 2 == 0 and n_groups % 2 == 0 and n_groups >= n_static
        for m in range(n_static):
            for stage in iteration(m, m % 2, True):
                stage()

        def body(mm, carry):
            for stage in iteration(2 * mm, 0, False) + iteration(2 * mm + 1, 1, False):
                stage()
            return carry

        lax.fori_loop(n_static // 2, n_groups // 2, body, 0)
        for m in (n_groups, n_groups + 1):
            for stage in iteration(m, m % 2, True):
                stage()

    sweep(ni_ref, nj_ref, pl.cdiv(n_blocks, depth), fill_near)

    far_blocks = range(ATT_NEAR_DIAGONALS, n_blocks)
    r_max = functools.reduce(jnp.maximum, [r_ref[i] for i in far_blocks])

    @pl.when(jnp.max(r_max) >= ATT_UNDERFLOW_LOG2)
    def _():
        sweep(fi_ref, fj_ref, 0)

    for run in fill_final:
        run()
    gain = gain_ref[...]
    for i in range(n_blocks):
        o = acc_ref[i]
        o2 = o * o
        ss_a = jnp.sum(jnp.where(first_v, o2, 0.0), axis=-1, keepdims=True)
        ss_b = jnp.sum(jnp.where(first_v, 0.0, o2), axis=-1, keepdims=True)
        ms = jnp.where(first_v, ss_a, ss_b) * (1.0 / HEAD_DIM)
        o_ref[i * t:(i + 1) * t, :] = (o * lax.rsqrt(ms + EPS) * gain).astype(BF16)


def _rms(x, g):
    ms = jnp.mean(x * x, axis=-1, keepdims=True)
    return x * lax.rsqrt(ms + EPS) * g


def _ffn_pieces(x_ref, yssd_ref, ysb_ref, row0, wo_ref, g_post_ref, g_pre_ref, wg_ref, wu_ref,
                wd_ref, g_out_ref, o_ref, x1_ref, h_ref, act_ref):
    chunks = [slice(c * FF_CHUNK, (c + 1) * FF_CHUNK) for c in range(D_FF // FF_CHUNK)]
    subs = range(0, TM_FFN, SUB_FFN)

    def mix(s):
        def run():
            rows = pl.ds(s, SUB_FFN)
            seq_rows = pl.ds(pl.multiple_of(row0 + s, SUB_FFN), SUB_FFN)
            y_sb = jnp.concatenate([ysb_ref[pair, seq_rows, :] for pair in range(N_PAIRS)], axis=1)
            m = (_dot(yssd_ref[rows, :], wo_ref[0:D_HEADS, :])
                 + _dot(y_sb, wo_ref[D_HEADS:2 * D_HEADS, :]))
            x1 = x_ref[rows, :] + _rms(m, g_post_ref[...])
            x1_ref[rows, :] = x1
            h_ref[rows, :] = _rms(x1, g_pre_ref[...]).astype(BF16)
        return run

    def gate_up(s, cols):
        def run():
            rows = pl.ds(s, SUB_FFN)
            h = h_ref[rows, :]
            act_ref[rows, cols] = (_silu(_dot(h, wg_ref[:, cols]))
                                   * _dot(h, wu_ref[:, cols])).astype(BF16)
        return run

    def down(s):
        def run():
            rows = pl.ds(s, SUB_FFN)
            f = _dot(act_ref[rows, :], wd_ref[...])
            o_ref[rows, :] = x1_ref[rows, :] + _rms(f, g_out_ref[...])
        return run

    return ([mix(s) for s in subs], [gate_up(s, cols) for s in subs for cols in chunks],
            [down(s) for s in subs])


def _attn_ffn_kernel(ni_ref, nj_ref, fi_ref, fj_ref, q_ref, k_ref, v_ref, w2_ref, sb_gain_ref,
                     x_ref, yssd_ref, wo_ref, g_post_ref, g_pre_ref, wg_ref, wu_ref, wd_ref,
                     g_out_ref, o_ref, q2_ref, kst_ref, vst_ref, r_ref, acc_ref, z_ref, arg_ref,
                     ysb_ref,
                     x1_ref, h_ref, act_ref):
    b = pl.program_id(0)
    p = pl.program_id(1)
    slot = lax.rem(b, 2)

    def attention(fillers):
        _attention_block(ni_ref, nj_ref, fi_ref, fj_ref, q_ref, k_ref, v_ref, w2_ref,
                         sb_gain_ref, ysb_ref.at[slot, p], q2_ref, kst_ref, vst_ref, r_ref, acc_ref,
                         z_ref, arg_ref, fillers)

    @pl.when(b == 0)
    def _():
        attention(None)

    @pl.when(b > 0)
    def _():
        mix, gate_up, down = _ffn_pieces(
            x_ref, yssd_ref, ysb_ref.at[1 - slot], p * TM_FFN, wo_ref, g_post_ref, g_pre_ref,
            wg_ref, wu_ref, wd_ref, g_out_ref, o_ref, x1_ref, h_ref, act_ref)
        attention((mix, gate_up + down[:-1], down[-1:]))


def _attn_ffn(q, k, v, w2, sb_gain, x2, y_ssd, w_out, g_post, g_pre, wg, wu, wd, g_out,
              batch, seq):
    n_q = seq // T_ATT
    assert n_q % ATT_DEPTH == 0
    assert seq == N_PAIRS * TM_FFN

    def tile_list(diagonals):
        tiles = [(i, i - d) for d in diagonals for i in range(d, n_q)]
        tiles += [(n_q, 0)] * (-len(tiles) % (2 * ATT_DEPTH))
        return (jnp.asarray([i for i, _ in tiles], jnp.int32),
                jnp.asarray([j for _, j in tiles], jnp.int32))

    near = tile_list(range(ATT_NEAR_DIAGONALS))
    far = tile_list(range(ATT_NEAR_DIAGONALS, n_q))
    att_blk = lambda b, p, *_: (jnp.minimum(b, batch - 1), p)
    ffn_blk = lambda b, p, *_: (jnp.where(b == 0, 0, (b - 1) * N_PAIRS + p), 0)
    const = lambda b, p, *_: (0, 0)
    single = pl.Buffered(1)
    return pl.pallas_call(
        _attn_ffn_kernel,
        grid_spec=pltpu.PrefetchScalarGridSpec(
            num_scalar_prefetch=4,
            grid=(batch + 1, N_PAIRS),
            in_specs=[
                pl.BlockSpec((seq, LANES), att_blk),
                pl.BlockSpec((seq, LANES), att_blk),
                pl.BlockSpec((seq, LANES), att_blk),
                pl.BlockSpec((2 * T_ATT, 2 * T_ATT), const),
                pl.BlockSpec((1, LANES), lambda b, p, *_: (0, p)),
                pl.BlockSpec((TM_FFN, D_MODEL), ffn_blk),
                pl.BlockSpec((TM_FFN, D_HEADS), ffn_blk),
                pl.BlockSpec((2 * D_HEADS, D_MODEL), const, pipeline_mode=single),
                pl.BlockSpec((1, D_MODEL), const),
                pl.BlockSpec((1, D_MODEL), const),
                pl.BlockSpec((D_MODEL, D_FF), const, pipeline_mode=single),
                pl.BlockSpec((D_MODEL, D_FF), const, pipeline_mode=single),
                pl.BlockSpec((D_FF, D_MODEL), const, pipeline_mode=single),
                pl.BlockSpec((1, D_MODEL), const),
            ],
            out_specs=pl.BlockSpec((TM_FFN, D_MODEL), ffn_blk),
            scratch_shapes=[
                pltpu.VMEM((n_q + 1, T_ATT, LANES), BF16),
                pltpu.VMEM((n_q, 2 * T_ATT, LANES), BF16),
                pltpu.VMEM((n_q, 2 * T_ATT, LANES), BF16),
                pltpu.VMEM((n_q + 1, T_ATT, 2 * T_ATT), F32),
                pltpu.VMEM((n_q + 1, T_ATT, LANES), F32),
                pltpu.VMEM((2, ATT_DEPTH, T_ATT, 2 * T_ATT), F32),
                pltpu.VMEM((ATT_DEPTH, T_ATT, 2 * T_ATT), F32),
                pltpu.VMEM((2, N_PAIRS, seq, LANES), BF16),
                pltpu.VMEM((TM_FFN, D_MODEL), F32),
                pltpu.VMEM((TM_FFN, D_MODEL), BF16),
                pltpu.VMEM((TM_FFN, D_FF), BF16),
            ],
        ),
        out_shape=jax.ShapeDtypeStruct((batch * seq, D_MODEL), F32),
        compiler_params=pltpu.CompilerParams(
            dimension_semantics=("arbitrary", "arbitrary"),
            vmem_limit_bytes=VMEM_LIMIT),
        name="attn_ffn",
    )(*near, *far, q, k, v, w2, sb_gain, x2, y_ssd, w_out, g_post, g_pre, wg, wu, wd, g_out)


def _expand_heads(v):
    return jnp.repeat(v.astype(F32), HEAD_DIM)[None, :]


def _pad_lanes(v):
    return jnp.pad(v.astype(F32), (0, LANES - v.shape[0]))[None, :]


def _layer(x2, batch, seq, pre_mix_gain, w_in, conv_w, conv_b, dt_bias, a_log, d_skip,
           ssd_norm_gain, sb_norm_gain, w_out, post_mix_gain, pre_ffn_gain, w_gate, w_up,
           w_down, post_ffn_gain):
    o_xbc, o_dt = D_HEADS, D_HEADS + D_CONV
    o_q = o_dt + N_HEADS
    scale = 1.0 / math.sqrt(HEAD_DIM)
    w_all = jnp.concatenate([
        w_in[:, 0:o_dt],
        jnp.pad(w_in[:, o_dt:o_q], ((0, 0), (0, LANES - N_HEADS))),
        w_in[:, o_q:o_q + D_HEADS] * scale,
        w_in[:, o_q + D_HEADS:],
    ], axis=1).astype(BF16)

    utri = jnp.triu(jnp.ones((T_SSD, T_SSD), BF16))
    a_row = _pad_lanes(-jnp.exp(a_log.astype(F32)))
    q, k, v, y_ssd = _proj_ssd(x2, pre_mix_gain[None, :], w_all, _pad_lanes(dt_bias), conv_w,
                               conv_b[None, :], a_row, _expand_heads(d_skip),
                               ssd_norm_gain[None, :], utri, seq)

    jj = jnp.arange(T_ATT)
    later = (jj[:, None] >= jj[None, :]).astype(BF16)
    half = jnp.concatenate([later, jnp.ones((T_ATT, T_ATT), BF16)], axis=1)
    w2 = jnp.concatenate([half, half], axis=0)
    return _attn_ffn(q, k, v, w2, sb_norm_gain[None, :], x2, y_ssd, w_out.astype(BF16),
                     post_mix_gain[None, :], pre_ffn_gain[None, :], w_gate.astype(BF16),
                     w_up.astype(BF16), w_down.astype(BF16), post_ffn_gain[None, :], batch, seq)


def kernel(x, pre_mix_gain, w_in, conv_w, conv_b, dt_bias, a_log, d_skip, ssd_norm_gain,
           sb_norm_gain, w_out, post_mix_gain, pre_ffn_gain, w_gate, w_up, w_down,
           post_ffn_gain):
    batch, seq, d = x.shape
    x2 = x.reshape(batch * seq, d)
    params = (pre_mix_gain, w_in, conv_w, conv_b, dt_bias, a_log, d_skip, ssd_norm_gain,
              sb_norm_gain, w_out, post_mix_gain, pre_ffn_gain, w_gate, w_up, w_down,
              post_ffn_gain)
    for layer in range(pre_mix_gain.shape[0]):
        x2 = _layer(x2, batch, seq, *(p[layer] for p in params))
    return x2.reshape(batch, seq, d)
```

```python
import functools
import itertools
import math

import jax
import jax.numpy as jnp
from jax import lax
from jax.experimental import pallas as pl
from jax.experimental.pallas import tpu as pltpu

F32 = jnp.float32
BF16 = jnp.bfloat16

EPS = 1e-6
LANES = 128

D_MODEL = 1024
N_HEADS = 8
HEAD_DIM = 64
D_HEADS = N_HEADS * HEAD_DIM
N_PAIRS = N_HEADS // 2
SSD_GROUPS = 2
SSD_STATE = 128
CONV_WIDTH = 4
D_CONV = D_HEADS + 2 * SSD_GROUPS * SSD_STATE
D_FF = 2816

C_Z = 0
C_XBC = C_Z + D_HEADS
C_DT = C_XBC + D_CONV
C_Q = C_DT + LANES
C_K = C_Q + D_HEADS
C_V = C_K + D_HEADS
C_END = C_V + D_HEADS

TM_PROJ = 512
PROJ_PIECE = 256
T_SSD = 128
SSD_SEGMENTS = 2 + N_PAIRS
T_ATT = 128
TM_FFN = 512
SUB_FFN = 256
FF_CHUNK = 256
CONV_TAIL = 8

VMEM_LIMIT = 56 * 1024 * 1024


def _dot(a, b):
    return jnp.dot(a, b, preferred_element_type=F32)


def _dot_nt(a, b):
    return lax.dot_general(a, b, (((1,), (1,)), ((), ())), preferred_element_type=F32)


def _dot_tn(a, b):
    return lax.dot_general(a, b, (((0,), (0,)), ((), ())), preferred_element_type=F32)


def _split3(x):
    hi = x.astype(BF16)
    r = x - hi.astype(F32)
    mid = r.astype(BF16)
    lo = (r - mid.astype(F32)).astype(BF16)
    return hi, mid, lo


def _silu(x):
    return x / (1.0 + jnp.exp(-x))


def _softplus(x):
    e = jnp.exp(-jnp.abs(x))
    u = 1.0 + e
    tiny = u == 1.0
    log1p_e = jnp.where(tiny, e, jnp.log(u) * (e / jnp.where(tiny, 1.0, u - 1.0)))
    return jnp.maximum(x, 0.0) + log1p_e


def _projection_pieces(x_ref, g_ref, w_ref, dtb_ref, h_ref, sz_ref, xbc_ref, dt_ref, q_ref,
                       k_ref, v_ref):
    x = x_ref[...]
    ms = jnp.mean(x * x, axis=-1, keepdims=True)
    h_ref[...] = (x * lax.rsqrt(ms + EPS) * g_ref[...]).astype(BF16)
    to_bf16 = lambda y: y.astype(BF16)
    segments = [(sz_ref, C_Z, C_XBC, _silu), (xbc_ref, C_XBC, C_DT, lambda y: y),
                (dt_ref, C_DT, C_Q, lambda y: _softplus(y + dtb_ref[...])),
                (q_ref, C_Q, C_K, to_bf16), (k_ref, C_K, C_V, to_bf16),
                (v_ref, C_V, C_END, to_bf16)]

    def piece(out_ref, c0, lo, hi, post):
        def run():
            out_ref[:, lo:hi] = post(_dot(h_ref[...], w_ref[:, c0 + lo:c0 + hi]))
        return run

    return [piece(out_ref, c0, lo, min(lo + PROJ_PIECE, c1 - c0), post)
            for out_ref, c0, c1, post in segments for lo in range(0, c1 - c0, PROJ_PIECE)]


def _proj_ssd_kernel(blocks_per_seq, x_ref, g_ref, w_ref, dtb_ref, cw_ref, cb_ref, a_ref,
                     dskip_ref, gain_ref, utri_ref, q_ref, k_ref, v_ref, y_ref,
                     h_ref, sz_ref, xbc_ref, dt_ref, ext_ref, state_ref):
    g = pl.program_id(0)

    @pl.when(g == 0)
    def _():
        sz_ref[1] = jnp.zeros(sz_ref.shape[1:], F32)
        xbc_ref[1] = jnp.zeros(xbc_ref.shape[1:], F32)
        dt_ref[1] = jnp.zeros(dt_ref.shape[1:], F32)

    @pl.when(lax.rem(g + blocks_per_seq - 1, blocks_per_seq) == 0)
    def _():
        ext_ref[0:CONV_TAIL, :] = jnp.zeros((CONV_TAIL, D_CONV), F32)
        state_ref[...] = jnp.zeros(state_ref.shape, F32)

    def step(new, old):
        pieces = _projection_pieces(x_ref, g_ref, w_ref, dtb_ref, h_ref, sz_ref.at[new],
                                    xbc_ref.at[new], dt_ref.at[new], q_ref, k_ref, v_ref)
        chunks = [pl.ds(c * T_SSD, T_SSD) for c in range(TM_PROJ // T_SSD)]
        segments = itertools.chain.from_iterable(
            _ssd_chunk(sz_ref.at[old, rows], xbc_ref.at[old, rows], dt_ref.at[old, rows],
                       cw_ref, cb_ref, a_ref, dskip_ref, gain_ref, utri_ref, y_ref.at[rows],
                       ext_ref, state_ref) for rows in chunks)
        n_segments = len(chunks) * SSD_SEGMENTS
        assert len(pieces) <= n_segments
        slot = {(n * n_segments) // len(pieces): run for n, run in enumerate(pieces)}
        for s in range(n_segments):
            if s in slot:
                slot[s]()
            next(segments)

    @pl.when(lax.rem(g, 2) == 0)
    def _():
        step(0, 1)

    @pl.when(lax.rem(g, 2) == 1)
    def _():
        step(1, 0)


def _proj_ssd(x2, gain, w_all, dtb, conv_w, conv_b, a_row, dskip, ssd_gain, utri, seq):
    m = x2.shape[0]
    n_blocks = m // TM_PROJ
    assert seq % TM_PROJ == 0 and TM_PROJ % T_SSD == 0
    cur = lambda g: (jnp.minimum(g, n_blocks - 1), 0)
    prev = lambda g: (jnp.maximum(g - 1, 0), 0)
    const = lambda g: (0, 0)
    qkv_spec = pl.BlockSpec((TM_PROJ, D_HEADS), cur)
    qkv_shape = jax.ShapeDtypeStruct((m, D_HEADS), BF16)
    return pl.pallas_call(
        functools.partial(_proj_ssd_kernel, seq // TM_PROJ),
        grid=(n_blocks + 1,),
        in_specs=[
            pl.BlockSpec((TM_PROJ, D_MODEL), cur),
            pl.BlockSpec((1, D_MODEL), const),
            pl.BlockSpec((D_MODEL, C_END), const, pipeline_mode=pl.Buffered(1)),
            pl.BlockSpec((1, LANES), const),
            pl.BlockSpec((CONV_WIDTH, D_CONV), const),
            pl.BlockSpec((1, D_CONV), const),
            pl.BlockSpec((1, LANES), const),
            pl.BlockSpec((1, D_HEADS), const),
            pl.BlockSpec((1, D_HEADS), const),
            pl.BlockSpec((T_SSD, T_SSD), const),
        ],
        out_specs=[qkv_spec, qkv_spec, qkv_spec, pl.BlockSpec((TM_PROJ, D_HEADS), prev)],
        out_shape=[qkv_shape, qkv_shape, qkv_shape, qkv_shape],
        scratch_shapes=[
            pltpu.VMEM((TM_PROJ, D_MODEL), BF16),
            pltpu.VMEM((2, TM_PROJ, D_HEADS), F32),
            pltpu.VMEM((2, TM_PROJ, D_CONV), F32),
            pltpu.VMEM((2, TM_PROJ, LANES), F32),
            pltpu.VMEM((T_SSD + CONV_TAIL, D_CONV), F32),
            pltpu.VMEM((N_PAIRS, SSD_STATE, LANES), F32),
        ],
        compiler_params=pltpu.CompilerParams(
            dimension_semantics=("arbitrary",), vmem_limit_bytes=VMEM_LIMIT),
        name="proj_ssd",
    )(x2, gain, w_all, dtb, conv_w, conv_b, a_row, dskip, ssd_gain, utri)


def _ssd_chunk(sz_ref, xbc_ref, dt_ref, cw_ref, cb_ref, a_ref, dskip_ref, gain_ref, utri_ref,
               y_ref, ext_ref, state_ref):
    t = T_SSD

    ext_ref[CONV_TAIL:CONV_TAIL + t, :] = xbc_ref[...]
    conv = cb_ref[...]
    for k in range(CONV_WIDTH):
        off = CONV_TAIL - (CONV_WIDTH - 1) + k
        conv = conv + ext_ref[off:off + t, :] * cw_ref[k:k + 1, :]
    ext_ref[0:CONV_TAIL, :] = ext_ref[t:t + CONV_TAIL, :]
    xa = _silu(conv)
    yield

    dtv = dt_ref[...]
    adt = dtv * a_ref[...]
    utri = utri_ref[...]
    acs_t = sum(_dot(part, utri) for part in _split3(adt.T))
    acs = acs_t.T

    lane = lax.broadcasted_iota(jnp.int32, (t, LANES), 1)
    first_head = lane < HEAD_DIM
    li = lax.broadcasted_iota(jnp.int32, (t, t), 0)
    si = lax.broadcasted_iota(jnp.int32, (t, t), 1)
    causal = li >= si

    def col(v, h):
        return jnp.broadcast_to(v[:, h:h + 1], (t, LANES))

    y_blocks = []
    cb_mats = []
    for g in range(SSD_GROUPS):
        bm = xa[:, D_HEADS + g * SSD_STATE:D_HEADS + (g + 1) * SSD_STATE].astype(BF16)
        cm = xa[:, D_HEADS + (SSD_GROUPS + g) * SSD_STATE:
                D_HEADS + (SSD_GROUPS + g + 1) * SSD_STATE].astype(BF16)
        cb_mats.append((bm, cm, _dot_nt(cm, bm)))
    yield

    for p in range(N_PAIRS):
        ha, hb = 2 * p, 2 * p + 1
        bm, cm, cbm = cb_mats[p // (N_PAIRS // SSD_GROUPS)]
        x2 = xa[:, p * LANES:(p + 1) * LANES]
        dt2 = jnp.where(first_head, col(dtv, ha), col(dtv, hb))
        acs2 = jnp.where(first_head, col(acs, ha), col(acs, hb))
        xdt2 = x2 * dt2

        def decay(h):
            seg = col(acs, h) - jnp.broadcast_to(acs_t[h:h + 1, :], (t, t))
            return (cbm * jnp.exp(jnp.where(causal, seg, -jnp.inf))).astype(BF16)

        m2 = jnp.concatenate([decay(ha), decay(hb)], axis=1)
        xdt_a = jnp.where(first_head, xdt2, 0.0).astype(BF16)
        xdt_b = jnp.where(first_head, 0.0, xdt2).astype(BF16)
        y_diag = _dot(m2, jnp.concatenate([xdt_a, xdt_b], axis=0))

        prev = state_ref[p]
        y_off = _dot(cm, prev.astype(BF16)) * jnp.exp(acs2)
        last = acs2[t - 1:t, :]
        xs = (xdt2 * jnp.exp(last - acs2)).astype(BF16)
        state_ref[p] = prev * jnp.exp(last) + _dot_tn(bm, xs)

        y_blocks.append(y_diag + y_off + dskip_ref[:, p * LANES:(p + 1) * LANES] * x2)
        if p + 1 < N_PAIRS:
            yield

    per_group = N_PAIRS // SSD_GROUPS
    for g in range(SSD_GROUPS):
        ys = []
        for p in range(g * per_group, (g + 1) * per_group):
            ys.append(y_blocks[p] * sz_ref[:, p * LANES:(p + 1) * LANES])
        ss = sum(jnp.sum(y * y, axis=-1, keepdims=True) for y in ys)
        inv = lax.rsqrt(ss * (1.0 / (per_group * LANES)) + EPS)
        for j, y in enumerate(ys):
            p = g * per_group + j
            y_ref[:, p * LANES:(p + 1) * LANES] = (
                y * inv * gain_ref[:, p * LANES:(p + 1) * LANES]).astype(BF16)
    yield


LOG2E = 1.4426950408889634
ATT_DEPTH = 4
ATT_NEAR_DIAGONALS = 3
ATT_UNDERFLOW_LOG2 = -160.0


def _attn_kernel(ni_ref, nj_ref, fi_ref, fj_ref, q_ref, k_ref, v_ref, w2_ref, gain_ref, o_ref,
                 q2_ref, kst_ref, vst_ref, r_ref, acc_ref, z_ref, arg_ref):
    t = T_ATT
    n_blocks = q2_ref.shape[0] - 1
    depth = ATT_DEPTH

    lane_v = lax.broadcasted_iota(jnp.int32, (t, LANES), 1)
    first_v = lane_v < HEAD_DIM

    for j in range(n_blocks):
        kb = k_ref[j * t:(j + 1) * t, :]
        vb = v_ref[j * t:(j + 1) * t, :]
        zero = jnp.zeros_like(kb)
        q2_ref[j] = q_ref[j * t:(j + 1) * t, :]
        kst_ref[j, 0:t, :] = jnp.where(first_v, kb, zero)
        kst_ref[j, t:2 * t, :] = jnp.where(first_v, zero, kb)
        vst_ref[j, 0:t, :] = jnp.where(first_v, vb, zero)
        vst_ref[j, t:2 * t, :] = jnp.where(first_v, zero, vb)
    q2_ref[n_blocks] = jnp.zeros(q2_ref.shape[1:], BF16)
    r_ref[n_blocks] = jnp.zeros(r_ref.shape[1:], F32)
    acc_ref[n_blocks] = jnp.zeros(acc_ref.shape[1:], F32)

    ti = lax.broadcasted_iota(jnp.int32, (t, 2 * t), 0)
    si = lax.broadcasted_iota(jnp.int32, (t, 2 * t), 1)
    strict = jnp.where(si >= t, si - t, si) < ti

    def stage_scores(tiles, z_buf):
        for u, (i, j) in enumerate(tiles):
            z_buf[u] = _dot_nt(q2_ref[i], kst_ref[j])

    def stage_suffix_sums(tiles, z_buf, diag):
        for u, (i, _) in enumerate(tiles):
            zn = z_buf[u] * (-LOG2E)
            l1 = jnp.minimum(zn, 0.0) - jnp.log2(1.0 + jnp.exp2(-jnp.abs(zn)))
            if diag:
                l1 = jnp.where(strict, l1, 0.0)
            hi = l1.astype(BF16)
            lo = (l1 - hi.astype(F32)).astype(BF16)
            res = [_dot(jnp.concatenate([hi[:, h * t:(h + 1) * t], lo[:, h * t:(h + 1) * t]],
                                        axis=1), w2_ref[...]) for h in range(2)]
            arg = jnp.concatenate([r[:, 0:t] for r in res], axis=1) - zn
            tot = jnp.concatenate([r[:, t:2 * t] for r in res], axis=1)
            if diag:
                arg = jnp.where(strict, arg, -jnp.inf)
            else:
                r_old = r_ref[i]
                arg = arg + r_old
                tot = tot + r_old
            arg_ref[u] = arg
            r_ref[i] = tot

    def stage_values(tiles, diag):
        for u, (i, j) in enumerate(tiles):
            contrib = _dot(jnp.exp2(arg_ref[u]).astype(BF16), vst_ref[j])
            if diag:
                acc_ref[i] = contrib
            else:
                acc_ref[i] += contrib

    def sweep(i_ref, j_ref, n_diag_groups):
        n_groups = i_ref.shape[0] // depth
        group = lambda g: [(i_ref[g * depth + u], j_ref[g * depth + u]) for u in range(depth)]
        n_static = n_diag_groups + 2
        assert n_static % 2 == 0 and n_groups % 2 == 0 and n_groups >= n_static

        def iteration(m, parity, static):
            if not static or m < n_groups:
                stage_scores(group(m), z_ref.at[parity])
            if not static or 0 <= m - 2 < n_groups:
                stage_values(group(m - 2), static and m - 2 < n_diag_groups)
            if not static or 0 <= m - 1 < n_groups:
                stage_suffix_sums(group(m - 1), z_ref.at[1 - parity],
                                  static and m - 1 < n_diag_groups)

        for m in range(n_static):
            iteration(m, m % 2, True)

        def body(mm, carry):
            iteration(2 * mm, 0, False)
            iteration(2 * mm + 1, 1, False)
            return carry

        lax.fori_loop(n_static // 2, n_groups // 2, body, 0)
        for m in (n_groups, n_groups + 1):
            iteration(m, m % 2, True)

    sweep(ni_ref, nj_ref, pl.cdiv(n_blocks, depth))

    far_blocks = range(ATT_NEAR_DIAGONALS, n_blocks)
    r_max = functools.reduce(jnp.maximum, [r_ref[i] for i in far_blocks])

    @pl.when(jnp.max(r_max) >= ATT_UNDERFLOW_LOG2)
    def _():
        sweep(fi_ref, fj_ref, 0)

    gain = gain_ref[...]
    for i in range(n_blocks):
        o = acc_ref[i]
        o2 = o * o
        ss_a = jnp.sum(jnp.where(first_v, o2, 0.0), axis=-1, keepdims=True)
        ss_b = jnp.sum(jnp.where(first_v, 0.0, o2), axis=-1, keepdims=True)
        ms = jnp.where(first_v, ss_a, ss_b) * (1.0 / HEAD_DIM)
        o_ref[i * t:(i + 1) * t, :] = (o * lax.rsqrt(ms + EPS) * gain).astype(BF16)


def _rms(x, g):
    ms = jnp.mean(x * x, axis=-1, keepdims=True)
    return x * lax.rsqrt(ms + EPS) * g


def _attention(q, k, v, w2, gain, batch, seq):
    n_q = seq // T_ATT
    assert n_q % ATT_DEPTH == 0

    def tile_list(diagonals):
        tiles = [(i, i - d) for d in diagonals for i in range(d, n_q)]
        tiles += [(n_q, 0)] * (-len(tiles) % (2 * ATT_DEPTH))
        return (jnp.asarray([i for i, _ in tiles], jnp.int32),
                jnp.asarray([j for _, j in tiles], jnp.int32))

    near = tile_list(range(ATT_NEAR_DIAGONALS))
    far = tile_list(range(ATT_NEAR_DIAGONALS, n_q))
    seq_blk = lambda b, p, *_: (b, p)
    return pl.pallas_call(
        _attn_kernel,
        grid_spec=pltpu.PrefetchScalarGridSpec(
            num_scalar_prefetch=4,
            grid=(batch, N_PAIRS),
            in_specs=[
                pl.BlockSpec((seq, LANES), seq_blk),
                pl.BlockSpec((seq, LANES), seq_blk),
                pl.BlockSpec((seq, LANES), seq_blk),
                pl.BlockSpec((2 * T_ATT, 2 * T_ATT), lambda b, p, *_: (0, 0)),
                pl.BlockSpec((1, LANES), lambda b, p, *_: (0, p)),
            ],
            out_specs=pl.BlockSpec((seq, LANES), seq_blk),
            scratch_shapes=[
                pltpu.VMEM((n_q + 1, T_ATT, LANES), BF16),
                pltpu.VMEM((n_q, 2 * T_ATT, LANES), BF16),
                pltpu.VMEM((n_q, 2 * T_ATT, LANES), BF16),
                pltpu.VMEM((n_q + 1, T_ATT, 2 * T_ATT), F32),
                pltpu.VMEM((n_q + 1, T_ATT, LANES), F32),
                pltpu.VMEM((2, ATT_DEPTH, T_ATT, 2 * T_ATT), F32),
                pltpu.VMEM((ATT_DEPTH, T_ATT, 2 * T_ATT), F32),
            ],
        ),
        out_shape=jax.ShapeDtypeStruct((batch * seq, D_HEADS), BF16),
        compiler_params=pltpu.CompilerParams(
            dimension_semantics=("arbitrary", "arbitrary"),
            vmem_limit_bytes=VMEM_LIMIT),
        name="sb_attention",
    )(*near, *far, q, k, v, w2, gain)


def _ffn_kernel(x_ref, yssd_ref, ysb_ref, wo_ref, g_post_ref, g_pre_ref, wg_ref, wu_ref,
                wd_ref, g_out_ref, o_ref, x1_ref, h_ref, act_ref):
    chunks = [slice(c * FF_CHUNK, (c + 1) * FF_CHUNK) for c in range(D_FF // FF_CHUNK)]
    subs = [pl.ds(s * SUB_FFN, SUB_FFN) for s in range(TM_FFN // SUB_FFN)]
    for rows in subs:
        mix = (_dot(yssd_ref[rows, :], wo_ref[0:D_HEADS, :])
               + _dot(ysb_ref[rows, :], wo_ref[D_HEADS:2 * D_HEADS, :]))
        x1 = x_ref[rows, :] + _rms(mix, g_post_ref[...])
        x1_ref[rows, :] = x1
        h_ref[rows, :] = _rms(x1, g_pre_ref[...]).astype(BF16)
    for rows in subs:
        for cols in chunks:
            h = h_ref[rows, :]
            gate = _dot(h, wg_ref[:, cols])
            up = _dot(h, wu_ref[:, cols])
            act_ref[rows, cols] = (_silu(gate) * up).astype(BF16)
    for rows in subs:
        f = _dot(act_ref[rows, :], wd_ref[...])
        o_ref[rows, :] = x1_ref[rows, :] + _rms(f, g_out_ref[...])


def _out_ffn(x2, y_ssd, y_sb, w_out, g_post, g_pre, wg, wu, wd, g_out):
    m = x2.shape[0]
    row = lambda i: (i, 0)
    const2 = lambda i: (0, 0)
    single = pl.Buffered(1)
    return pl.pallas_call(
        _ffn_kernel,
        grid=(m // TM_FFN,),
        in_specs=[
            pl.BlockSpec((TM_FFN, D_MODEL), row),
            pl.BlockSpec((TM_FFN, D_HEADS), row),
            pl.BlockSpec((TM_FFN, D_HEADS), row),
            pl.BlockSpec((2 * D_HEADS, D_MODEL), const2, pipeline_mode=single),
            pl.BlockSpec((1, D_MODEL), const2),
            pl.BlockSpec((1, D_MODEL), const2),
            pl.BlockSpec((D_MODEL, D_FF), const2, pipeline_mode=single),
            pl.BlockSpec((D_MODEL, D_FF), const2, pipeline_mode=single),
            pl.BlockSpec((D_FF, D_MODEL), const2, pipeline_mode=single),
            pl.BlockSpec((1, D_MODEL), const2),
        ],
        out_specs=pl.BlockSpec((TM_FFN, D_MODEL), row),
        out_shape=jax.ShapeDtypeStruct((m, D_MODEL), F32),
        scratch_shapes=[
            pltpu.VMEM((TM_FFN, D_MODEL), F32),
            pltpu.VMEM((TM_FFN, D_MODEL), BF16),
            pltpu.VMEM((TM_FFN, D_FF), BF16),
        ],
        compiler_params=pltpu.CompilerParams(
            dimension_semantics=("arbitrary",), vmem_limit_bytes=VMEM_LIMIT),
        name="out_ffn",
    )(x2, y_ssd, y_sb, w_out, g_post, g_pre, wg, wu, wd, g_out)


def _expand_heads(v):
    return jnp.repeat(v.astype(F32), HEAD_DIM)[None, :]


def _pad_lanes(v):
    return jnp.pad(v.astype(F32), (0, LANES - v.shape[0]))[None, :]


def _layer(x2, batch, seq, pre_mix_gain, w_in, conv_w, conv_b, dt_bias, a_log, d_skip,
           ssd_norm_gain, sb_norm_gain, w_out, post_mix_gain, pre_ffn_gain, w_gate, w_up,
           w_down, post_ffn_gain):
    o_xbc, o_dt = D_HEADS, D_HEADS + D_CONV
    o_q = o_dt + N_HEADS
    scale = 1.0 / math.sqrt(HEAD_DIM)
    w_all = jnp.concatenate([
        w_in[:, 0:o_dt],
        jnp.pad(w_in[:, o_dt:o_q], ((0, 0), (0, LANES - N_HEADS))),
        w_in[:, o_q:o_q + D_HEADS] * scale,
        w_in[:, o_q + D_HEADS:],
    ], axis=1).astype(BF16)

    utri = jnp.triu(jnp.ones((T_SSD, T_SSD), BF16))
    a_row = _pad_lanes(-jnp.exp(a_log.astype(F32)))
    q, k, v, y_ssd = _proj_ssd(x2, pre_mix_gain[None, :], w_all, _pad_lanes(dt_bias), conv_w,
                               conv_b[None, :], a_row, _expand_heads(d_skip),
                               ssd_norm_gain[None, :], utri, seq)

    jj = jnp.arange(T_ATT)
    later = (jj[:, None] >= jj[None, :]).astype(BF16)
    half = jnp.concatenate([later, jnp.ones((T_ATT, T_ATT), BF16)], axis=1)
    w2 = jnp.concatenate([half, half], axis=0)
    y_sb = _attention(q, k, v, w2, sb_norm_gain[None, :], batch, seq)

    return _out_ffn(x2, y_ssd, y_sb, w_out.astype(BF16), post_mix_gain[None, :],
                    pre_ffn_gain[None, :], w_gate.astype(BF16), w_up.astype(BF16),
                    w_down.astype(BF16), post_ffn_gain[None, :])


def kernel(x, pre_mix_gain, w_in, conv_w, conv_b, dt_bias, a_log, d_skip, ssd_norm_gain,
           sb_norm_gain, w_out, post_mix_gain, pre_ffn_gain, w_gate, w_up, w_down,
           post_ffn_gain):
    batch, seq, d = x.shape
    x2 = x.reshape(batch * seq, d)
    params = (pre_mix_gain, w_in, conv_w, conv_b, dt_bias, a_log, d_skip, ssd_norm_gain,
              sb_norm_gain, w_out, post_mix_gain, pre_ffn_gain, w_gate, w_up, w_down,
              post_ffn_gain)
    for layer in range(pre_mix_gain.shape[0]):
        x2 = _layer(x2, batch, seq, *(p[layer] for p in params))
    return x2.reshape(batch, seq, d)
```

```python
import functools
import itertools
import math

import jax
import jax.numpy as jnp
from jax import lax
from jax.experimental import pallas as pl
from jax.experimental.pallas import tpu as pltpu

F32 = jnp.float32
BF16 = jnp.bfloat16

EPS = 1e-6
LANES = 128

D_MODEL = 1024
N_HEADS = 8
HEAD_DIM = 64
D_HEADS = N_HEADS * HEAD_DIM
N_PAIRS = N_HEADS // 2
SSD_GROUPS = 2
SSD_STATE = 128
CONV_WIDTH = 4
D_CONV = D_HEADS + 2 * SSD_GROUPS * SSD_STATE
D_FF = 2816

C_Z = 0
C_XBC = C_Z + D_HEADS
C_DT = C_XBC + D_CONV
C_Q = C_DT + LANES
C_K = C_Q + D_HEADS
C_V = C_K + D_HEADS
C_END = C_V + D_HEADS

TM_PROJ = 512
PROJ_PIECE = 256
T_SSD = 128
SSD_SEGMENTS = 2 + N_PAIRS
T_ATT = 128
TM_FFN = 512
SUB_FFN = 256
FF_CHUNK = 256
CONV_TAIL = 8

VMEM_LIMIT = 56 * 1024 * 1024


def _dot(a, b):
    return jnp.dot(a, b, preferred_element_type=F32)


def _dot_nt(a, b):
    return lax.dot_general(a, b, (((1,), (1,)), ((), ())), preferred_element_type=F32)


def _dot_tn(a, b):
    return lax.dot_general(a, b, (((0,), (0,)), ((), ())), preferred_element_type=F32)


def _split3(x):
    hi = x.astype(BF16)
    r = x - hi.astype(F32)
    mid = r.astype(BF16)
    lo = (r - mid.astype(F32)).astype(BF16)
    return hi, mid, lo


def _silu(x):
    return x / (1.0 + jnp.exp(-x))


def _softplus(x):
    e = jnp.exp(-jnp.abs(x))
    u = 1.0 + e
    tiny = u == 1.0
    log1p_e = jnp.where(tiny, e, jnp.log(u) * (e / jnp.where(tiny, 1.0, u - 1.0)))
    return jnp.maximum(x, 0.0) + log1p_e


def _projection_pieces(x_ref, g_ref, w_ref, dtb_ref, h_ref, sz_ref, xbc_ref, dt_ref, q_ref,
                       k_ref, v_ref):
    x = x_ref[...]
    ms = jnp.mean(x * x, axis=-1, keepdims=True)
    h_ref[...] = (x * lax.rsqrt(ms + EPS) * g_ref[...]).astype(BF16)
    to_bf16 = lambda y: y.astype(BF16)
    segments = [(sz_ref, C_Z, C_XBC, _silu), (xbc_ref, C_XBC, C_DT, lambda y: y),
                (dt_ref, C_DT, C_Q, lambda y: _softplus(y + dtb_ref[...])),
                (q_ref, C_Q, C_K, to_bf16), (k_ref, C_K, C_V, to_bf16),
                (v_ref, C_V, C_END, to_bf16)]

    def piece(out_ref, c0, lo, hi, post):
        def run():
            out_ref[:, lo:hi] = post(_dot(h_ref[...], w_ref[:, c0 + lo:c0 + hi]))
        return run

    return [piece(out_ref, c0, lo, min(lo + PROJ_PIECE, c1 - c0), post)
            for out_ref, c0, c1, post in segments for lo in range(0, c1 - c0, PROJ_PIECE)]


def _prepare_weight(w_in_ref, w_ref):
    o_dt = D_HEADS + D_CONV
    o_q = o_dt + N_HEADS
    scale = 1.0 / math.sqrt(HEAD_DIM)
    rows_per_step = LANES
    lane = lax.broadcasted_iota(jnp.int32, (rows_per_step, LANES), 1)
    for r in range(0, D_MODEL, rows_per_step):
        rows = pl.ds(r, rows_per_step)
        w_ref[rows, C_Z:C_DT] = w_in_ref[rows, 0:o_dt].astype(BF16)
        dt_block = w_in_ref[rows, o_dt:o_dt + LANES]
        w_ref[rows, C_DT:C_Q] = jnp.where(lane < N_HEADS, dt_block, 0.0).astype(BF16)
        w_ref[rows, C_Q:C_K] = (w_in_ref[rows, o_q:o_q + D_HEADS] * scale).astype(BF16)
        w_ref[rows, C_K:C_END] = w_in_ref[rows, o_q + D_HEADS:o_q + 3 * D_HEADS].astype(BF16)


def _proj_ssd_kernel(blocks_per_seq, x_ref, g_ref, w_in_ref, dtb_ref, cw_ref, cb_ref, a_ref,
                     dskip_ref, gain_ref, utri_ref, q_ref, k_ref, v_ref, y_ref,
                     w_ref, h_ref, sz_ref, xbc_ref, dt_ref, ext_ref, state_ref):
    g = pl.program_id(0)

    @pl.when(g == 0)
    def _():
        _prepare_weight(w_in_ref, w_ref)
        sz_ref[1] = jnp.zeros(sz_ref.shape[1:], F32)
        xbc_ref[1] = jnp.zeros(xbc_ref.shape[1:], F32)
        dt_ref[1] = jnp.zeros(dt_ref.shape[1:], F32)

    @pl.when(lax.rem(g + blocks_per_seq - 1, blocks_per_seq) == 0)
    def _():
        ext_ref[0:CONV_TAIL, :] = jnp.zeros((CONV_TAIL, D_CONV), F32)
        state_ref[...] = jnp.zeros(state_ref.shape, F32)

    def step(new, old):
        pieces = _projection_pieces(x_ref, g_ref, w_ref, dtb_ref, h_ref, sz_ref.at[new],
                                    xbc_ref.at[new], dt_ref.at[new], q_ref, k_ref, v_ref)
        chunks = [pl.ds(c * T_SSD, T_SSD) for c in range(TM_PROJ // T_SSD)]
        segments = itertools.chain.from_iterable(
            _ssd_chunk(sz_ref.at[old, rows], xbc_ref.at[old, rows], dt_ref.at[old, rows],
                       cw_ref, cb_ref, a_ref, dskip_ref, gain_ref, utri_ref, y_ref.at[rows],
                       ext_ref, state_ref) for rows in chunks)
        n_segments = len(chunks) * SSD_SEGMENTS
        assert len(pieces) <= n_segments
        slot = {(n * n_segments) // len(pieces): run for n, run in enumerate(pieces)}
        for s in range(n_segments):
            if s in slot:
                slot[s]()
            next(segments)

    @pl.when(lax.rem(g, 2) == 0)
    def _():
        step(0, 1)

    @pl.when(lax.rem(g, 2) == 1)
    def _():
        step(1, 0)


def _proj_ssd(x2, gain, w_in, layer, dtb, conv_w, conv_b, a_row, dskip, ssd_gain, utri, seq):
    m = x2.shape[0]
    n_blocks = m // TM_PROJ
    assert seq % TM_PROJ == 0 and TM_PROJ % T_SSD == 0
    assert w_in.shape[1:] == (D_MODEL, C_END - (LANES - N_HEADS))
    cur = lambda g: (jnp.minimum(g, n_blocks - 1), 0)
    prev = lambda g: (jnp.maximum(g - 1, 0), 0)
    const = lambda g: (0, 0)
    qkv_spec = pl.BlockSpec((TM_PROJ, D_HEADS), cur)
    qkv_shape = jax.ShapeDtypeStruct((m, D_HEADS), BF16)
    return pl.pallas_call(
        functools.partial(_proj_ssd_kernel, seq // TM_PROJ),
        grid=(n_blocks + 1,),
        in_specs=[
            pl.BlockSpec((TM_PROJ, D_MODEL), cur),
            pl.BlockSpec((1, D_MODEL), const),
            pl.BlockSpec((None,) + w_in.shape[1:], lambda g: (layer, 0, 0),
                         pipeline_mode=pl.Buffered(1)),
            pl.BlockSpec((1, LANES), const),
            pl.BlockSpec((CONV_WIDTH, D_CONV), const),
            pl.BlockSpec((1, D_CONV), const),
            pl.BlockSpec((1, LANES), const),
            pl.BlockSpec((1, D_HEADS), const),
            pl.BlockSpec((1, D_HEADS), const),
            pl.BlockSpec((T_SSD, T_SSD), const),
        ],
        out_specs=[qkv_spec, qkv_spec, qkv_spec, pl.BlockSpec((TM_PROJ, D_HEADS), prev)],
        out_shape=[qkv_shape, qkv_shape, qkv_shape, qkv_shape],
        scratch_shapes=[
            pltpu.VMEM((D_MODEL, C_END), BF16),
            pltpu.VMEM((TM_PROJ, D_MODEL), BF16),
            pltpu.VMEM((2, TM_PROJ, D_HEADS), F32),
            pltpu.VMEM((2, TM_PROJ, D_CONV), F32),
            pltpu.VMEM((2, TM_PROJ, LANES), F32),
            pltpu.VMEM((T_SSD + CONV_TAIL, D_CONV), F32),
            pltpu.VMEM((N_PAIRS, SSD_STATE, LANES), F32),
        ],
        compiler_params=pltpu.CompilerParams(
            dimension_semantics=("arbitrary",), vmem_limit_bytes=VMEM_LIMIT),
        name="proj_ssd",
    )(x2, gain, w_in, dtb, conv_w, conv_b, a_row, dskip, ssd_gain, utri)


def _ssd_chunk(sz_ref, xbc_ref, dt_ref, cw_ref, cb_ref, a_ref, dskip_ref, gain_ref, utri_ref,
               y_ref, ext_ref, state_ref):
    t = T_SSD

    ext_ref[CONV_TAIL:CONV_TAIL + t, :] = xbc_ref[...]
    conv = cb_ref[...]
    for k in range(CONV_WIDTH):
        off = CONV_TAIL - (CONV_WIDTH - 1) + k
        conv = conv + ext_ref[off:off + t, :] * cw_ref[k:k + 1, :]
    ext_ref[0:CONV_TAIL, :] = ext_ref[t:t + CONV_TAIL, :]
    xa = _silu(conv)
    yield

    dtv = dt_ref[...]
    adt = dtv * a_ref[...]
    utri = utri_ref[...]
    acs_t = sum(_dot(part, utri) for part in _split3(adt.T))
    acs = acs_t.T

    lane = lax.broadcasted_iota(jnp.int32, (t, LANES), 1)
    first_head = lane < HEAD_DIM
    li = lax.broadcasted_iota(jnp.int32, (t, t), 0)
    si = lax.broadcasted_iota(jnp.int32, (t, t), 1)
    causal = li >= si

    def col(v, h):
        return jnp.broadcast_to(v[:, h:h + 1], (t, LANES))

    y_blocks = []
    cb_mats = []
    for g in range(SSD_GROUPS):
        bm = xa[:, D_HEADS + g * SSD_STATE:D_HEADS + (g + 1) * SSD_STATE].astype(BF16)
        cm = xa[:, D_HEADS + (SSD_GROUPS + g) * SSD_STATE:
                D_HEADS + (SSD_GROUPS + g + 1) * SSD_STATE].astype(BF16)
        cb_mats.append((bm, cm, _dot_nt(cm, bm)))
    yield

    for p in range(N_PAIRS):
        ha, hb = 2 * p, 2 * p + 1
        bm, cm, cbm = cb_mats[p // (N_PAIRS // SSD_GROUPS)]
        x2 = xa[:, p * LANES:(p + 1) * LANES]
        dt2 = jnp.where(first_head, col(dtv, ha), col(dtv, hb))
        acs2 = jnp.where(first_head, col(acs, ha), col(acs, hb))
        xdt2 = x2 * dt2

        def decay(h):
            seg = col(acs, h) - jnp.broadcast_to(acs_t[h:h + 1, :], (t, t))
            return (cbm * jnp.exp(jnp.where(causal, seg, -jnp.inf))).astype(BF16)

        m2 = jnp.concatenate([decay(ha), decay(hb)], axis=1)
        xdt_a = jnp.where(first_head, xdt2, 0.0).astype(BF16)
        xdt_b = jnp.where(first_head, 0.0, xdt2).astype(BF16)
        y_diag = _dot(m2, jnp.concatenate([xdt_a, xdt_b], axis=0))

        prev = state_ref[p]
        y_off = _dot(cm, prev.astype(BF16)) * jnp.exp(acs2)
        last = acs2[t - 1:t, :]
        xs = (xdt2 * jnp.exp(last - acs2)).astype(BF16)
        state_ref[p] = prev * jnp.exp(last) + _dot_tn(bm, xs)

        y_blocks.append(y_diag + y_off + dskip_ref[:, p * LANES:(p + 1) * LANES] * x2)
        if p + 1 < N_PAIRS:
            yield

    per_group = N_PAIRS // SSD_GROUPS
    for g in range(SSD_GROUPS):
        ys = []
        for p in range(g * per_group, (g + 1) * per_group):
            ys.append(y_blocks[p] * sz_ref[:, p * LANES:(p + 1) * LANES])
        ss = sum(jnp.sum(y * y, axis=-1, keepdims=True) for y in ys)
        inv = lax.rsqrt(ss * (1.0 / (per_group * LANES)) + EPS)
        for j, y in enumerate(ys):
            p = g * per_group + j
            y_ref[:, p * LANES:(p + 1) * LANES] = (
                y * inv * gain_ref[:, p * LANES:(p + 1) * LANES]).astype(BF16)
    yield


LOG2E = 1.4426950408889634
ATT_DEPTH = 4
ATT_NEAR_DIAGONALS = 3
ATT_UNDERFLOW_LOG2 = -160.0


def _attn_kernel(ni_ref, nj_ref, fi_ref, fj_ref, q_ref, k_ref, v_ref, w2_ref, gain_ref, o_ref,
                 q2_ref, kst_ref, vst_ref, r_ref, acc_ref, z_ref, arg_ref):
    t = T_ATT
    n_blocks = q2_ref.shape[0] - 1
    depth = ATT_DEPTH

    lane_v = lax.broadcasted_iota(jnp.int32, (t, LANES), 1)
    first_v = lane_v < HEAD_DIM

    for j in range(n_blocks):
        kb = k_ref[j * t:(j + 1) * t, :]
        vb = v_ref[j * t:(j + 1) * t, :]
        zero = jnp.zeros_like(kb)
        q2_ref[j] = q_ref[j * t:(j + 1) * t, :]
        kst_ref[j, 0:t, :] = jnp.where(first_v, kb, zero)
        kst_ref[j, t:2 * t, :] = jnp.where(first_v, zero, kb)
        vst_ref[j, 0:t, :] = jnp.where(first_v, vb, zero)
        vst_ref[j, t:2 * t, :] = jnp.where(first_v, zero, vb)
    q2_ref[n_blocks] = jnp.zeros(q2_ref.shape[1:], BF16)
    r_ref[n_blocks] = jnp.zeros(r_ref.shape[1:], F32)
    acc_ref[n_blocks] = jnp.zeros(acc_ref.shape[1:], F32)

    ti = lax.broadcasted_iota(jnp.int32, (t, 2 * t), 0)
    si = lax.broadcasted_iota(jnp.int32, (t, 2 * t), 1)
    strict = jnp.where(si >= t, si - t, si) < ti

    def stage_scores(tiles, z_buf):
        for u, (i, j) in enumerate(tiles):
            z_buf[u] = _dot_nt(q2_ref[i], kst_ref[j])

    def stage_suffix_sums(tiles, z_buf, diag):
        for u, (i, _) in enumerate(tiles):
            zn = z_buf[u] * (-LOG2E)
            l1 = jnp.minimum(zn, 0.0) - jnp.log2(1.0 + jnp.exp2(-jnp.abs(zn)))
            if diag:
                l1 = jnp.where(strict, l1, 0.0)
            hi = l1.astype(BF16)
            lo = (l1 - hi.astype(F32)).astype(BF16)
            res = [_dot(jnp.concatenate([hi[:, h * t:(h + 1) * t], lo[:, h * t:(h + 1) * t]],
                                        axis=1), w2_ref[...]) for h in range(2)]
            arg = jnp.concatenate([r[:, 0:t] for r in res], axis=1) - zn
            tot = jnp.concatenate([r[:, t:2 * t] for r in res], axis=1)
            if diag:
                arg = jnp.where(strict, arg, -jnp.inf)
            else:
                r_old = r_ref[i]
                arg = arg + r_old
                tot = tot + r_old
            arg_ref[u] = arg
            r_ref[i] = tot

    def stage_values(tiles, diag):
        for u, (i, j) in enumerate(tiles):
            contrib = _dot(jnp.exp2(arg_ref[u]).astype(BF16), vst_ref[j])
            if diag:
                acc_ref[i] = contrib
            else:
                acc_ref[i] += contrib

    def sweep(i_ref, j_ref, n_diag_groups):
        n_groups = i_ref.shape[0] // depth
        group = lambda g: [(i_ref[g * depth + u], j_ref[g * depth + u]) for u in range(depth)]
        n_static = n_diag_groups + 2
        assert n_static % 2 == 0 and n_groups % 2 == 0 and n_groups >= n_static

        def iteration(m, parity, static):
            if not static or m < n_groups:
                stage_scores(group(m), z_ref.at[parity])
            if not static or 0 <= m - 2 < n_groups:
                stage_values(group(m - 2), static and m - 2 < n_diag_groups)
            if not static or 0 <= m - 1 < n_groups:
                stage_suffix_sums(group(m - 1), z_ref.at[1 - parity],
                                  static and m - 1 < n_diag_groups)

        for m in range(n_static):
            iteration(m, m % 2, True)

        def body(mm, carry):
            iteration(2 * mm, 0, False)
            iteration(2 * mm + 1, 1, False)
            return carry

        lax.fori_loop(n_static // 2, n_groups // 2, body, 0)
        for m in (n_groups, n_groups + 1):
            iteration(m, m % 2, True)

    sweep(ni_ref, nj_ref, pl.cdiv(n_blocks, depth))

    far_blocks = range(ATT_NEAR_DIAGONALS, n_blocks)
    r_max = functools.reduce(jnp.maximum, [r_ref[i] for i in far_blocks])

    @pl.when(jnp.max(r_max) >= ATT_UNDERFLOW_LOG2)
    def _():
        sweep(fi_ref, fj_ref, 0)

    gain = gain_ref[...]
    for i in range(n_blocks):
        o = acc_ref[i]
        o2 = o * o
        ss_a = jnp.sum(jnp.where(first_v, o2, 0.0), axis=-1, keepdims=True)
        ss_b = jnp.sum(jnp.where(first_v, 0.0, o2), axis=-1, keepdims=True)
        ms = jnp.where(first_v, ss_a, ss_b) * (1.0 / HEAD_DIM)
        o_ref[i * t:(i + 1) * t, :] = (o * lax.rsqrt(ms + EPS) * gain).astype(BF16)


def _rms(x, g):
    ms = jnp.mean(x * x, axis=-1, keepdims=True)
    return x * lax.rsqrt(ms + EPS) * g


def _attention(q, k, v, w2, gain, batch, seq):
    n_q = seq // T_ATT
    assert n_q % ATT_DEPTH == 0

    def tile_list(diagonals):
        tiles = [(i, i - d) for d in diagonals for i in range(d, n_q)]
        tiles += [(n_q, 0)] * (-len(tiles) % (2 * ATT_DEPTH))
        return (jnp.asarray([i for i, _ in tiles], jnp.int32),
                jnp.asarray([j for _, j in tiles], jnp.int32))

    near = tile_list(range(ATT_NEAR_DIAGONALS))
    far = tile_list(range(ATT_NEAR_DIAGONALS, n_q))
    seq_blk = lambda b, p, *_: (b, p)
    return pl.pallas_call(
        _attn_kernel,
        grid_spec=pltpu.PrefetchScalarGridSpec(
            num_scalar_prefetch=4,
            grid=(batch, N_PAIRS),
            in_specs=[
                pl.BlockSpec((seq, LANES), seq_blk),
                pl.BlockSpec((seq, LANES), seq_blk),
                pl.BlockSpec((seq, LANES), seq_blk),
                pl.BlockSpec((2 * T_ATT, 2 * T_ATT), lambda b, p, *_: (0, 0)),
                pl.BlockSpec((1, LANES), lambda b, p, *_: (0, p)),
            ],
            out_specs=pl.BlockSpec((seq, LANES), seq_blk),
            scratch_shapes=[
                pltpu.VMEM((n_q + 1, T_ATT, LANES), BF16),
                pltpu.VMEM((n_q, 2 * T_ATT, LANES), BF16),
                pltpu.VMEM((n_q, 2 * T_ATT, LANES), BF16),
                pltpu.VMEM((n_q + 1, T_ATT, 2 * T_ATT), F32),
                pltpu.VMEM((n_q + 1, T_ATT, LANES), F32),
                pltpu.VMEM((2, ATT_DEPTH, T_ATT, 2 * T_ATT), F32),
                pltpu.VMEM((ATT_DEPTH, T_ATT, 2 * T_ATT), F32),
            ],
        ),
        out_shape=jax.ShapeDtypeStruct((batch * seq, D_HEADS), BF16),
        compiler_params=pltpu.CompilerParams(
            dimension_semantics=("arbitrary", "arbitrary"),
            vmem_limit_bytes=VMEM_LIMIT),
        name="sb_attention",
    )(*near, *far, q, k, v, w2, gain)


def _ffn_kernel(x_ref, yssd_ref, ysb_ref, wo_ref, g_post_ref, g_pre_ref, wg_ref, wu_ref,
                wd_ref, g_out_ref, o_ref, x1_ref, h_ref, act_ref):
    chunks = [slice(c * FF_CHUNK, (c + 1) * FF_CHUNK) for c in range(D_FF // FF_CHUNK)]
    subs = [pl.ds(s * SUB_FFN, SUB_FFN) for s in range(TM_FFN // SUB_FFN)]
    for rows in subs:
        mix = (_dot(yssd_ref[rows, :], wo_ref[0:D_HEADS, :])
               + _dot(ysb_ref[rows, :], wo_ref[D_HEADS:2 * D_HEADS, :]))
        x1 = x_ref[rows, :] + _rms(mix, g_post_ref[...])
        x1_ref[rows, :] = x1
        h_ref[rows, :] = _rms(x1, g_pre_ref[...]).astype(BF16)
    for rows in subs:
        for cols in chunks:
            h = h_ref[rows, :]
            gate = _dot(h, wg_ref[:, cols])
            up = _dot(h, wu_ref[:, cols])
            act_ref[rows, cols] = (_silu(gate) * up).astype(BF16)
    for rows in subs:
        f = _dot(act_ref[rows, :], wd_ref[...])
        o_ref[rows, :] = x1_ref[rows, :] + _rms(f, g_out_ref[...])


def _out_ffn(x2, y_ssd, y_sb, w_out, g_post, g_pre, wg, wu, wd, g_out):
    m = x2.shape[0]
    row = lambda i: (i, 0)
    const2 = lambda i: (0, 0)
    single = pl.Buffered(1)
    return pl.pallas_call(
        _ffn_kernel,
        grid=(m // TM_FFN,),
        in_specs=[
            pl.BlockSpec((TM_FFN, D_MODEL), row),
            pl.BlockSpec((TM_FFN, D_HEADS), row),
            pl.BlockSpec((TM_FFN, D_HEADS), row),
            pl.BlockSpec((2 * D_HEADS, D_MODEL), const2, pipeline_mode=single),
            pl.BlockSpec((1, D_MODEL), const2),
            pl.BlockSpec((1, D_MODEL), const2),
            pl.BlockSpec((D_MODEL, D_FF), const2, pipeline_mode=single),
            pl.BlockSpec((D_MODEL, D_FF), const2, pipeline_mode=single),
            pl.BlockSpec((D_FF, D_MODEL), const2, pipeline_mode=single),
            pl.BlockSpec((1, D_MODEL), const2),
        ],
        out_specs=pl.BlockSpec((TM_FFN, D_MODEL), row),
        out_shape=jax.ShapeDtypeStruct((m, D_MODEL), F32),
        scratch_shapes=[
            pltpu.VMEM((TM_FFN, D_MODEL), F32),
            pltpu.VMEM((TM_FFN, D_MODEL), BF16),
            pltpu.VMEM((TM_FFN, D_FF), BF16),
        ],
        compiler_params=pltpu.CompilerParams(
            dimension_semantics=("arbitrary",), vmem_limit_bytes=VMEM_LIMIT),
        name="out_ffn",
    )(x2, y_ssd, y_sb, w_out, g_post, g_pre, wg, wu, wd, g_out)


def _expand_heads(v):
    return jnp.repeat(v.astype(F32), HEAD_DIM)[None, :]


def _pad_lanes(v):
    return jnp.pad(v.astype(F32), (0, LANES - v.shape[0]))[None, :]


def _layer(x2, batch, seq, w_in, layer, pre_mix_gain, conv_w, conv_b, dt_bias, a_log, d_skip,
           ssd_norm_gain, sb_norm_gain, w_out, post_mix_gain, pre_ffn_gain, w_gate, w_up,
           w_down, post_ffn_gain):
    utri = jnp.triu(jnp.ones((T_SSD, T_SSD), BF16))
    a_row = _pad_lanes(-jnp.exp(a_log.astype(F32)))
    q, k, v, y_ssd = _proj_ssd(x2, pre_mix_gain[None, :], w_in, layer, _pad_lanes(dt_bias),
                               conv_w, conv_b[None, :], a_row, _expand_heads(d_skip),
                               ssd_norm_gain[None, :], utri, seq)

    jj = jnp.arange(T_ATT)
    later = (jj[:, None] >= jj[None, :]).astype(BF16)
    half = jnp.concatenate([later, jnp.ones((T_ATT, T_ATT), BF16)], axis=1)
    w2 = jnp.concatenate([half, half], axis=0)
    y_sb = _attention(q, k, v, w2, sb_norm_gain[None, :], batch, seq)

    return _out_ffn(x2, y_ssd, y_sb, w_out.astype(BF16), post_mix_gain[None, :],
                    pre_ffn_gain[None, :], w_gate.astype(BF16), w_up.astype(BF16),
                    w_down.astype(BF16), post_ffn_gain[None, :])


def kernel(x, pre_mix_gain, w_in, conv_w, conv_b, dt_bias, a_log, d_skip, ssd_norm_gain,
           sb_norm_gain, w_out, post_mix_gain, pre_ffn_gain, w_gate, w_up, w_down,
           post_ffn_gain):
    batch, seq, d = x.shape
    x2 = x.reshape(batch * seq, d)
    params = (pre_mix_gain, conv_w, conv_b, dt_bias, a_log, d_skip, ssd_norm_gain,
              sb_norm_gain, w_out, post_mix_gain, pre_ffn_gain, w_gate, w_up, w_down,
              post_ffn_gain)
    for layer in range(pre_mix_gain.shape[0]):
        x2 = _layer(x2, batch, seq, w_in, layer, *(p[layer] for p in params))
    return x2.reshape(batch, seq, d)
```

```python
import functools
import itertools
import math

import jax
import jax.numpy as jnp
from jax import lax
from jax.experimental import pallas as pl
from jax.experimental.pallas import tpu as pltpu

F32 = jnp.float32
BF16 = jnp.bfloat16

EPS = 1e-6
LANES = 128
BF16_SUBLANES = 16

D_MODEL = 1024
N_HEADS = 8
HEAD_DIM = 64
D_HEADS = N_HEADS * HEAD_DIM
N_PAIRS = N_HEADS // 2
SSD_GROUPS = 2
SSD_STATE = 128
CONV_WIDTH = 4
D_CONV = D_HEADS + 2 * SSD_GROUPS * SSD_STATE
D_FF = 2816

C_Z = 0
C_XBC = C_Z + D_HEADS
C_DT = C_XBC + D_CONV
C_Q = C_DT + LANES
C_K = C_Q + D_HEADS
C_V = C_K + D_HEADS
C_END = C_V + D_HEADS

TM_PROJ = 512
PROJ_PIECE = 256
T_SSD = 128
SSD_SEGMENTS = 2 + N_PAIRS
T_ATT = 128
TM_FFN = 512
SUB_FFN = 256
FF_CHUNK = 256
CONV_TAIL = 8

VMEM_LIMIT = 56 * 1024 * 1024


def _dot(a, b):
    return jnp.dot(a, b, preferred_element_type=F32)


def _dot_nt(a, b):
    return lax.dot_general(a, b, (((1,), (1,)), ((), ())), preferred_element_type=F32)


def _dot_tn(a, b):
    return lax.dot_general(a, b, (((0,), (0,)), ((), ())), preferred_element_type=F32)


def _split3(x):
    hi = x.astype(BF16)
    r = x - hi.astype(F32)
    mid = r.astype(BF16)
    lo = (r - mid.astype(F32)).astype(BF16)
    return hi, mid, lo


def _silu(x):
    return x / (1.0 + jnp.exp(-x))


def _softplus(x):
    e = jnp.exp(-jnp.abs(x))
    u = 1.0 + e
    tiny = u == 1.0
    log1p_e = jnp.where(tiny, e, jnp.log(u) * (e / jnp.where(tiny, 1.0, u - 1.0)))
    return jnp.maximum(x, 0.0) + log1p_e


def _projection_pieces(x_ref, g_ref, w_ref, dtb_ref, h_ref, sz_ref, xbc_ref, dt_ref, q_ref,
                       k_ref, v_ref):
    x = x_ref[...]
    ms = jnp.mean(x * x, axis=-1, keepdims=True)
    h_ref[...] = (x * lax.rsqrt(ms + EPS) * g_ref[...]).astype(BF16)
    to_bf16 = lambda y: y.astype(BF16)
    segments = [(sz_ref, C_Z, C_XBC, _silu), (xbc_ref, C_XBC, C_DT, lambda y: y),
                (dt_ref, C_DT, C_Q, lambda y: _softplus(y + dtb_ref[...])),
                (q_ref, C_Q, C_K, to_bf16), (k_ref, C_K, C_V, to_bf16),
                (v_ref, C_V, C_END, to_bf16)]

    def piece(out_ref, c0, lo, hi, post):
        def run():
            out_ref[:, lo:hi] = post(_dot(h_ref[...], w_ref[:, c0 + lo:c0 + hi]))
        return run

    return [piece(out_ref, c0, lo, min(lo + PROJ_PIECE, c1 - c0), post)
            for out_ref, c0, c1, post in segments for lo in range(0, c1 - c0, PROJ_PIECE)]


def _prepare_weight(w_in_ref, w_ref):
    o_dt = D_HEADS + D_CONV
    o_q = o_dt + N_HEADS
    scale = 1.0 / math.sqrt(HEAD_DIM)
    rows_per_step = LANES
    lane = lax.broadcasted_iota(jnp.int32, (rows_per_step, LANES), 1)
    for r in range(0, D_MODEL, rows_per_step):
        rows = pl.ds(r, rows_per_step)
        w_ref[rows, C_Z:C_DT] = w_in_ref[rows, 0:o_dt].astype(BF16)
        dt_block = w_in_ref[rows, o_dt:o_dt + LANES]
        w_ref[rows, C_DT:C_Q] = jnp.where(lane < N_HEADS, dt_block, 0.0).astype(BF16)
        w_ref[rows, C_Q:C_K] = (w_in_ref[rows, o_q:o_q + D_HEADS] * scale).astype(BF16)
        w_ref[rows, C_K:C_END] = w_in_ref[rows, o_q + D_HEADS:o_q + 3 * D_HEADS].astype(BF16)


def _proj_ssd_kernel(blocks_per_seq, x_ref, g_ref, w_in_ref, dtb_ref, cw_ref, cb_ref, a_ref,
                     dskip_ref, gain_ref, utri_ref, q_ref, k_ref, v_ref, y_ref,
                     w_ref, h_ref, sz_ref, xbc_ref, dt_ref, ext_ref, state_ref):
    g = pl.program_id(0)

    @pl.when(g == 0)
    def _():
        _prepare_weight(w_in_ref, w_ref)
        sz_ref[1] = jnp.zeros(sz_ref.shape[1:], F32)
        xbc_ref[1] = jnp.zeros(xbc_ref.shape[1:], F32)
        dt_ref[1] = jnp.zeros(dt_ref.shape[1:], F32)

    @pl.when(lax.rem(g + blocks_per_seq - 1, blocks_per_seq) == 0)
    def _():
        ext_ref[0:CONV_TAIL, :] = jnp.zeros((CONV_TAIL, D_CONV), F32)
        state_ref[...] = jnp.zeros(state_ref.shape, F32)

    def step(new, old):
        pieces = _projection_pieces(x_ref, g_ref, w_ref, dtb_ref, h_ref, sz_ref.at[new],
                                    xbc_ref.at[new], dt_ref.at[new], q_ref, k_ref, v_ref)
        chunks = [pl.ds(c * T_SSD, T_SSD) for c in range(TM_PROJ // T_SSD)]
        segments = itertools.chain.from_iterable(
            _ssd_chunk(sz_ref.at[old, rows], xbc_ref.at[old, rows], dt_ref.at[old, rows],
                       cw_ref, cb_ref, a_ref, dskip_ref, gain_ref, utri_ref, y_ref.at[rows],
                       ext_ref, state_ref) for rows in chunks)
        n_segments = len(chunks) * SSD_SEGMENTS
        assert len(pieces) <= n_segments
        slot = {(n * n_segments) // len(pieces): run for n, run in enumerate(pieces)}
        for s in range(n_segments):
            if s in slot:
                slot[s]()
            next(segments)

    @pl.when(lax.rem(g, 2) == 0)
    def _():
        step(0, 1)

    @pl.when(lax.rem(g, 2) == 1)
    def _():
        step(1, 0)


def _proj_ssd(x2, gain, w_in, layer, dtb, conv_w, conv_b, a_row, dskip, ssd_gain, utri, seq):
    m = x2.shape[0]
    n_blocks = m // TM_PROJ
    assert seq % TM_PROJ == 0 and TM_PROJ % T_SSD == 0
    assert w_in.shape[1:] == (D_MODEL, C_END - (LANES - N_HEADS))
    cur = lambda g: (jnp.minimum(g, n_blocks - 1), 0)
    prev = lambda g: (jnp.maximum(g - 1, 0), 0)
    const = lambda g: (0, 0)
    qkv_spec = pl.BlockSpec((TM_PROJ, D_HEADS), cur)
    qkv_shape = jax.ShapeDtypeStruct((m, D_HEADS), BF16)
    return pl.pallas_call(
        functools.partial(_proj_ssd_kernel, seq // TM_PROJ),
        grid=(n_blocks + 1,),
        in_specs=[
            pl.BlockSpec((TM_PROJ, D_MODEL), cur),
            pl.BlockSpec((1, D_MODEL), const),
            pl.BlockSpec((None,) + w_in.shape[1:], lambda g: (layer, 0, 0),
                         pipeline_mode=pl.Buffered(1)),
            pl.BlockSpec((1, LANES), const),
            pl.BlockSpec((CONV_WIDTH, D_CONV), const),
            pl.BlockSpec((1, D_CONV), const),
            pl.BlockSpec((1, LANES), const),
            pl.BlockSpec((1, D_HEADS), const),
            pl.BlockSpec((1, D_HEADS), const),
            pl.BlockSpec((T_SSD, T_SSD), const),
        ],
        out_specs=[qkv_spec, qkv_spec, qkv_spec, pl.BlockSpec((TM_PROJ, D_HEADS), prev)],
        out_shape=[qkv_shape, qkv_shape, qkv_shape, qkv_shape],
        scratch_shapes=[
            pltpu.VMEM((D_MODEL, C_END), BF16),
            pltpu.VMEM((TM_PROJ, D_MODEL), BF16),
            pltpu.VMEM((2, TM_PROJ, D_HEADS), F32),
            pltpu.VMEM((2, TM_PROJ, D_CONV), F32),
            pltpu.VMEM((2, TM_PROJ, LANES), F32),
            pltpu.VMEM((T_SSD + CONV_TAIL, D_CONV), F32),
            pltpu.VMEM((N_PAIRS, SSD_STATE, LANES), F32),
        ],
        compiler_params=pltpu.CompilerParams(
            dimension_semantics=("arbitrary",), vmem_limit_bytes=VMEM_LIMIT),
        name="proj_ssd",
    )(x2, gain, w_in, dtb, conv_w, conv_b, a_row, dskip, ssd_gain, utri)


def _ssd_chunk(sz_ref, xbc_ref, dt_ref, cw_ref, cb_ref, a_ref, dskip_ref, gain_ref, utri_ref,
               y_ref, ext_ref, state_ref):
    t = T_SSD

    ext_ref[CONV_TAIL:CONV_TAIL + t, :] = xbc_ref[...]
    conv = cb_ref[...]
    for k in range(CONV_WIDTH):
        off = CONV_TAIL - (CONV_WIDTH - 1) + k
        conv = conv + ext_ref[off:off + t, :] * cw_ref[k:k + 1, :]
    ext_ref[0:CONV_TAIL, :] = ext_ref[t:t + CONV_TAIL, :]
    xa = _silu(conv)
    yield

    dtv = dt_ref[...]
    adt = dtv * a_ref[...]
    utri = utri_ref[...]
    acs_t = sum(_dot(part, utri) for part in _split3(adt.T))
    acs = acs_t.T

    lane = lax.broadcasted_iota(jnp.int32, (t, LANES), 1)
    first_head = lane < HEAD_DIM
    li = lax.broadcasted_iota(jnp.int32, (t, t), 0)
    si = lax.broadcasted_iota(jnp.int32, (t, t), 1)
    causal = li >= si

    def col(v, h):
        return jnp.broadcast_to(v[:, h:h + 1], (t, LANES))

    y_blocks = []
    cb_mats = []
    for g in range(SSD_GROUPS):
        bm = xa[:, D_HEADS + g * SSD_STATE:D_HEADS + (g + 1) * SSD_STATE].astype(BF16)
        cm = xa[:, D_HEADS + (SSD_GROUPS + g) * SSD_STATE:
                D_HEADS + (SSD_GROUPS + g + 1) * SSD_STATE].astype(BF16)
        cb_mats.append((bm, cm, _dot_nt(cm, bm)))
    yield

    for p in range(N_PAIRS):
        ha, hb = 2 * p, 2 * p + 1
        bm, cm, cbm = cb_mats[p // (N_PAIRS // SSD_GROUPS)]
        x2 = xa[:, p * LANES:(p + 1) * LANES]
        dt2 = jnp.where(first_head, col(dtv, ha), col(dtv, hb))
        acs2 = jnp.where(first_head, col(acs, ha), col(acs, hb))
        xdt2 = x2 * dt2

        def decay(h):
            seg = col(acs, h) - jnp.broadcast_to(acs_t[h:h + 1, :], (t, t))
            return (cbm * jnp.exp(jnp.where(causal, seg, -jnp.inf))).astype(BF16)

        m2 = jnp.concatenate([decay(ha), decay(hb)], axis=1)
        xdt_a = jnp.where(first_head, xdt2, 0.0).astype(BF16)
        xdt_b = jnp.where(first_head, 0.0, xdt2).astype(BF16)
        y_diag = _dot(m2, jnp.concatenate([xdt_a, xdt_b], axis=0))

        prev = state_ref[p]
        y_off = _dot(cm, prev.astype(BF16)) * jnp.exp(acs2)
        last = acs2[t - 1:t, :]
        xs = (xdt2 * jnp.exp(last - acs2)).astype(BF16)
        state_ref[p] = prev * jnp.exp(last) + _dot_tn(bm, xs)

        y_blocks.append(y_diag + y_off + dskip_ref[:, p * LANES:(p + 1) * LANES] * x2)
        if p + 1 < N_PAIRS:
            yield

    per_group = N_PAIRS // SSD_GROUPS
    for g in range(SSD_GROUPS):
        ys = []
        for p in range(g * per_group, (g + 1) * per_group):
            ys.append(y_blocks[p] * sz_ref[:, p * LANES:(p + 1) * LANES])
        ss = sum(jnp.sum(y * y, axis=-1, keepdims=True) for y in ys)
        inv = lax.rsqrt(ss * (1.0 / (per_group * LANES)) + EPS)
        for j, y in enumerate(ys):
            p = g * per_group + j
            y_ref[:, p * LANES:(p + 1) * LANES] = (
                y * inv * gain_ref[:, p * LANES:(p + 1) * LANES]).astype(BF16)
    yield


LOG2E = 1.4426950408889634
ATT_DEPTH = 4
ATT_NEAR_DIAGONALS = 3
ATT_UNDERFLOW_LOG2 = -160.0


def _attn_kernel(n_cast, ni_ref, nj_ref, fi_ref, fj_ref, q_ref, k_ref, v_ref, w2_ref, gain_ref,
                 *refs):
    cast_in, (o_ref, *cast_out) = refs[:n_cast], refs[n_cast:2 * n_cast + 1]
    q2_ref, kst_ref, vst_ref, r_ref, acc_ref, z_ref, arg_ref = refs[2 * n_cast + 1:]
    for src_ref, dst_ref in zip(cast_in, cast_out, strict=True):
        dst_ref[...] = src_ref[...].astype(BF16)

    t = T_ATT
    n_blocks = q2_ref.shape[0] - 1
    depth = ATT_DEPTH

    lane_v = lax.broadcasted_iota(jnp.int32, (t, LANES), 1)
    first_v = lane_v < HEAD_DIM

    for j in range(n_blocks):
        kb = k_ref[j * t:(j + 1) * t, :]
        vb = v_ref[j * t:(j + 1) * t, :]
        zero = jnp.zeros_like(kb)
        q2_ref[j] = q_ref[j * t:(j + 1) * t, :]
        kst_ref[j, 0:t, :] = jnp.where(first_v, kb, zero)
        kst_ref[j, t:2 * t, :] = jnp.where(first_v, zero, kb)
        vst_ref[j, 0:t, :] = jnp.where(first_v, vb, zero)
        vst_ref[j, t:2 * t, :] = jnp.where(first_v, zero, vb)
    q2_ref[n_blocks] = jnp.zeros(q2_ref.shape[1:], BF16)
    r_ref[n_blocks] = jnp.zeros(r_ref.shape[1:], F32)
    acc_ref[n_blocks] = jnp.zeros(acc_ref.shape[1:], F32)

    ti = lax.broadcasted_iota(jnp.int32, (t, 2 * t), 0)
    si = lax.broadcasted_iota(jnp.int32, (t, 2 * t), 1)
    strict = jnp.where(si >= t, si - t, si) < ti

    def stage_scores(tiles, z_buf):
        for u, (i, j) in enumerate(tiles):
            z_buf[u] = _dot_nt(q2_ref[i], kst_ref[j])

    def stage_suffix_sums(tiles, z_buf, diag):
        for u, (i, _) in enumerate(tiles):
            zn = z_buf[u] * (-LOG2E)
            l1 = jnp.minimum(zn, 0.0) - jnp.log2(1.0 + jnp.exp2(-jnp.abs(zn)))
            if diag:
                l1 = jnp.where(strict, l1, 0.0)
            hi = l1.astype(BF16)
            lo = (l1 - hi.astype(F32)).astype(BF16)
            res = [_dot(jnp.concatenate([hi[:, h * t:(h + 1) * t], lo[:, h * t:(h + 1) * t]],
                                        axis=1), w2_ref[...]) for h in range(2)]
            arg = jnp.concatenate([r[:, 0:t] for r in res], axis=1) - zn
            tot = jnp.concatenate([r[:, t:2 * t] for r in res], axis=1)
            if diag:
                arg = jnp.where(strict, arg, -jnp.inf)
            else:
                r_old = r_ref[i]
                arg = arg + r_old
                tot = tot + r_old
            arg_ref[u] = arg
            r_ref[i] = tot

    def stage_values(tiles, diag):
        for u, (i, j) in enumerate(tiles):
            contrib = _dot(jnp.exp2(arg_ref[u]).astype(BF16), vst_ref[j])
            if diag:
                acc_ref[i] = contrib
            else:
                acc_ref[i] += contrib

    def sweep(i_ref, j_ref, n_diag_groups):
        n_groups = i_ref.shape[0] // depth
        group = lambda g: [(i_ref[g * depth + u], j_ref[g * depth + u]) for u in range(depth)]
        n_static = n_diag_groups + 2
        assert n_static % 2 == 0 and n_groups % 2 == 0 and n_groups >= n_static

        def iteration(m, parity, static):
            if not static or m < n_groups:
                stage_scores(group(m), z_ref.at[parity])
            if not static or 0 <= m - 2 < n_groups:
                stage_values(group(m - 2), static and m - 2 < n_diag_groups)
            if not static or 0 <= m - 1 < n_groups:
                stage_suffix_sums(group(m - 1), z_ref.at[1 - parity],
                                  static and m - 1 < n_diag_groups)

        for m in range(n_static):
            iteration(m, m % 2, True)

        def body(mm, carry):
            iteration(2 * mm, 0, False)
            iteration(2 * mm + 1, 1, False)
            return carry

        lax.fori_loop(n_static // 2, n_groups // 2, body, 0)
        for m in (n_groups, n_groups + 1):
            iteration(m, m % 2, True)

    sweep(ni_ref, nj_ref, pl.cdiv(n_blocks, depth))

    far_blocks = range(ATT_NEAR_DIAGONALS, n_blocks)
    r_max = functools.reduce(jnp.maximum, [r_ref[i] for i in far_blocks])

    @pl.when(jnp.max(r_max) >= ATT_UNDERFLOW_LOG2)
    def _():
        sweep(fi_ref, fj_ref, 0)

    gain = gain_ref[...]
    for i in range(n_blocks):
        o = acc_ref[i]
        o2 = o * o
        ss_a = jnp.sum(jnp.where(first_v, o2, 0.0), axis=-1, keepdims=True)
        ss_b = jnp.sum(jnp.where(first_v, 0.0, o2), axis=-1, keepdims=True)
        ms = jnp.where(first_v, ss_a, ss_b) * (1.0 / HEAD_DIM)
        o_ref[i * t:(i + 1) * t, :] = (o * lax.rsqrt(ms + EPS) * gain).astype(BF16)


def _rms(x, g):
    ms = jnp.mean(x * x, axis=-1, keepdims=True)
    return x * lax.rsqrt(ms + EPS) * g


def _attention(q, k, v, w2, gain, to_cast, batch, seq):
    n_q = seq // T_ATT
    assert n_q % ATT_DEPTH == 0
    n_steps = batch * N_PAIRS

    def slab_spec(w):
        rows = w.shape[0]
        slab = next(s for s in range(BF16_SUBLANES, rows + 1, BF16_SUBLANES)
                    if rows % s == 0 and rows // s <= n_steps)
        last = rows // slab - 1
        return pl.BlockSpec((slab, w.shape[1]),
                            lambda b, p, *_: (jnp.minimum(b * N_PAIRS + p, last), 0))

    cast_specs = [slab_spec(w) for w in to_cast]

    def tile_list(diagonals):
        tiles = [(i, i - d) for d in diagonals for i in range(d, n_q)]
        tiles += [(n_q, 0)] * (-len(tiles) % (2 * ATT_DEPTH))
        return (jnp.asarray([i for i, _ in tiles], jnp.int32),
                jnp.asarray([j for _, j in tiles], jnp.int32))

    near = tile_list(range(ATT_NEAR_DIAGONALS))
    far = tile_list(range(ATT_NEAR_DIAGONALS, n_q))
    seq_blk = lambda b, p, *_: (b, p)
    return pl.pallas_call(
        functools.partial(_attn_kernel, len(to_cast)),
        grid_spec=pltpu.PrefetchScalarGridSpec(
            num_scalar_prefetch=4,
            grid=(batch, N_PAIRS),
            in_specs=[
                pl.BlockSpec((seq, LANES), seq_blk),
                pl.BlockSpec((seq, LANES), seq_blk),
                pl.BlockSpec((seq, LANES), seq_blk),
                pl.BlockSpec((2 * T_ATT, 2 * T_ATT), lambda b, p, *_: (0, 0)),
                pl.BlockSpec((1, LANES), lambda b, p, *_: (0, p)),
            ] + cast_specs,
            out_specs=[pl.BlockSpec((seq, LANES), seq_blk)] + cast_specs,
            scratch_shapes=[
                pltpu.VMEM((n_q + 1, T_ATT, LANES), BF16),
                pltpu.VMEM((n_q, 2 * T_ATT, LANES), BF16),
                pltpu.VMEM((n_q, 2 * T_ATT, LANES), BF16),
                pltpu.VMEM((n_q + 1, T_ATT, 2 * T_ATT), F32),
                pltpu.VMEM((n_q + 1, T_ATT, LANES), F32),
                pltpu.VMEM((2, ATT_DEPTH, T_ATT, 2 * T_ATT), F32),
                pltpu.VMEM((ATT_DEPTH, T_ATT, 2 * T_ATT), F32),
            ],
        ),
        out_shape=[jax.ShapeDtypeStruct((batch * seq, D_HEADS), BF16)]
        + [jax.ShapeDtypeStruct(w.shape, BF16) for w in to_cast],
        compiler_params=pltpu.CompilerParams(
            dimension_semantics=("arbitrary", "arbitrary"),
            vmem_limit_bytes=VMEM_LIMIT),
        name="sb_attention",
    )(*near, *far, q, k, v, w2, gain, *to_cast)


def _ffn_kernel(x_ref, yssd_ref, ysb_ref, wo_ref, g_post_ref, g_pre_ref, wg_ref, wu_ref,
                wd_ref, g_out_ref, o_ref, x1_ref, h_ref, act_ref):
    chunks = [slice(c * FF_CHUNK, (c + 1) * FF_CHUNK) for c in range(D_FF // FF_CHUNK)]
    subs = [pl.ds(s * SUB_FFN, SUB_FFN) for s in range(TM_FFN // SUB_FFN)]
    for rows in subs:
        mix = (_dot(yssd_ref[rows, :], wo_ref[0:D_HEADS, :])
               + _dot(ysb_ref[rows, :], wo_ref[D_HEADS:2 * D_HEADS, :]))
        x1 = x_ref[rows, :] + _rms(mix, g_post_ref[...])
        x1_ref[rows, :] = x1
        h_ref[rows, :] = _rms(x1, g_pre_ref[...]).astype(BF16)
    for rows in subs:
        for cols in chunks:
            h = h_ref[rows, :]
            gate = _dot(h, wg_ref[:, cols])
            up = _dot(h, wu_ref[:, cols])
            act_ref[rows, cols] = (_silu(gate) * up).astype(BF16)
    for rows in subs:
        f = _dot(act_ref[rows, :], wd_ref[...])
        o_ref[rows, :] = x1_ref[rows, :] + _rms(f, g_out_ref[...])


def _out_ffn(x2, y_ssd, y_sb, w_out, g_post, g_pre, wg, wu, wd, g_out):
    m = x2.shape[0]
    row = lambda i: (i, 0)
    const2 = lambda i: (0, 0)
    single = pl.Buffered(1)
    return pl.pallas_call(
        _ffn_kernel,
        grid=(m // TM_FFN,),
        in_specs=[
            pl.BlockSpec((TM_FFN, D_MODEL), row),
            pl.BlockSpec((TM_FFN, D_HEADS), row),
            pl.BlockSpec((TM_FFN, D_HEADS), row),
            pl.BlockSpec((2 * D_HEADS, D_MODEL), const2, pipeline_mode=single),
            pl.BlockSpec((1, D_MODEL), const2),
            pl.BlockSpec((1, D_MODEL), const2),
            pl.BlockSpec((D_MODEL, D_FF), const2, pipeline_mode=single),
            pl.BlockSpec((D_MODEL, D_FF), const2, pipeline_mode=single),
            pl.BlockSpec((D_FF, D_MODEL), const2, pipeline_mode=single),
            pl.BlockSpec((1, D_MODEL), const2),
        ],
        out_specs=pl.BlockSpec((TM_FFN, D_MODEL), row),
        out_shape=jax.ShapeDtypeStruct((m, D_MODEL), F32),
        scratch_shapes=[
            pltpu.VMEM((TM_FFN, D_MODEL), F32),
            pltpu.VMEM((TM_FFN, D_MODEL), BF16),
            pltpu.VMEM((TM_FFN, D_FF), BF16),
        ],
        compiler_params=pltpu.CompilerParams(
            dimension_semantics=("arbitrary",), vmem_limit_bytes=VMEM_LIMIT),
        name="out_ffn",
    )(x2, y_ssd, y_sb, w_out, g_post, g_pre, wg, wu, wd, g_out)


def _expand_heads(v):
    return jnp.repeat(v.astype(F32), HEAD_DIM)[None, :]


def _pad_lanes(v):
    return jnp.pad(v.astype(F32), (0, LANES - v.shape[0]))[None, :]


def _layer(x2, batch, seq, w_in, layer, pre_mix_gain, conv_w, conv_b, dt_bias, a_log, d_skip,
           ssd_norm_gain, sb_norm_gain, w_out, post_mix_gain, pre_ffn_gain, w_gate, w_up,
           w_down, post_ffn_gain):
    utri = jnp.triu(jnp.ones((T_SSD, T_SSD), BF16))
    a_row = _pad_lanes(-jnp.exp(a_log.astype(F32)))
    q, k, v, y_ssd = _proj_ssd(x2, pre_mix_gain[None, :], w_in, layer, _pad_lanes(dt_bias),
                               conv_w, conv_b[None, :], a_row, _expand_heads(d_skip),
                               ssd_norm_gain[None, :], utri, seq)

    jj = jnp.arange(T_ATT)
    later = (jj[:, None] >= jj[None, :]).astype(BF16)
    half = jnp.concatenate([later, jnp.ones((T_ATT, T_ATT), BF16)], axis=1)
    w2 = jnp.concatenate([half, half], axis=0)
    y_sb, wo, wg, wu, wd = _attention(q, k, v, w2, sb_norm_gain[None, :],
                                      (w_out, w_gate, w_up, w_down), batch, seq)

    return _out_ffn(x2, y_ssd, y_sb, wo, post_mix_gain[None, :], pre_ffn_gain[None, :], wg, wu,
                    wd, post_ffn_gain[None, :])


def kernel(x, pre_mix_gain, w_in, conv_w, conv_b, dt_bias, a_log, d_skip, ssd_norm_gain,
           sb_norm_gain, w_out, post_mix_gain, pre_ffn_gain, w_gate, w_up, w_down,
           post_ffn_gain):
    batch, seq, d = x.shape
    x2 = x.reshape(batch * seq, d)
    params = (pre_mix_gain, conv_w, conv_b, dt_bias, a_log, d_skip, ssd_norm_gain,
              sb_norm_gain, w_out, post_mix_gain, pre_ffn_gain, w_gate, w_up, w_down,
              post_ffn_gain)
    for layer in range(pre_mix_gain.shape[0]):
        x2 = _layer(x2, batch, seq, w_in, layer, *(p[layer] for p in params))
    return x2.reshape(batch, seq, d)
```

```python
import functools
import itertools
import math

import jax
import jax.numpy as jnp
from jax import lax
from jax.experimental import pallas as pl
from jax.experimental.pallas import tpu as pltpu

F32 = jnp.float32
BF16 = jnp.bfloat16

EPS = 1e-6
LANES = 128
BF16_SUBLANES = 16

D_MODEL = 1024
N_HEADS = 8
HEAD_DIM = 64
D_HEADS = N_HEADS * HEAD_DIM
N_PAIRS = N_HEADS // 2
SSD_GROUPS = 2
SSD_STATE = 128
CONV_WIDTH = 4
D_CONV = D_HEADS + 2 * SSD_GROUPS * SSD_STATE
D_FF = 2816

C_Z = 0
C_XBC = C_Z + D_HEADS
C_DT = C_XBC + D_CONV
C_Q = C_DT + LANES
C_K = C_Q + D_HEADS
C_V = C_K + D_HEADS
C_END = C_V + D_HEADS

TM_PROJ = 512
PROJ_PIECE = 256
T_SSD = 128
SSD_SEGMENTS = 2 + N_PAIRS
T_ATT = 128
TM_FFN = 512
SUB_FFN = 256
FF_CHUNK = 256
CONV_TAIL = 8

VMEM_LIMIT = 56 * 1024 * 1024


def _dot(a, b):
    return jnp.dot(a, b, preferred_element_type=F32)


def _dot_nt(a, b):
    return lax.dot_general(a, b, (((1,), (1,)), ((), ())), preferred_element_type=F32)


def _dot_tn(a, b):
    return lax.dot_general(a, b, (((0,), (0,)), ((), ())), preferred_element_type=F32)


def _split3(x):
    hi = x.astype(BF16)
    r = x - hi.astype(F32)
    mid = r.astype(BF16)
    lo = (r - mid.astype(F32)).astype(BF16)
    return hi, mid, lo


def _silu(x):
    return x / (1.0 + jnp.exp(-x))


def _softplus(x):
    e = jnp.exp(-jnp.abs(x))
    u = 1.0 + e
    tiny = u == 1.0
    log1p_e = jnp.where(tiny, e, jnp.log(u) * (e / jnp.where(tiny, 1.0, u - 1.0)))
    return jnp.maximum(x, 0.0) + log1p_e


def _projection_pieces(x_ref, g_ref, w_ref, dtb_ref, h_ref, sz_ref, xbc_ref, dt_ref, q_ref,
                       k_ref, v_ref):
    x = x_ref[...]
    ms = jnp.mean(x * x, axis=-1, keepdims=True)
    h_ref[...] = (x * lax.rsqrt(ms + EPS) * g_ref[...]).astype(BF16)
    to_bf16 = lambda y: y.astype(BF16)
    segments = [(sz_ref, C_Z, C_XBC, _silu), (xbc_ref, C_XBC, C_DT, lambda y: y),
                (dt_ref, C_DT, C_Q, lambda y: _softplus(y + dtb_ref[...])),
                (q_ref, C_Q, C_K, to_bf16), (k_ref, C_K, C_V, to_bf16),
                (v_ref, C_V, C_END, to_bf16)]

    def piece(out_ref, c0, lo, hi, post):
        def run():
            out_ref[:, lo:hi] = post(_dot(h_ref[...], w_ref[:, c0 + lo:c0 + hi]))
        return run

    return [piece(out_ref, c0, lo, min(lo + PROJ_PIECE, c1 - c0), post)
            for out_ref, c0, c1, post in segments for lo in range(0, c1 - c0, PROJ_PIECE)]


def _prepare_weight(wt_ref, w_ref):
    o_dt = D_HEADS + D_CONV
    o_q = o_dt + N_HEADS
    scale = 1.0 / math.sqrt(HEAD_DIM)
    lane = lax.broadcasted_iota(jnp.int32, (D_MODEL, LANES), 1)

    def block(row0):
        return wt_ref[row0:row0 + LANES, :].T

    for c in range(0, o_dt, LANES):
        w_ref[:, C_Z + c:C_Z + c + LANES] = block(c).astype(BF16)
    w_ref[:, C_DT:C_Q] = jnp.where(lane < N_HEADS, block(o_dt), 0.0).astype(BF16)
    for c in range(0, D_HEADS, LANES):
        w_ref[:, C_Q + c:C_Q + c + LANES] = (block(o_q + c) * scale).astype(BF16)
    for c in range(D_HEADS, 3 * D_HEADS, LANES):
        w_ref[:, C_Q + c:C_Q + c + LANES] = block(o_q + c).astype(BF16)


def _proj_ssd_kernel(blocks_per_seq, x_ref, g_ref, wt_ref, dtb_ref, cw_ref, cb_ref, a_ref,
                     dskip_ref, gain_ref, utri_ref, q_ref, k_ref, v_ref, y_ref,
                     w_ref, h_ref, sz_ref, xbc_ref, dt_ref, ext_ref, state_ref):
    g = pl.program_id(0)

    @pl.when(g == 0)
    def _():
        _prepare_weight(wt_ref, w_ref)
        sz_ref[1] = jnp.zeros(sz_ref.shape[1:], F32)
        xbc_ref[1] = jnp.zeros(xbc_ref.shape[1:], F32)
        dt_ref[1] = jnp.zeros(dt_ref.shape[1:], F32)

    @pl.when(lax.rem(g + blocks_per_seq - 1, blocks_per_seq) == 0)
    def _():
        ext_ref[0:CONV_TAIL, :] = jnp.zeros((CONV_TAIL, D_CONV), F32)
        state_ref[...] = jnp.zeros(state_ref.shape, F32)

    def step(new, old):
        pieces = _projection_pieces(x_ref, g_ref, w_ref, dtb_ref, h_ref, sz_ref.at[new],
                                    xbc_ref.at[new], dt_ref.at[new], q_ref, k_ref, v_ref)
        chunks = [pl.ds(c * T_SSD, T_SSD) for c in range(TM_PROJ // T_SSD)]
        segments = itertools.chain.from_iterable(
            _ssd_chunk(sz_ref.at[old, rows], xbc_ref.at[old, rows], dt_ref.at[old, rows],
                       cw_ref, cb_ref, a_ref, dskip_ref, gain_ref, utri_ref, y_ref.at[rows],
                       ext_ref, state_ref) for rows in chunks)
        n_segments = len(chunks) * SSD_SEGMENTS
        assert len(pieces) <= n_segments
        slot = {(n * n_segments) // len(pieces): run for n, run in enumerate(pieces)}
        for s in range(n_segments):
            if s in slot:
                slot[s]()
            next(segments)

    @pl.when(lax.rem(g, 2) == 0)
    def _():
        step(0, 1)

    @pl.when(lax.rem(g, 2) == 1)
    def _():
        step(1, 0)


def _proj_ssd(x2, gain, w_in_t, layer, dtb, conv_w, conv_b, a_row, dskip, ssd_gain, utri, seq):
    m = x2.shape[0]
    n_blocks = m // TM_PROJ
    assert seq % TM_PROJ == 0 and TM_PROJ % T_SSD == 0
    assert w_in_t.shape[1:] == (C_END - (LANES - N_HEADS), D_MODEL)
    cur = lambda g: (jnp.minimum(g, n_blocks - 1), 0)
    prev = lambda g: (jnp.maximum(g - 1, 0), 0)
    const = lambda g: (0, 0)
    qkv_spec = pl.BlockSpec((TM_PROJ, D_HEADS), cur)
    qkv_shape = jax.ShapeDtypeStruct((m, D_HEADS), BF16)
    return pl.pallas_call(
        functools.partial(_proj_ssd_kernel, seq // TM_PROJ),
        grid=(n_blocks + 1,),
        in_specs=[
            pl.BlockSpec((TM_PROJ, D_MODEL), cur),
            pl.BlockSpec((1, D_MODEL), const),
            pl.BlockSpec((None,) + w_in_t.shape[1:], lambda g: (layer, 0, 0),
                         pipeline_mode=pl.Buffered(1)),
            pl.BlockSpec((1, LANES), const),
            pl.BlockSpec((CONV_WIDTH, D_CONV), const),
            pl.BlockSpec((1, D_CONV), const),
            pl.BlockSpec((1, LANES), const),
            pl.BlockSpec((1, D_HEADS), const),
            pl.BlockSpec((1, D_HEADS), const),
            pl.BlockSpec((T_SSD, T_SSD), const),
        ],
        out_specs=[qkv_spec, qkv_spec, qkv_spec, pl.BlockSpec((TM_PROJ, D_HEADS), prev)],
        out_shape=[qkv_shape, qkv_shape, qkv_shape, qkv_shape],
        scratch_shapes=[
            pltpu.VMEM((D_MODEL, C_END), BF16),
            pltpu.VMEM((TM_PROJ, D_MODEL), BF16),
            pltpu.VMEM((2, TM_PROJ, D_HEADS), F32),
            pltpu.VMEM((2, TM_PROJ, D_CONV), F32),
            pltpu.VMEM((2, TM_PROJ, LANES), F32),
            pltpu.VMEM((T_SSD + CONV_TAIL, D_CONV), F32),
            pltpu.VMEM((N_PAIRS, SSD_STATE, LANES), F32),
        ],
        compiler_params=pltpu.CompilerParams(
            dimension_semantics=("arbitrary",), vmem_limit_bytes=VMEM_LIMIT),
        name="proj_ssd",
    )(x2, gain, w_in_t, dtb, conv_w, conv_b, a_row, dskip, ssd_gain, utri)


def _ssd_chunk(sz_ref, xbc_ref, dt_ref, cw_ref, cb_ref, a_ref, dskip_ref, gain_ref, utri_ref,
               y_ref, ext_ref, state_ref):
    t = T_SSD

    ext_ref[CONV_TAIL:CONV_TAIL + t, :] = xbc_ref[...]
    conv = cb_ref[...]
    for k in range(CONV_WIDTH):
        off = CONV_TAIL - (CONV_WIDTH - 1) + k
        conv = conv + ext_ref[off:off + t, :] * cw_ref[k:k + 1, :]
    ext_ref[0:CONV_TAIL, :] = ext_ref[t:t + CONV_TAIL, :]
    xa = _silu(conv)
    yield

    dtv = dt_ref[...]
    adt = dtv * a_ref[...]
    utri = utri_ref[...]
    acs_t = sum(_dot(part, utri) for part in _split3(adt.T))
    acs = acs_t.T

    lane = lax.broadcasted_iota(jnp.int32, (t, LANES), 1)
    first_head = lane < HEAD_DIM
    li = lax.broadcasted_iota(jnp.int32, (t, t), 0)
    si = lax.broadcasted_iota(jnp.int32, (t, t), 1)
    causal = li >= si

    def col(v, h):
        return jnp.broadcast_to(v[:, h:h + 1], (t, LANES))

    y_blocks = []
    cb_mats = []
    for g in range(SSD_GROUPS):
        bm = xa[:, D_HEADS + g * SSD_STATE:D_HEADS + (g + 1) * SSD_STATE].astype(BF16)
        cm = xa[:, D_HEADS + (SSD_GROUPS + g) * SSD_STATE:
                D_HEADS + (SSD_GROUPS + g + 1) * SSD_STATE].astype(BF16)
        cb_mats.append((bm, cm, _dot_nt(cm, bm)))
    yield

    for p in range(N_PAIRS):
        ha, hb = 2 * p, 2 * p + 1
        bm, cm, cbm = cb_mats[p // (N_PAIRS // SSD_GROUPS)]
        x2 = xa[:, p * LANES:(p + 1) * LANES]
        dt2 = jnp.where(first_head, col(dtv, ha), col(dtv, hb))
        acs2 = jnp.where(first_head, col(acs, ha), col(acs, hb))
        xdt2 = x2 * dt2

        def decay(h):
            seg = col(acs, h) - jnp.broadcast_to(acs_t[h:h + 1, :], (t, t))
            return (cbm * jnp.exp(jnp.where(causal, seg, -jnp.inf))).astype(BF16)

        m2 = jnp.concatenate([decay(ha), decay(hb)], axis=1)
        xdt_a = jnp.where(first_head, xdt2, 0.0).astype(BF16)
        xdt_b = jnp.where(first_head, 0.0, xdt2).astype(BF16)
        y_diag = _dot(m2, jnp.concatenate([xdt_a, xdt_b], axis=0))

        prev = state_ref[p]
        y_off = _dot(cm, prev.astype(BF16)) * jnp.exp(acs2)
        last = acs2[t - 1:t, :]
        xs = (xdt2 * jnp.exp(last - acs2)).astype(BF16)
        state_ref[p] = prev * jnp.exp(last) + _dot_tn(bm, xs)

        y_blocks.append(y_diag + y_off + dskip_ref[:, p * LANES:(p + 1) * LANES] * x2)
        if p + 1 < N_PAIRS:
            yield

    per_group = N_PAIRS // SSD_GROUPS
    for g in range(SSD_GROUPS):
        ys = []
        for p in range(g * per_group, (g + 1) * per_group):
            ys.append(y_blocks[p] * sz_ref[:, p * LANES:(p + 1) * LANES])
        ss = sum(jnp.sum(y * y, axis=-1, keepdims=True) for y in ys)
        inv = lax.rsqrt(ss * (1.0 / (per_group * LANES)) + EPS)
        for j, y in enumerate(ys):
            p = g * per_group + j
            y_ref[:, p * LANES:(p + 1) * LANES] = (
                y * inv * gain_ref[:, p * LANES:(p + 1) * LANES]).astype(BF16)
    yield


LOG2E = 1.4426950408889634
ATT_DEPTH = 4
ATT_NEAR_DIAGONALS = 3
ATT_UNDERFLOW_LOG2 = -160.0


def _attn_kernel(n_cast, ni_ref, nj_ref, fi_ref, fj_ref, q_ref, k_ref, v_ref, w2_ref, gain_ref,
                 *refs):
    cast_in, (o_ref, *cast_out) = refs[:n_cast], refs[n_cast:2 * n_cast + 1]
    q2_ref, kst_ref, vst_ref, r_ref, acc_ref, z_ref, arg_ref = refs[2 * n_cast + 1:]
    for src_ref, dst_ref in zip(cast_in, cast_out, strict=True):
        dst_ref[...] = src_ref[...].astype(BF16)

    t = T_ATT
    n_blocks = q2_ref.shape[0] - 1
    depth = ATT_DEPTH

    lane_v = lax.broadcasted_iota(jnp.int32, (t, LANES), 1)
    first_v = lane_v < HEAD_DIM

    for j in range(n_blocks):
        kb = k_ref[j * t:(j + 1) * t, :]
        vb = v_ref[j * t:(j + 1) * t, :]
        zero = jnp.zeros_like(kb)
        q2_ref[j] = q_ref[j * t:(j + 1) * t, :]
        kst_ref[j, 0:t, :] = jnp.where(first_v, kb, zero)
        kst_ref[j, t:2 * t, :] = jnp.where(first_v, zero, kb)
        vst_ref[j, 0:t, :] = jnp.where(first_v, vb, zero)
        vst_ref[j, t:2 * t, :] = jnp.where(first_v, zero, vb)
    q2_ref[n_blocks] = jnp.zeros(q2_ref.shape[1:], BF16)
    r_ref[n_blocks] = jnp.zeros(r_ref.shape[1:], F32)
    acc_ref[n_blocks] = jnp.zeros(acc_ref.shape[1:], F32)

    ti = lax.broadcasted_iota(jnp.int32, (t, 2 * t), 0)
    si = lax.broadcasted_iota(jnp.int32, (t, 2 * t), 1)
    strict = jnp.where(si >= t, si - t, si) < ti

    def stage_scores(tiles, z_buf):
        for u, (i, j) in enumerate(tiles):
            z_buf[u] = _dot_nt(q2_ref[i], kst_ref[j])

    def stage_suffix_sums(tiles, z_buf, diag):
        for u, (i, _) in enumerate(tiles):
            zn = z_buf[u] * (-LOG2E)
            l1 = jnp.minimum(zn, 0.0) - jnp.log2(1.0 + jnp.exp2(-jnp.abs(zn)))
            if diag:
                l1 = jnp.where(strict, l1, 0.0)
            hi = l1.astype(BF16)
            lo = (l1 - hi.astype(F32)).astype(BF16)
            res = [_dot(jnp.concatenate([hi[:, h * t:(h + 1) * t], lo[:, h * t:(h + 1) * t]],
                                        axis=1), w2_ref[...]) for h in range(2)]
            arg = jnp.concatenate([r[:, 0:t] for r in res], axis=1) - zn
            tot = jnp.concatenate([r[:, t:2 * t] for r in res], axis=1)
            if diag:
                arg = jnp.where(strict, arg, -jnp.inf)
            else:
                r_old = r_ref[i]
                arg = arg + r_old
                tot = tot + r_old
            arg_ref[u] = arg
            r_ref[i] = tot

    def stage_values(tiles, diag):
        for u, (i, j) in enumerate(tiles):
            contrib = _dot(jnp.exp2(arg_ref[u]).astype(BF16), vst_ref[j])
            if diag:
                acc_ref[i] = contrib
            else:
                acc_ref[i] += contrib

    def sweep(i_ref, j_ref, n_diag_groups):
        n_groups = i_ref.shape[0] // depth
        group = lambda g: [(i_ref[g * depth + u], j_ref[g * depth + u]) for u in range(depth)]
        n_static = n_diag_groups + 2
        assert n_static % 2 == 0 and n_groups % 2 == 0 and n_groups >= n_static

        def iteration(m, parity, static):
            if not static or m < n_groups:
                stage_scores(group(m), z_ref.at[parity])
            if not static or 0 <= m - 2 < n_groups:
                stage_values(group(m - 2), static and m - 2 < n_diag_groups)
            if not static or 0 <= m - 1 < n_groups:
                stage_suffix_sums(group(m - 1), z_ref.at[1 - parity],
                                  static and m - 1 < n_diag_groups)

        for m in range(n_static):
            iteration(m, m % 2, True)

        def body(mm, carry):
            iteration(2 * mm, 0, False)
            iteration(2 * mm + 1, 1, False)
            return carry

        lax.fori_loop(n_static // 2, n_groups // 2, body, 0)
        for m in (n_groups, n_groups + 1):
            iteration(m, m % 2, True)

    sweep(ni_ref, nj_ref, pl.cdiv(n_blocks, depth))

    far_blocks = range(ATT_NEAR_DIAGONALS, n_blocks)
    r_max = functools.reduce(jnp.maximum, [r_ref[i] for i in far_blocks])

    @pl.when(jnp.max(r_max) >= ATT_UNDERFLOW_LOG2)
    def _():
        sweep(fi_ref, fj_ref, 0)

    gain = gain_ref[...]
    for i in range(n_blocks):
        o = acc_ref[i]
        o2 = o * o
        ss_a = jnp.sum(jnp.where(first_v, o2, 0.0), axis=-1, keepdims=True)
        ss_b = jnp.sum(jnp.where(first_v, 0.0, o2), axis=-1, keepdims=True)
        ms = jnp.where(first_v, ss_a, ss_b) * (1.0 / HEAD_DIM)
        o_ref[i * t:(i + 1) * t, :] = (o * lax.rsqrt(ms + EPS) * gain).astype(BF16)


def _rms(x, g):
    ms = jnp.mean(x * x, axis=-1, keepdims=True)
    return x * lax.rsqrt(ms + EPS) * g


def _attention(q, k, v, w2, gain, to_cast, batch, seq):
    n_q = seq // T_ATT
    assert n_q % ATT_DEPTH == 0
    n_steps = batch * N_PAIRS

    def slab_spec(w):
        rows = w.shape[0]
        slab = next(s for s in range(BF16_SUBLANES, rows + 1, BF16_SUBLANES)
                    if rows % s == 0 and rows // s <= n_steps)
        last = rows // slab - 1
        return pl.BlockSpec((slab, w.shape[1]),
                            lambda b, p, *_: (jnp.minimum(b * N_PAIRS + p, last), 0))

    cast_specs = [slab_spec(w) for w in to_cast]

    def tile_list(diagonals):
        tiles = [(i, i - d) for d in diagonals for i in range(d, n_q)]
        tiles += [(n_q, 0)] * (-len(tiles) % (2 * ATT_DEPTH))
        return (jnp.asarray([i for i, _ in tiles], jnp.int32),
                jnp.asarray([j for _, j in tiles], jnp.int32))

    near = tile_list(range(ATT_NEAR_DIAGONALS))
    far = tile_list(range(ATT_NEAR_DIAGONALS, n_q))
    seq_blk = lambda b, p, *_: (b, p)
    return pl.pallas_call(
        functools.partial(_attn_kernel, len(to_cast)),
        grid_spec=pltpu.PrefetchScalarGridSpec(
            num_scalar_prefetch=4,
            grid=(batch, N_PAIRS),
            in_specs=[
                pl.BlockSpec((seq, LANES), seq_blk),
                pl.BlockSpec((seq, LANES), seq_blk),
                pl.BlockSpec((seq, LANES), seq_blk),
                pl.BlockSpec((2 * T_ATT, 2 * T_ATT), lambda b, p, *_: (0, 0)),
                pl.BlockSpec((1, LANES), lambda b, p, *_: (0, p)),
            ] + cast_specs,
            out_specs=[pl.BlockSpec((seq, LANES), seq_blk)] + cast_specs,
            scratch_shapes=[
                pltpu.VMEM((n_q + 1, T_ATT, LANES), BF16),
                pltpu.VMEM((n_q, 2 * T_ATT, LANES), BF16),
                pltpu.VMEM((n_q, 2 * T_ATT, LANES), BF16),
                pltpu.VMEM((n_q + 1, T_ATT, 2 * T_ATT), F32),
                pltpu.VMEM((n_q + 1, T_ATT, LANES), F32),
                pltpu.VMEM((2, ATT_DEPTH, T_ATT, 2 * T_ATT), F32),
                pltpu.VMEM((ATT_DEPTH, T_ATT, 2 * T_ATT), F32),
            ],
        ),
        out_shape=[jax.ShapeDtypeStruct((batch * seq, D_HEADS), BF16)]
        + [jax.ShapeDtypeStruct(w.shape, BF16) for w in to_cast],
        compiler_params=pltpu.CompilerParams(
            dimension_semantics=("arbitrary", "arbitrary"),
            vmem_limit_bytes=VMEM_LIMIT),
        name="sb_attention",
    )(*near, *far, q, k, v, w2, gain, *to_cast)


def _ffn_kernel(x_ref, yssd_ref, ysb_ref, wo_ref, g_post_ref, g_pre_ref, wg_ref, wu_ref,
                wd_ref, g_out_ref, o_ref, x1_ref, h_ref, act_ref):
    chunks = [slice(c * FF_CHUNK, (c + 1) * FF_CHUNK) for c in range(D_FF // FF_CHUNK)]
    subs = [pl.ds(s * SUB_FFN, SUB_FFN) for s in range(TM_FFN // SUB_FFN)]
    for rows in subs:
        mix = (_dot(yssd_ref[rows, :], wo_ref[0:D_HEADS, :])
               + _dot(ysb_ref[rows, :], wo_ref[D_HEADS:2 * D_HEADS, :]))
        x1 = x_ref[rows, :] + _rms(mix, g_post_ref[...])
        x1_ref[rows, :] = x1
        h_ref[rows, :] = _rms(x1, g_pre_ref[...]).astype(BF16)
    for rows in subs:
        for cols in chunks:
            h = h_ref[rows, :]
            gate = _dot(h, wg_ref[:, cols])
            up = _dot(h, wu_ref[:, cols])
            act_ref[rows, cols] = (_silu(gate) * up).astype(BF16)
    for rows in subs:
        f = _dot(act_ref[rows, :], wd_ref[...])
        o_ref[rows, :] = x1_ref[rows, :] + _rms(f, g_out_ref[...])


def _out_ffn(x2, y_ssd, y_sb, w_out, g_post, g_pre, wg, wu, wd, g_out):
    m = x2.shape[0]
    row = lambda i: (i, 0)
    const2 = lambda i: (0, 0)
    single = pl.Buffered(1)
    return pl.pallas_call(
        _ffn_kernel,
        grid=(m // TM_FFN,),
        in_specs=[
            pl.BlockSpec((TM_FFN, D_MODEL), row),
            pl.BlockSpec((TM_FFN, D_HEADS), row),
            pl.BlockSpec((TM_FFN, D_HEADS), row),
            pl.BlockSpec((2 * D_HEADS, D_MODEL), const2, pipeline_mode=single),
            pl.BlockSpec((1, D_MODEL), const2),
            pl.BlockSpec((1, D_MODEL), const2),
            pl.BlockSpec((D_MODEL, D_FF), const2, pipeline_mode=single),
            pl.BlockSpec((D_MODEL, D_FF), const2, pipeline_mode=single),
            pl.BlockSpec((D_FF, D_MODEL), const2, pipeline_mode=single),
            pl.BlockSpec((1, D_MODEL), const2),
        ],
        out_specs=pl.BlockSpec((TM_FFN, D_MODEL), row),
        out_shape=jax.ShapeDtypeStruct((m, D_MODEL), F32),
        scratch_shapes=[
            pltpu.VMEM((TM_FFN, D_MODEL), F32),
            pltpu.VMEM((TM_FFN, D_MODEL), BF16),
            pltpu.VMEM((TM_FFN, D_FF), BF16),
        ],
        compiler_params=pltpu.CompilerParams(
            dimension_semantics=("arbitrary",), vmem_limit_bytes=VMEM_LIMIT),
        name="out_ffn",
    )(x2, y_ssd, y_sb, w_out, g_post, g_pre, wg, wu, wd, g_out)


def _expand_heads(v):
    return jnp.repeat(v.astype(F32), HEAD_DIM)[None, :]


def _pad_lanes(v):
    return jnp.pad(v.astype(F32), (0, LANES - v.shape[0]))[None, :]


def _layer(x2, batch, seq, w_in, layer, pre_mix_gain, conv_w, conv_b, dt_bias, a_log, d_skip,
           ssd_norm_gain, sb_norm_gain, w_out, post_mix_gain, pre_ffn_gain, w_gate, w_up,
           w_down, post_ffn_gain):
    utri = jnp.triu(jnp.ones((T_SSD, T_SSD), BF16))
    a_row = _pad_lanes(-jnp.exp(a_log.astype(F32)))
    q, k, v, y_ssd = _proj_ssd(x2, pre_mix_gain[None, :], jnp.swapaxes(w_in, 1, 2), layer,
                               _pad_lanes(dt_bias), conv_w, conv_b[None, :], a_row,
                               _expand_heads(d_skip), ssd_norm_gain[None, :], utri, seq)

    jj = jnp.arange(T_ATT)
    later = (jj[:, None] >= jj[None, :]).astype(BF16)
    half = jnp.concatenate([later, jnp.ones((T_ATT, T_ATT), BF16)], axis=1)
    w2 = jnp.concatenate([half, half], axis=0)
    y_sb, wo, wg, wu, wd = _attention(q, k, v, w2, sb_norm_gain[None, :],
                                      (w_out, w_gate, w_up, w_down), batch, seq)

    return _out_ffn(x2, y_ssd, y_sb, wo, post_mix_gain[None, :], pre_ffn_gain[None, :], wg, wu,
                    wd, post_ffn_gain[None, :])


def kernel(x, pre_mix_gain, w_in, conv_w, conv_b, dt_bias, a_log, d_skip, ssd_norm_gain,
           sb_norm_gain, w_out, post_mix_gain, pre_ffn_gain, w_gate, w_up, w_down,
           post_ffn_gain):
    batch, seq, d = x.shape
    x2 = x.reshape(batch * seq, d)
    params = (pre_mix_gain, conv_w, conv_b, dt_bias, a_log, d_skip, ssd_norm_gain,
              sb_norm_gain, w_out, post_mix_gain, pre_ffn_gain, w_gate, w_up, w_down,
              post_ffn_gain)
    for layer in range(pre_mix_gain.shape[0]):
        x2 = _layer(x2, batch, seq, w_in, layer, *(p[layer] for p in params))
    return x2.reshape(batch, seq, d)
```

```python
import functools
import itertools
import math

import jax
import jax.numpy as jnp
from jax import lax
from jax.experimental import pallas as pl
from jax.experimental.pallas import tpu as pltpu

F32 = jnp.float32
BF16 = jnp.bfloat16

EPS = 1e-6
LANES = 128
BF16_SUBLANES = 16

D_MODEL = 1024
N_HEADS = 8
HEAD_DIM = 64
D_HEADS = N_HEADS * HEAD_DIM
N_PAIRS = N_HEADS // 2
SSD_GROUPS = 2
SSD_STATE = 128
CONV_WIDTH = 4
D_CONV = D_HEADS + 2 * SSD_GROUPS * SSD_STATE
D_FF = 2816

C_Z = 0
C_XBC = C_Z + D_HEADS
C_DT = C_XBC + D_CONV
C_Q = C_DT + LANES
C_K = C_Q + D_HEADS
C_V = C_K + D_HEADS
C_END = C_V + D_HEADS

TM_PROJ = 512
PROJ_PIECE = 256
T_SSD = 128
SSD_SEGMENTS = 2 + N_PAIRS
T_ATT = 128
TM_FFN = 512
SUB_FFN = 256
FF_CHUNK = 256
CONV_TAIL = 8

VMEM_LIMIT = 56 * 1024 * 1024


def _dot(a, b):
    return jnp.dot(a, b, preferred_element_type=F32)


def _dot_nt(a, b):
    return lax.dot_general(a, b, (((1,), (1,)), ((), ())), preferred_element_type=F32)


def _dot_tn(a, b):
    return lax.dot_general(a, b, (((0,), (0,)), ((), ())), preferred_element_type=F32)


def _split3(x):
    hi = x.astype(BF16)
    r = x - hi.astype(F32)
    mid = r.astype(BF16)
    lo = (r - mid.astype(F32)).astype(BF16)
    return hi, mid, lo


def _silu(x):
    return x / (1.0 + jnp.exp(-x))


def _softplus(x):
    e = jnp.exp(-jnp.abs(x))
    u = 1.0 + e
    tiny = u == 1.0
    log1p_e = jnp.where(tiny, e, jnp.log(u) * (e / jnp.where(tiny, 1.0, u - 1.0)))
    return jnp.maximum(x, 0.0) + log1p_e


def _projection_pieces(x_ref, g_ref, w_ref, dtb_ref, h_ref, sz_ref, xbc_ref, dt_ref, q_ref,
                       k_ref, v_ref):
    x = x_ref[...]
    ms = jnp.mean(x * x, axis=-1, keepdims=True)
    h_ref[...] = (x * lax.rsqrt(ms + EPS) * g_ref[...]).astype(BF16)
    to_bf16 = lambda y: y.astype(BF16)
    segments = [(sz_ref, C_Z, C_XBC, _silu), (xbc_ref, C_XBC, C_DT, lambda y: y),
                (dt_ref, C_DT, C_Q, lambda y: _softplus(y + dtb_ref[...])),
                (q_ref, C_Q, C_K, to_bf16), (k_ref, C_K, C_V, to_bf16),
                (v_ref, C_V, C_END, to_bf16)]

    def piece(out_ref, c0, lo, hi, post):
        def run():
            out_ref[:, lo:hi] = post(_dot(h_ref[...], w_ref[:, c0 + lo:c0 + hi]))
        return run

    return [piece(out_ref, c0, lo, min(lo + PROJ_PIECE, c1 - c0), post)
            for out_ref, c0, c1, post in segments for lo in range(0, c1 - c0, PROJ_PIECE)]


def _prepare_weight(wt_ref, w_ref):
    o_dt = D_HEADS + D_CONV
    o_q = o_dt + N_HEADS
    scale = 1.0 / math.sqrt(HEAD_DIM)
    lane = lax.broadcasted_iota(jnp.int32, (D_MODEL, LANES), 1)

    def block(row0):
        return wt_ref[row0:row0 + LANES, :].T

    for c in range(0, o_dt, LANES):
        w_ref[:, C_Z + c:C_Z + c + LANES] = block(c).astype(BF16)
    w_ref[:, C_DT:C_Q] = jnp.where(lane < N_HEADS, block(o_dt), 0.0).astype(BF16)
    for c in range(0, D_HEADS, LANES):
        w_ref[:, C_Q + c:C_Q + c + LANES] = (block(o_q + c) * scale).astype(BF16)
    for c in range(D_HEADS, 3 * D_HEADS, LANES):
        w_ref[:, C_Q + c:C_Q + c + LANES] = block(o_q + c).astype(BF16)


def _proj_ssd_kernel(blocks_per_seq, x_ref, g_ref, wt_ref, dtb_ref, cw_ref, cb_ref, a_ref,
                     dskip_ref, gain_ref, utri_ref, q_ref, k_ref, v_ref, y_ref,
                     w_ref, h_ref, sz_ref, xbc_ref, dt_ref, ext_ref, state_ref):
    g = pl.program_id(0)

    @pl.when(g == 0)
    def _():
        _prepare_weight(wt_ref, w_ref)
        sz_ref[1] = jnp.zeros(sz_ref.shape[1:], F32)
        xbc_ref[1] = jnp.zeros(xbc_ref.shape[1:], F32)
        dt_ref[1] = jnp.zeros(dt_ref.shape[1:], F32)

    @pl.when(lax.rem(g + blocks_per_seq - 1, blocks_per_seq) == 0)
    def _():
        ext_ref[0:CONV_TAIL, :] = jnp.zeros((CONV_TAIL, D_CONV), F32)
        state_ref[...] = jnp.zeros(state_ref.shape, F32)

    def step(new, old):
        pieces = _projection_pieces(x_ref, g_ref, w_ref, dtb_ref, h_ref, sz_ref.at[new],
                                    xbc_ref.at[new], dt_ref.at[new], q_ref, k_ref, v_ref)
        chunks = [pl.ds(c * T_SSD, T_SSD) for c in range(TM_PROJ // T_SSD)]
        segments = itertools.chain.from_iterable(
            _ssd_chunk(sz_ref.at[old, rows], xbc_ref.at[old, rows], dt_ref.at[old, rows],
                       cw_ref, cb_ref, a_ref, dskip_ref, gain_ref, utri_ref, y_ref.at[rows],
                       ext_ref, state_ref) for rows in chunks)
        n_segments = len(chunks) * SSD_SEGMENTS
        assert len(pieces) <= n_segments
        slot = {(n * n_segments) // len(pieces): run for n, run in enumerate(pieces)}
        for s in range(n_segments):
            if s in slot:
                slot[s]()
            next(segments)

    @pl.when(lax.rem(g, 2) == 0)
    def _():
        step(0, 1)

    @pl.when(lax.rem(g, 2) == 1)
    def _():
        step(1, 0)


def _proj_ssd(x2, gain, w_in_t, layer, dtb, conv_w, conv_b, a_row, dskip, ssd_gain, utri, seq):
    m = x2.shape[0]
    n_blocks = m // TM_PROJ
    assert seq % TM_PROJ == 0 and TM_PROJ % T_SSD == 0
    assert w_in_t.shape[1:] == (C_END - (LANES - N_HEADS), D_MODEL)
    cur = lambda g: (jnp.minimum(g, n_blocks - 1), 0)
    prev = lambda g: (jnp.maximum(g - 1, 0), 0)
    const = lambda g: (0, 0)
    qkv_spec = pl.BlockSpec((TM_PROJ, D_HEADS), cur)
    qkv_shape = jax.ShapeDtypeStruct((m, D_HEADS), BF16)
    return pl.pallas_call(
        functools.partial(_proj_ssd_kernel, seq // TM_PROJ),
        grid=(n_blocks + 1,),
        in_specs=[
            pl.BlockSpec((TM_PROJ, D_MODEL), cur),
            pl.BlockSpec((1, D_MODEL), const),
            pl.BlockSpec((None,) + w_in_t.shape[1:], lambda g: (layer, 0, 0),
                         pipeline_mode=pl.Buffered(1)),
            pl.BlockSpec((1, LANES), const),
            pl.BlockSpec((CONV_WIDTH, D_CONV), const),
            pl.BlockSpec((1, D_CONV), const),
            pl.BlockSpec((1, LANES), const),
            pl.BlockSpec((1, D_HEADS), const),
            pl.BlockSpec((1, D_HEADS), const),
            pl.BlockSpec((T_SSD, T_SSD), const),
        ],
        out_specs=[qkv_spec, qkv_spec, qkv_spec, pl.BlockSpec((TM_PROJ, D_HEADS), prev)],
        out_shape=[qkv_shape, qkv_shape, qkv_shape, qkv_shape],
        scratch_shapes=[
            pltpu.VMEM((D_MODEL, C_END), BF16),
            pltpu.VMEM((TM_PROJ, D_MODEL), BF16),
            pltpu.VMEM((2, TM_PROJ, D_HEADS), F32),
            pltpu.VMEM((2, TM_PROJ, D_CONV), F32),
            pltpu.VMEM((2, TM_PROJ, LANES), F32),
            pltpu.VMEM((T_SSD + CONV_TAIL, D_CONV), F32),
            pltpu.VMEM((N_PAIRS, SSD_STATE, LANES), F32),
        ],
        compiler_params=pltpu.CompilerParams(
            dimension_semantics=("arbitrary",), vmem_limit_bytes=VMEM_LIMIT),
        name="proj_ssd",
    )(x2, gain, w_in_t, dtb, conv_w, conv_b, a_row, dskip, ssd_gain, utri)


def _ssd_chunk(sz_ref, xbc_ref, dt_ref, cw_ref, cb_ref, a_ref, dskip_ref, gain_ref, utri_ref,
               y_ref, ext_ref, state_ref):
    t = T_SSD

    ext_ref[CONV_TAIL:CONV_TAIL + t, :] = xbc_ref[...]
    conv = cb_ref[...]
    for k in range(CONV_WIDTH):
        off = CONV_TAIL - (CONV_WIDTH - 1) + k
        conv = conv + ext_ref[off:off + t, :] * cw_ref[k:k + 1, :]
    ext_ref[0:CONV_TAIL, :] = ext_ref[t:t + CONV_TAIL, :]
    xa = _silu(conv)
    yield

    dtv = dt_ref[...]
    adt = dtv * a_ref[...]
    utri = utri_ref[...]
    acs_t = sum(_dot(part, utri) for part in _split3(adt.T))
    acs = acs_t.T

    lane = lax.broadcasted_iota(jnp.int32, (t, LANES), 1)
    first_head = lane < HEAD_DIM
    li = lax.broadcasted_iota(jnp.int32, (t, t), 0)
    si = lax.broadcasted_iota(jnp.int32, (t, t), 1)
    causal = li >= si

    def col(v, h):
        return jnp.broadcast_to(v[:, h:h + 1], (t, LANES))

    y_blocks = []
    cb_mats = []
    for g in range(SSD_GROUPS):
        bm = xa[:, D_HEADS + g * SSD_STATE:D_HEADS + (g + 1) * SSD_STATE].astype(BF16)
        cm = xa[:, D_HEADS + (SSD_GROUPS + g) * SSD_STATE:
                D_HEADS + (SSD_GROUPS + g + 1) * SSD_STATE].astype(BF16)
        cb_mats.append((bm, cm, _dot_nt(cm, bm)))
    yield

    for p in range(N_PAIRS):
        ha, hb = 2 * p, 2 * p + 1
        bm, cm, cbm = cb_mats[p // (N_PAIRS // SSD_GROUPS)]
        x2 = xa[:, p * LANES:(p + 1) * LANES]
        dt2 = jnp.where(first_head, col(dtv, ha), col(dtv, hb))
        acs2 = jnp.where(first_head, col(acs, ha), col(acs, hb))
        xdt2 = x2 * dt2

        def decay(h):
            seg = col(acs, h) - jnp.broadcast_to(acs_t[h:h + 1, :], (t, t))
            return (cbm * jnp.exp(jnp.where(causal, seg, -jnp.inf))).astype(BF16)

        m2 = jnp.concatenate([decay(ha), decay(hb)], axis=1)
        xdt_a = jnp.where(first_head, xdt2, 0.0).astype(BF16)
        xdt_b = jnp.where(first_head, 0.0, xdt2).astype(BF16)
        y_diag = _dot(m2, jnp.concatenate([xdt_a, xdt_b], axis=0))

        prev = state_ref[p]
        y_off = _dot(cm, prev.astype(BF16)) * jnp.exp(acs2)
        last = acs2[t - 1:t, :]
        xs = (xdt2 * jnp.exp(last - acs2)).astype(BF16)
        state_ref[p] = prev * jnp.exp(last) + _dot_tn(bm, xs)

        y_blocks.append(y_diag + y_off + dskip_ref[:, p * LANES:(p + 1) * LANES] * x2)
        if p + 1 < N_PAIRS:
            yield

    per_group = N_PAIRS // SSD_GROUPS
    for g in range(SSD_GROUPS):
        ys = []
        for p in range(g * per_group, (g + 1) * per_group):
            ys.append(y_blocks[p] * sz_ref[:, p * LANES:(p + 1) * LANES])
        ss = sum(jnp.sum(y * y, axis=-1, keepdims=True) for y in ys)
        inv = lax.rsqrt(ss * (1.0 / (per_group * LANES)) + EPS)
        for j, y in enumerate(ys):
            p = g * per_group + j
            y_ref[:, p * LANES:(p + 1) * LANES] = (
                y * inv * gain_ref[:, p * LANES:(p + 1) * LANES]).astype(BF16)
    yield


LOG2E = 1.4426950408889634
ATT_DEPTH = 2
ATT_NEAR_DIAGONALS = 3
ATT_UNDERFLOW_LOG2 = -160.0


def _attn_kernel(n_cast, ni_ref, nj_ref, fi_ref, fj_ref, q_ref, k_ref, v_ref, w2_ref, gain_ref,
                 *refs):
    cast_in, (o_ref, *cast_out) = refs[:n_cast], refs[n_cast:2 * n_cast + 1]
    q2_ref, kst_ref, vst_ref, r_ref, acc_ref, z_ref, arg_ref = refs[2 * n_cast + 1:]
    for src_ref, dst_ref in zip(cast_in, cast_out, strict=True):
        dst_ref[...] = src_ref[...].astype(BF16)

    t = T_ATT
    n_blocks = q2_ref.shape[0] - 1
    depth = ATT_DEPTH

    lane_v = lax.broadcasted_iota(jnp.int32, (t, LANES), 1)
    first_v = lane_v < HEAD_DIM

    for j in range(n_blocks):
        kb = k_ref[j * t:(j + 1) * t, :]
        vb = v_ref[j * t:(j + 1) * t, :]
        zero = jnp.zeros_like(kb)
        q2_ref[j] = q_ref[j * t:(j + 1) * t, :]
        kst_ref[j, 0:t, :] = jnp.where(first_v, kb, zero)
        kst_ref[j, t:2 * t, :] = jnp.where(first_v, zero, kb)
        vst_ref[j, 0:t, :] = jnp.where(first_v, vb, zero)
        vst_ref[j, t:2 * t, :] = jnp.where(first_v, zero, vb)
    q2_ref[n_blocks] = jnp.zeros(q2_ref.shape[1:], BF16)
    r_ref[n_blocks] = jnp.zeros(r_ref.shape[1:], F32)
    acc_ref[n_blocks] = jnp.zeros(acc_ref.shape[1:], F32)

    ti = lax.broadcasted_iota(jnp.int32, (t, 2 * t), 0)
    si = lax.broadcasted_iota(jnp.int32, (t, 2 * t), 1)
    strict = jnp.where(si >= t, si - t, si) < ti

    def stage_scores(tiles, z_buf):
        for u, (i, j) in enumerate(tiles):
            z_buf[u] = _dot_nt(q2_ref[i], kst_ref[j])

    def stage_suffix_sums(tiles, z_buf, diag):
        for u, (i, _) in enumerate(tiles):
            zn = z_buf[u] * (-LOG2E)
            l1 = jnp.minimum(zn, 0.0) - jnp.log2(1.0 + jnp.exp2(-jnp.abs(zn)))
            if diag:
                l1 = jnp.where(strict, l1, 0.0)
            hi = l1.astype(BF16)
            lo = (l1 - hi.astype(F32)).astype(BF16)
            res = [_dot(jnp.concatenate([hi[:, h * t:(h + 1) * t], lo[:, h * t:(h + 1) * t]],
                                        axis=1), w2_ref[...]) for h in range(2)]
            arg = jnp.concatenate([r[:, 0:t] for r in res], axis=1) - zn
            tot = jnp.concatenate([r[:, t:2 * t] for r in res], axis=1)
            if diag:
                arg = jnp.where(strict, arg, -jnp.inf)
            else:
                r_old = r_ref[i]
                arg = arg + r_old
                tot = tot + r_old
            arg_ref[u] = arg
            r_ref[i] = tot

    def stage_values(tiles, diag):
        for u, (i, j) in enumerate(tiles):
            contrib = _dot(jnp.exp2(arg_ref[u]).astype(BF16), vst_ref[j])
            if diag:
                acc_ref[i] = contrib
            else:
                acc_ref[i] += contrib

    def sweep(i_ref, j_ref, n_diag_groups):
        n_groups = i_ref.shape[0] // depth
        group = lambda g: [(i_ref[g * depth + u], j_ref[g * depth + u]) for u in range(depth)]
        n_static = n_diag_groups + 2
        assert n_static % 2 == 0 and n_groups % 2 == 0 and n_groups >= n_static

        def iteration(m, parity, static):
            if not static or m < n_groups:
                stage_scores(group(m), z_ref.at[parity])
            if not static or 0 <= m - 2 < n_groups:
                stage_values(group(m - 2), static and m - 2 < n_diag_groups)
            if not static or 0 <= m - 1 < n_groups:
                stage_suffix_sums(group(m - 1), z_ref.at[1 - parity],
                                  static and m - 1 < n_diag_groups)

        for m in range(n_static):
            iteration(m, m % 2, True)

        def body(mm, carry):
            iteration(2 * mm, 0, False)
            iteration(2 * mm + 1, 1, False)
            return carry

        lax.fori_loop(n_static // 2, n_groups // 2, body, 0)
        for m in (n_groups, n_groups + 1):
            iteration(m, m % 2, True)

    sweep(ni_ref, nj_ref, pl.cdiv(n_blocks, depth))

    far_blocks = range(ATT_NEAR_DIAGONALS, n_blocks)
    r_max = functools.reduce(jnp.maximum, [r_ref[i] for i in far_blocks])

    @pl.when(jnp.max(r_max) >= ATT_UNDERFLOW_LOG2)
    def _():
        sweep(fi_ref, fj_ref, 0)

    gain = gain_ref[...]
    for i in range(n_blocks):
        o = acc_ref[i]
        o2 = o * o
        ss_a = jnp.sum(jnp.where(first_v, o2, 0.0), axis=-1, keepdims=True)
        ss_b = jnp.sum(jnp.where(first_v, 0.0, o2), axis=-1, keepdims=True)
        ms = jnp.where(first_v, ss_a, ss_b) * (1.0 / HEAD_DIM)
        o_ref[i * t:(i + 1) * t, :] = (o * lax.rsqrt(ms + EPS) * gain).astype(BF16)


def _rms(x, g):
    ms = jnp.mean(x * x, axis=-1, keepdims=True)
    return x * lax.rsqrt(ms + EPS) * g


def _attention(q, k, v, w2, gain, to_cast, batch, seq):
    n_q = seq // T_ATT
    assert n_q % ATT_DEPTH == 0
    n_steps = batch * N_PAIRS

    def slab_spec(w):
        rows = w.shape[0]
        slab = next(s for s in range(BF16_SUBLANES, rows + 1, BF16_SUBLANES)
                    if rows % s == 0 and rows // s <= n_steps)
        last = rows // slab - 1
        return pl.BlockSpec((slab, w.shape[1]),
                            lambda b, p, *_: (jnp.minimum(b * N_PAIRS + p, last), 0))

    cast_specs = [slab_spec(w) for w in to_cast]

    def tile_list(diagonals):
        tiles = [(i, i - d) for d in diagonals for i in range(d, n_q)]
        tiles += [(n_q, 0)] * (-len(tiles) % (2 * ATT_DEPTH))
        return (jnp.asarray([i for i, _ in tiles], jnp.int32),
                jnp.asarray([j for _, j in tiles], jnp.int32))

    near = tile_list(range(ATT_NEAR_DIAGONALS))
    far = tile_list(range(ATT_NEAR_DIAGONALS, n_q))
    seq_blk = lambda b, p, *_: (b, p)
    return pl.pallas_call(
        functools.partial(_attn_kernel, len(to_cast)),
        grid_spec=pltpu.PrefetchScalarGridSpec(
            num_scalar_prefetch=4,
            grid=(batch, N_PAIRS),
            in_specs=[
                pl.BlockSpec((seq, LANES), seq_blk),
                pl.BlockSpec((seq, LANES), seq_blk),
                pl.BlockSpec((seq, LANES), seq_blk),
                pl.BlockSpec((2 * T_ATT, 2 * T_ATT), lambda b, p, *_: (0, 0)),
                pl.BlockSpec((1, LANES), lambda b, p, *_: (0, p)),
            ] + cast_specs,
            out_specs=[pl.BlockSpec((seq, LANES), seq_blk)] + cast_specs,
            scratch_shapes=[
                pltpu.VMEM((n_q + 1, T_ATT, LANES), BF16),
                pltpu.VMEM((n_q, 2 * T_ATT, LANES), BF16),
                pltpu.VMEM((n_q, 2 * T_ATT, LANES), BF16),
                pltpu.VMEM((n_q + 1, T_ATT, 2 * T_ATT), F32),
                pltpu.VMEM((n_q + 1, T_ATT, LANES), F32),
                pltpu.VMEM((2, ATT_DEPTH, T_ATT, 2 * T_ATT), F32),
                pltpu.VMEM((ATT_DEPTH, T_ATT, 2 * T_ATT), F32),
            ],
        ),
        out_shape=[jax.ShapeDtypeStruct((batch * seq, D_HEADS), BF16)]
        + [jax.ShapeDtypeStruct(w.shape, BF16) for w in to_cast],
        compiler_params=pltpu.CompilerParams(
            dimension_semantics=("arbitrary", "arbitrary"),
            vmem_limit_bytes=VMEM_LIMIT),
        name="sb_attention",
    )(*near, *far, q, k, v, w2, gain, *to_cast)


def _ffn_kernel(x_ref, yssd_ref, ysb_ref, wo_ref, g_post_ref, g_pre_ref, wg_ref, wu_ref,
                wd_ref, g_out_ref, o_ref, x1_ref, h_ref, act_ref):
    chunks = [slice(c * FF_CHUNK, (c + 1) * FF_CHUNK) for c in range(D_FF // FF_CHUNK)]
    subs = [pl.ds(s * SUB_FFN, SUB_FFN) for s in range(TM_FFN // SUB_FFN)]
    for rows in subs:
        mix = (_dot(yssd_ref[rows, :], wo_ref[0:D_HEADS, :])
               + _dot(ysb_ref[rows, :], wo_ref[D_HEADS:2 * D_HEADS, :]))
        x1 = x_ref[rows, :] + _rms(mix, g_post_ref[...])
        x1_ref[rows, :] = x1
        h_ref[rows, :] = _rms(x1, g_pre_ref[...]).astype(BF16)
    for rows in subs:
        for cols in chunks:
            h = h_ref[rows, :]
            gate = _dot(h, wg_ref[:, cols])
            up = _dot(h, wu_ref[:, cols])
            act_ref[rows, cols] = (_silu(gate) * up).astype(BF16)
    for rows in subs:
        f = _dot(act_ref[rows, :], wd_ref[...])
        o_ref[rows, :] = x1_ref[rows, :] + _rms(f, g_out_ref[...])


def _out_ffn(x2, y_ssd, y_sb, w_out, g_post, g_pre, wg, wu, wd, g_out):
    m = x2.shape[0]
    row = lambda i: (i, 0)
    const2 = lambda i: (0, 0)
    single = pl.Buffered(1)
    return pl.pallas_call(
        _ffn_kernel,
        grid=(m // TM_FFN,),
        in_specs=[
            pl.BlockSpec((TM_FFN, D_MODEL), row),
            pl.BlockSpec((TM_FFN, D_HEADS), row),
            pl.BlockSpec((TM_FFN, D_HEADS), row),
            pl.BlockSpec((2 * D_HEADS, D_MODEL), const2, pipeline_mode=single),
            pl.BlockSpec((1, D_MODEL), const2),
            pl.BlockSpec((1, D_MODEL), const2),
            pl.BlockSpec((D_MODEL, D_FF), const2, pipeline_mode=single),
            pl.BlockSpec((D_MODEL, D_FF), const2, pipeline_mode=single),
            pl.BlockSpec((D_FF, D_MODEL), const2, pipeline_mode=single),
            pl.BlockSpec((1, D_MODEL), const2),
        ],
        out_specs=pl.BlockSpec((TM_FFN, D_MODEL), row),
        out_shape=jax.ShapeDtypeStruct((m, D_MODEL), F32),
        scratch_shapes=[
            pltpu.VMEM((TM_FFN, D_MODEL), F32),
            pltpu.VMEM((TM_FFN, D_MODEL), BF16),
            pltpu.VMEM((TM_FFN, D_FF), BF16),
        ],
        compiler_params=pltpu.CompilerParams(
            dimension_semantics=("arbitrary",), vmem_limit_bytes=VMEM_LIMIT),
        name="out_ffn",
    )(x2, y_ssd, y_sb, w_out, g_post, g_pre, wg, wu, wd, g_out)


def _expand_heads(v):
    return jnp.repeat(v.astype(F32), HEAD_DIM)[None, :]


def _pad_lanes(v):
    return jnp.pad(v.astype(F32), (0, LANES - v.shape[0]))[None, :]


def _layer(x2, batch, seq, w_in, layer, pre_mix_gain, conv_w, conv_b, dt_bias, a_log, d_skip,
           ssd_norm_gain, sb_norm_gain, w_out, post_mix_gain, pre_ffn_gain, w_gate, w_up,
           w_down, post_ffn_gain):
    utri = jnp.triu(jnp.ones((T_SSD, T_SSD), BF16))
    a_row = _pad_lanes(-jnp.exp(a_log.astype(F32)))
    q, k, v, y_ssd = _proj_ssd(x2, pre_mix_gain[None, :], jnp.swapaxes(w_in, 1, 2), layer,
                               _pad_lanes(dt_bias), conv_w, conv_b[None, :], a_row,
                               _expand_heads(d_skip), ssd_norm_gain[None, :], utri, seq)

    jj = jnp.arange(T_ATT)
    later = (jj[:, None] >= jj[None, :]).astype(BF16)
    half = jnp.concatenate([later, jnp.ones((T_ATT, T_ATT), BF16)], axis=1)
    w2 = jnp.concatenate([half, half], axis=0)
    y_sb, wo, wg, wu, wd = _attention(q, k, v, w2, sb_norm_gain[None, :],
                                      (w_out, w_gate, w_up, w_down), batch, seq)

    return _out_ffn(x2, y_ssd, y_sb, wo, post_mix_gain[None, :], pre_ffn_gain[None, :], wg, wu,
                    wd, post_ffn_gain[None, :])


def kernel(x, pre_mix_gain, w_in, conv_w, conv_b, dt_bias, a_log, d_skip, ssd_norm_gain,
           sb_norm_gain, w_out, post_mix_gain, pre_ffn_gain, w_gate, w_up, w_down,
           post_ffn_gain):
    batch, seq, d = x.shape
    x2 = x.reshape(batch * seq, d)
    params = (pre_mix_gain, conv_w, conv_b, dt_bias, a_log, d_skip, ssd_norm_gain,
              sb_norm_gain, w_out, post_mix_gain, pre_ffn_gain, w_gate, w_up, w_down,
              post_ffn_gain)
    for layer in range(pre_mix_gain.shape[0]):
        x2 = _layer(x2, batch, seq, w_in, layer, *(p[layer] for p in params))
    return x2.reshape(batch, seq, d)
```

```python
import functools
import itertools
import math

import jax
import jax.numpy as jnp
from jax import lax
from jax.experimental import pallas as pl
from jax.experimental.pallas import tpu as pltpu

F32 = jnp.float32
BF16 = jnp.bfloat16

EPS = 1e-6
LANES = 128
BF16_SUBLANES = 16

D_MODEL = 1024
N_HEADS = 8
HEAD_DIM = 64
D_HEADS = N_HEADS * HEAD_DIM
N_PAIRS = N_HEADS // 2
SSD_GROUPS = 2
SSD_STATE = 128
CONV_WIDTH = 4
D_CONV = D_HEADS + 2 * SSD_GROUPS * SSD_STATE
D_FF = 2816

C_Z = 0
C_XBC = C_Z + D_HEADS
C_DT = C_XBC + D_CONV
C_Q = C_DT + LANES
C_K = C_Q + D_HEADS
C_V = C_K + D_HEADS
C_END = C_V + D_HEADS

TM_PROJ = 512
PROJ_PIECE = 256
T_SSD = 128
SSD_SEGMENTS = 2 + N_PAIRS
T_ATT = 128
TM_FFN = 512
SUB_FFN = 256
FF_CHUNK = 256
CONV_TAIL = 8

VMEM_LIMIT = 56 * 1024 * 1024


def _dot(a, b):
    return jnp.dot(a, b, preferred_element_type=F32)


def _dot_nt(a, b):
    return lax.dot_general(a, b, (((1,), (1,)), ((), ())), preferred_element_type=F32)


def _dot_tn(a, b):
    return lax.dot_general(a, b, (((0,), (0,)), ((), ())), preferred_element_type=F32)


def _split3(x):
    hi = x.astype(BF16)
    r = x - hi.astype(F32)
    mid = r.astype(BF16)
    lo = (r - mid.astype(F32)).astype(BF16)
    return hi, mid, lo


def _silu(x):
    return x / (1.0 + jnp.exp(-x))


def _softplus(x):
    e = jnp.exp(-jnp.abs(x))
    u = 1.0 + e
    tiny = u == 1.0
    log1p_e = jnp.where(tiny, e, jnp.log(u) * (e / jnp.where(tiny, 1.0, u - 1.0)))
    return jnp.maximum(x, 0.0) + log1p_e


def _projection_pieces(x_ref, g_ref, w_ref, dtb_ref, h_ref, sz_ref, xbc_ref, dt_ref, q_ref,
                       k_ref, v_ref):
    x = x_ref[...]
    ms = jnp.mean(x * x, axis=-1, keepdims=True)
    h_ref[...] = (x * lax.rsqrt(ms + EPS) * g_ref[...]).astype(BF16)
    to_bf16 = lambda y: y.astype(BF16)
    segments = [(sz_ref, C_Z, C_XBC, _silu), (xbc_ref, C_XBC, C_DT, lambda y: y),
                (dt_ref, C_DT, C_Q, lambda y: _softplus(y + dtb_ref[...])),
                (q_ref, C_Q, C_K, to_bf16), (k_ref, C_K, C_V, to_bf16),
                (v_ref, C_V, C_END, to_bf16)]

    def piece(out_ref, c0, lo, hi, post):
        def run():
            out_ref[:, lo:hi] = post(_dot(h_ref[...], w_ref[:, c0 + lo:c0 + hi]))
        return run

    return [piece(out_ref, c0, lo, min(lo + PROJ_PIECE, c1 - c0), post)
            for out_ref, c0, c1, post in segments for lo in range(0, c1 - c0, PROJ_PIECE)]


def _prepare_weight(wt_ref, w_ref):
    o_dt = D_HEADS + D_CONV
    o_q = o_dt + N_HEADS
    scale = 1.0 / math.sqrt(HEAD_DIM)
    lane = lax.broadcasted_iota(jnp.int32, (D_MODEL, LANES), 1)

    def block(row0):
        return wt_ref[row0:row0 + LANES, :].T

    for c in range(0, o_dt, LANES):
        w_ref[:, C_Z + c:C_Z + c + LANES] = block(c).astype(BF16)
    w_ref[:, C_DT:C_Q] = jnp.where(lane < N_HEADS, block(o_dt), 0.0).astype(BF16)
    for c in range(0, D_HEADS, LANES):
        w_ref[:, C_Q + c:C_Q + c + LANES] = (block(o_q + c) * scale).astype(BF16)
    for c in range(D_HEADS, 3 * D_HEADS, LANES):
        w_ref[:, C_Q + c:C_Q + c + LANES] = block(o_q + c).astype(BF16)


def _proj_ssd_kernel(blocks_per_seq, x_ref, g_ref, wt_ref, dtb_ref, cw_ref, cb_ref, a_ref,
                     dskip_ref, gain_ref, utri_ref, q_ref, k_ref, v_ref, y_ref,
                     w_ref, h_ref, sz_ref, xbc_ref, dt_ref, ext_ref, state_ref):
    g = pl.program_id(0)

    @pl.when(g == 0)
    def _():
        _prepare_weight(wt_ref, w_ref)
        sz_ref[1] = jnp.zeros(sz_ref.shape[1:], F32)
        xbc_ref[1] = jnp.zeros(xbc_ref.shape[1:], F32)
        dt_ref[1] = jnp.zeros(dt_ref.shape[1:], F32)

    @pl.when(lax.rem(g + blocks_per_seq - 1, blocks_per_seq) == 0)
    def _():
        ext_ref[0:CONV_TAIL, :] = jnp.zeros((CONV_TAIL, D_CONV), F32)
        state_ref[...] = jnp.zeros(state_ref.shape, F32)

    def step(new, old):
        pieces = _projection_pieces(x_ref, g_ref, w_ref, dtb_ref, h_ref, sz_ref.at[new],
                                    xbc_ref.at[new], dt_ref.at[new], q_ref, k_ref, v_ref)
        chunks = [pl.ds(c * T_SSD, T_SSD) for c in range(TM_PROJ // T_SSD)]
        segments = itertools.chain.from_iterable(
            _ssd_chunk(sz_ref.at[old, rows], xbc_ref.at[old, rows], dt_ref.at[old, rows],
                       cw_ref, cb_ref, a_ref, dskip_ref, gain_ref, utri_ref, y_ref.at[rows],
                       ext_ref, state_ref) for rows in chunks)
        n_segments = len(chunks) * SSD_SEGMENTS
        assert len(pieces) <= n_segments
        slot = {(n * n_segments) // len(pieces): run for n, run in enumerate(pieces)}
        for s in range(n_segments):
            if s in slot:
                slot[s]()
            next(segments)

    @pl.when(lax.rem(g, 2) == 0)
    def _():
        step(0, 1)

    @pl.when(lax.rem(g, 2) == 1)
    def _():
        step(1, 0)


def _proj_ssd(x2, gain, w_in_t, layer, dtb, conv_w, conv_b, a_row, dskip, ssd_gain, utri, seq):
    m = x2.shape[0]
    n_blocks = m // TM_PROJ
    assert seq % TM_PROJ == 0 and TM_PROJ % T_SSD == 0
    assert w_in_t.shape[1:] == (C_END - (LANES - N_HEADS), D_MODEL)
    cur = lambda g: (jnp.minimum(g, n_blocks - 1), 0)
    prev = lambda g: (jnp.maximum(g - 1, 0), 0)
    const = lambda g: (0, 0)
    qkv_spec = pl.BlockSpec((TM_PROJ, D_HEADS), cur)
    qkv_shape = jax.ShapeDtypeStruct((m, D_HEADS), BF16)
    return pl.pallas_call(
        functools.partial(_proj_ssd_kernel, seq // TM_PROJ),
        grid=(n_blocks + 1,),
        in_specs=[
            pl.BlockSpec((TM_PROJ, D_MODEL), cur),
            pl.BlockSpec((1, D_MODEL), const),
            pl.BlockSpec((None,) + w_in_t.shape[1:], lambda g: (layer, 0, 0),
                         pipeline_mode=pl.Buffered(1)),
            pl.BlockSpec((1, LANES), const),
            pl.BlockSpec((CONV_WIDTH, D_CONV), const),
            pl.BlockSpec((1, D_CONV), const),
            pl.BlockSpec((1, LANES), const),
            pl.BlockSpec((1, D_HEADS), const),
            pl.BlockSpec((1, D_HEADS), const),
            pl.BlockSpec((T_SSD, T_SSD), const),
        ],
        out_specs=[qkv_spec, qkv_spec, qkv_spec, pl.BlockSpec((TM_PROJ, D_HEADS), prev)],
        out_shape=[qkv_shape, qkv_shape, qkv_shape, qkv_shape],
        scratch_shapes=[
            pltpu.VMEM((D_MODEL, C_END), BF16),
            pltpu.VMEM((TM_PROJ, D_MODEL), BF16),
            pltpu.VMEM((2, TM_PROJ, D_HEADS), F32),
            pltpu.VMEM((2, TM_PROJ, D_CONV), F32),
            pltpu.VMEM((2, TM_PROJ, LANES), F32),
            pltpu.VMEM((T_SSD + CONV_TAIL, D_CONV), F32),
            pltpu.VMEM((N_PAIRS, SSD_STATE, LANES), F32),
        ],
        compiler_params=pltpu.CompilerParams(
            dimension_semantics=("arbitrary",), vmem_limit_bytes=VMEM_LIMIT),
        name="proj_ssd",
    )(x2, gain, w_in_t, dtb, conv_w, conv_b, a_row, dskip, ssd_gain, utri)


def _ssd_chunk(sz_ref, xbc_ref, dt_ref, cw_ref, cb_ref, a_ref, dskip_ref, gain_ref, utri_ref,
               y_ref, ext_ref, state_ref):
    t = T_SSD

    ext_ref[CONV_TAIL:CONV_TAIL + t, :] = xbc_ref[...]
    conv = cb_ref[...]
    for k in range(CONV_WIDTH):
        off = CONV_TAIL - (CONV_WIDTH - 1) + k
        conv = conv + ext_ref[off:off + t, :] * cw_ref[k:k + 1, :]
    ext_ref[0:CONV_TAIL, :] = ext_ref[t:t + CONV_TAIL, :]
    xa = _silu(conv)
    yield

    dtv = dt_ref[...]
    adt = dtv * a_ref[...]
    utri = utri_ref[...]
    acs_t = sum(_dot(part, utri) for part in _split3(adt.T))
    acs = acs_t.T

    lane = lax.broadcasted_iota(jnp.int32, (t, LANES), 1)
    first_head = lane < HEAD_DIM
    li = lax.broadcasted_iota(jnp.int32, (t, t), 0)
    si = lax.broadcasted_iota(jnp.int32, (t, t), 1)
    causal = li >= si

    def col(v, h):
        return jnp.broadcast_to(v[:, h:h + 1], (t, LANES))

    y_blocks = []
    cb_mats = []
    for g in range(SSD_GROUPS):
        bm = xa[:, D_HEADS + g * SSD_STATE:D_HEADS + (g + 1) * SSD_STATE].astype(BF16)
        cm = xa[:, D_HEADS + (SSD_GROUPS + g) * SSD_STATE:
                D_HEADS + (SSD_GROUPS + g + 1) * SSD_STATE].astype(BF16)
        cb_mats.append((bm, cm, _dot_nt(cm, bm)))
    yield

    for p in range(N_PAIRS):
        ha, hb = 2 * p, 2 * p + 1
        bm, cm, cbm = cb_mats[p // (N_PAIRS // SSD_GROUPS)]
        x2 = xa[:, p * LANES:(p + 1) * LANES]
        dt2 = jnp.where(first_head, col(dtv, ha), col(dtv, hb))
        acs2 = jnp.where(first_head, col(acs, ha), col(acs, hb))
        xdt2 = x2 * dt2

        def decay(h):
            seg = col(acs, h) - jnp.broadcast_to(acs_t[h:h + 1, :], (t, t))
            return (cbm * jnp.exp(jnp.where(causal, seg, -jnp.inf))).astype(BF16)

        m2 = jnp.concatenate([decay(ha), decay(hb)], axis=1)
        xdt_a = jnp.where(first_head, xdt2, 0.0).astype(BF16)
        xdt_b = jnp.where(first_head, 0.0, xdt2).astype(BF16)
        y_diag = _dot(m2, jnp.concatenate([xdt_a, xdt_b], axis=0))

        prev = state_ref[p]
        y_off = _dot(cm, prev.astype(BF16)) * jnp.exp(acs2)
        last = acs2[t - 1:t, :]
        xs = (xdt2 * jnp.exp(last - acs2)).astype(BF16)
        state_ref[p] = prev * jnp.exp(last) + _dot_tn(bm, xs)

        y_blocks.append(y_diag + y_off + dskip_ref[:, p * LANES:(p + 1) * LANES] * x2)
        if p + 1 < N_PAIRS:
            yield

    per_group = N_PAIRS // SSD_GROUPS
    for g in range(SSD_GROUPS):
        ys = []
        for p in range(g * per_group, (g + 1) * per_group):
            ys.append(y_blocks[p] * sz_ref[:, p * LANES:(p + 1) * LANES])
        ss = sum(jnp.sum(y * y, axis=-1, keepdims=True) for y in ys)
        inv = lax.rsqrt(ss * (1.0 / (per_group * LANES)) + EPS)
        for j, y in enumerate(ys):
            p = g * per_group + j
            y_ref[:, p * LANES:(p + 1) * LANES] = (
                y * inv * gain_ref[:, p * LANES:(p + 1) * LANES]).astype(BF16)
    yield


LOG2E = 1.4426950408889634
ATT_DEPTH = 4
ATT_NEAR_DIAGONALS = 3
ATT_UNDERFLOW_LOG2 = -160.0


def _attn_kernel(n_cast, ni_ref, nj_ref, fi_ref, fj_ref, q_ref, k_ref, v_ref, w2_ref, gain_ref,
                 *refs):
    cast_in, (o_ref, *cast_out) = refs[:n_cast], refs[n_cast:2 * n_cast + 1]
    q2_ref, kst_ref, vst_ref, r_ref, acc_ref, z_ref, arg_ref = refs[2 * n_cast + 1:]
    for src_ref, dst_ref in zip(cast_in, cast_out, strict=True):
        dst_ref[...] = src_ref[...].astype(BF16)

    t = T_ATT
    n_blocks = q2_ref.shape[0] - 1
    depth = ATT_DEPTH

    lane_v = lax.broadcasted_iota(jnp.int32, (t, LANES), 1)
    first_v = lane_v < HEAD_DIM

    for j in range(n_blocks):
        kb = k_ref[j * t:(j + 1) * t, :]
        vb = v_ref[j * t:(j + 1) * t, :]
        zero = jnp.zeros_like(kb)
        q2_ref[j] = q_ref[j * t:(j + 1) * t, :]
        kst_ref[j, 0:t, :] = jnp.where(first_v, kb, zero)
        kst_ref[j, t:2 * t, :] = jnp.where(first_v, zero, kb)
        vst_ref[j, 0:t, :] = jnp.where(first_v, vb, zero)
        vst_ref[j, t:2 * t, :] = jnp.where(first_v, zero, vb)
    q2_ref[n_blocks] = jnp.zeros(q2_ref.shape[1:], BF16)
    r_ref[n_blocks] = jnp.zeros(r_ref.shape[1:], F32)
    acc_ref[n_blocks] = jnp.zeros(acc_ref.shape[1:], F32)

    ti = lax.broadcasted_iota(jnp.int32, (t, 2 * t), 0)
    si = lax.broadcasted_iota(jnp.int32, (t, 2 * t), 1)
    strict = jnp.where(si >= t, si - t, si) < ti

    def stage_scores(tiles, z_buf):
        for u, (i, j) in enumerate(tiles):
            z_buf[u] = _dot_nt(q2_ref[i], kst_ref[j])

    def stage_suffix_sums(tiles, z_buf, diag):
        for u, (i, _) in enumerate(tiles):
            zn = z_buf[u] * (-LOG2E)
            l1 = jnp.minimum(zn, 0.0) - jnp.log2(1.0 + jnp.exp2(-jnp.abs(zn)))
            if diag:
                l1 = jnp.where(strict, l1, 0.0)
            hi = l1.astype(BF16)
            lo = (l1 - hi.astype(F32)).astype(BF16)
            res = [_dot(jnp.concatenate([hi[:, h * t:(h + 1) * t], lo[:, h * t:(h + 1) * t]],
                                        axis=1), w2_ref[...]) for h in range(2)]
            arg = jnp.concatenate([r[:, 0:t] for r in res], axis=1) - zn
            tot = jnp.concatenate([r[:, t:2 * t] for r in res], axis=1)
            if diag:
                arg = jnp.where(strict, arg, -jnp.inf)
            else:
                r_old = r_ref[i]
                arg = arg + r_old
                tot = tot + r_old
            arg_ref[u] = arg
            r_ref[i] = tot

    def stage_values(tiles, diag):
        for u, (i, j) in enumerate(tiles):
            contrib = _dot(jnp.exp2(arg_ref[u]).astype(BF16), vst_ref[j])
            if diag:
                acc_ref[i] = contrib
            else:
                acc_ref[i] += contrib

    def sweep(i_ref, j_ref, diagonals, n_diag_groups):
        n_groups = i_ref.shape[0] // depth
        n_real = sum(n_blocks - d for d in diagonals)
        n_static = n_diag_groups + 2
        assert n_static % 2 == 0 and n_groups % 2 == 0 and n_groups >= n_static

        def group(g, static):
            count = min(depth, n_real - g * depth) if static else depth
            return [(i_ref[g * depth + u], j_ref[g * depth + u]) for u in range(count)]

        def iteration(m, parity, static):
            if not static or m < n_groups:
                stage_scores(group(m, static), z_ref.at[parity])
            if not static or 0 <= m - 2 < n_groups:
                stage_values(group(m - 2, static), static and m - 2 < n_diag_groups)
            if not static or 0 <= m - 1 < n_groups:
                stage_suffix_sums(group(m - 1, static), z_ref.at[1 - parity],
                                  static and m - 1 < n_diag_groups)

        for m in range(n_static):
            iteration(m, m % 2, True)

        def body(mm, carry):
            iteration(2 * mm, 0, False)
            iteration(2 * mm + 1, 1, False)
            return carry

        lax.fori_loop(n_static // 2, n_groups // 2, body, 0)
        for m in (n_groups, n_groups + 1):
            iteration(m, m % 2, True)

    sweep(ni_ref, nj_ref, range(ATT_NEAR_DIAGONALS), pl.cdiv(n_blocks, depth))

    far_blocks = range(ATT_NEAR_DIAGONALS, n_blocks)
    r_max = functools.reduce(jnp.maximum, [r_ref[i] for i in far_blocks])

    @pl.when(jnp.max(r_max) >= ATT_UNDERFLOW_LOG2)
    def _():
        sweep(fi_ref, fj_ref, range(ATT_NEAR_DIAGONALS, n_blocks), 0)

    gain = gain_ref[...]
    for i in range(n_blocks):
        o = acc_ref[i]
        o2 = o * o
        ss_a = jnp.sum(jnp.where(first_v, o2, 0.0), axis=-1, keepdims=True)
        ss_b = jnp.sum(jnp.where(first_v, 0.0, o2), axis=-1, keepdims=True)
        ms = jnp.where(first_v, ss_a, ss_b) * (1.0 / HEAD_DIM)
        o_ref[i * t:(i + 1) * t, :] = (o * lax.rsqrt(ms + EPS) * gain).astype(BF16)


def _rms(x, g):
    ms = jnp.mean(x * x, axis=-1, keepdims=True)
    return x * lax.rsqrt(ms + EPS) * g


def _attention(q, k, v, w2, gain, to_cast, batch, seq):
    n_q = seq // T_ATT
    assert n_q % ATT_DEPTH == 0
    n_steps = batch * N_PAIRS

    def slab_spec(w):
        rows = w.shape[0]
        slab = next(s for s in range(BF16_SUBLANES, rows + 1, BF16_SUBLANES)
                    if rows % s == 0 and rows // s <= n_steps)
        last = rows // slab - 1
        return pl.BlockSpec((slab, w.shape[1]),
                            lambda b, p, *_: (jnp.minimum(b * N_PAIRS + p, last), 0))

    cast_specs = [slab_spec(w) for w in to_cast]

    def tile_list(diagonals):
        tiles = [(i, i - d) for d in diagonals for i in range(d, n_q)]
        tiles += [(n_q, 0)] * (-len(tiles) % (2 * ATT_DEPTH))
        return (jnp.asarray([i for i, _ in tiles], jnp.int32),
                jnp.asarray([j for _, j in tiles], jnp.int32))

    near = tile_list(range(ATT_NEAR_DIAGONALS))
    far = tile_list(range(ATT_NEAR_DIAGONALS, n_q))
    seq_blk = lambda b, p, *_: (b, p)
    return pl.pallas_call(
        functools.partial(_attn_kernel, len(to_cast)),
        grid_spec=pltpu.PrefetchScalarGridSpec(
            num_scalar_prefetch=4,
            grid=(batch, N_PAIRS),
            in_specs=[
                pl.BlockSpec((seq, LANES), seq_blk),
                pl.BlockSpec((seq, LANES), seq_blk),
                pl.BlockSpec((seq, LANES), seq_blk),
                pl.BlockSpec((2 * T_ATT, 2 * T_ATT), lambda b, p, *_: (0, 0)),
                pl.BlockSpec((1, LANES), lambda b, p, *_: (0, p)),
            ] + cast_specs,
            out_specs=[pl.BlockSpec((seq, LANES), seq_blk)] + cast_specs,
            scratch_shapes=[
                pltpu.VMEM((n_q + 1, T_ATT, LANES), BF16),
                pltpu.VMEM((n_q, 2 * T_ATT, LANES), BF16),
                pltpu.VMEM((n_q, 2 * T_ATT, LANES), BF16),
                pltpu.VMEM((n_q + 1, T_ATT, 2 * T_ATT), F32),
                pltpu.VMEM((n_q + 1, T_ATT, LANES), F32),
                pltpu.VMEM((2, ATT_DEPTH, T_ATT, 2 * T_ATT), F32),
                pltpu.VMEM((ATT_DEPTH, T_ATT, 2 * T_ATT), F32),
            ],
        ),
        out_shape=[jax.ShapeDtypeStruct((batch * seq, D_HEADS), BF16)]
        + [jax.ShapeDtypeStruct(w.shape, BF16) for w in to_cast],
        compiler_params=pltpu.CompilerParams(
            dimension_semantics=("arbitrary", "arbitrary"),
            vmem_limit_bytes=VMEM_LIMIT),
        name="sb_attention",
    )(*near, *far, q, k, v, w2, gain, *to_cast)


def _ffn_kernel(x_ref, yssd_ref, ysb_ref, wo_ref, g_post_ref, g_pre_ref, wg_ref, wu_ref,
                wd_ref, g_out_ref, o_ref, x1_ref, h_ref, act_ref):
    chunks = [slice(c * FF_CHUNK, (c + 1) * FF_CHUNK) for c in range(D_FF // FF_CHUNK)]
    subs = [pl.ds(s * SUB_FFN, SUB_FFN) for s in range(TM_FFN // SUB_FFN)]
    for rows in subs:
        mix = (_dot(yssd_ref[rows, :], wo_ref[0:D_HEADS, :])
               + _dot(ysb_ref[rows, :], wo_ref[D_HEADS:2 * D_HEADS, :]))
        x1 = x_ref[rows, :] + _rms(mix, g_post_ref[...])
        x1_ref[rows, :] = x1
        h_ref[rows, :] = _rms(x1, g_pre_ref[...]).astype(BF16)
    for rows in subs:
        for cols in chunks:
            h = h_ref[rows, :]
            gate = _dot(h, wg_ref[:, cols])
            up = _dot(h, wu_ref[:, cols])
            act_ref[rows, cols] = (_silu(gate) * up).astype(BF16)
    for rows in subs:
        f = _dot(act_ref[rows, :], wd_ref[...])
        o_ref[rows, :] = x1_ref[rows, :] + _rms(f, g_out_ref[...])


def _out_ffn(x2, y_ssd, y_sb, w_out, g_post, g_pre, wg, wu, wd, g_out):
    m = x2.shape[0]
    row = lambda i: (i, 0)
    const2 = lambda i: (0, 0)
    single = pl.Buffered(1)
    return pl.pallas_call(
        _ffn_kernel,
        grid=(m // TM_FFN,),
        in_specs=[
            pl.BlockSpec((TM_FFN, D_MODEL), row),
            pl.BlockSpec((TM_FFN, D_HEADS), row),
            pl.BlockSpec((TM_FFN, D_HEADS), row),
            pl.BlockSpec((2 * D_HEADS, D_MODEL), const2, pipeline_mode=single),
            pl.BlockSpec((1, D_MODEL), const2),
            pl.BlockSpec((1, D_MODEL), const2),
            pl.BlockSpec((D_MODEL, D_FF), const2, pipeline_mode=single),
            pl.BlockSpec((D_MODEL, D_FF), const2, pipeline_mode=single),
            pl.BlockSpec((D_FF, D_MODEL), const2, pipeline_mode=single),
            pl.BlockSpec((1, D_MODEL), const2),
        ],
        out_specs=pl.BlockSpec((TM_FFN, D_MODEL), row),
        out_shape=jax.ShapeDtypeStruct((m, D_MODEL), F32),
        scratch_shapes=[
            pltpu.VMEM((TM_FFN, D_MODEL), F32),
            pltpu.VMEM((TM_FFN, D_MODEL), BF16),
            pltpu.VMEM((TM_FFN, D_FF), BF16),
        ],
        compiler_params=pltpu.CompilerParams(
            dimension_semantics=("arbitrary",), vmem_limit_bytes=VMEM_LIMIT),
        name="out_ffn",
    )(x2, y_ssd, y_sb, w_out, g_post, g_pre, wg, wu, wd, g_out)


def _expand_heads(v):
    return jnp.repeat(v.astype(F32), HEAD_DIM)[None, :]


def _pad_lanes(v):
    return jnp.pad(v.astype(F32), (0, LANES - v.shape[0]))[None, :]


def _layer(x2, batch, seq, w_in, layer, pre_mix_gain, conv_w, conv_b, dt_bias, a_log, d_skip,
           ssd_norm_gain, sb_norm_gain, w_out, post_mix_gain, pre_ffn_gain, w_gate, w_up,
           w_down, post_ffn_gain):
    utri = jnp.triu(jnp.ones((T_SSD, T_SSD), BF16))
    a_row = _pad_lanes(-jnp.exp(a_log.astype(F32)))
    q, k, v, y_ssd = _proj_ssd(x2, pre_mix_gain[None, :], jnp.swapaxes(w_in, 1, 2), layer,
                               _pad_lanes(dt_bias), conv_w, conv_b[None, :], a_row,
                               _expand_heads(d_skip), ssd_norm_gain[None, :], utri, seq)

    jj = jnp.arange(T_ATT)
    later = (jj[:, None] >= jj[None, :]).astype(BF16)
    half = jnp.concatenate([later, jnp.ones((T_ATT, T_ATT), BF16)], axis=1)
    w2 = jnp.concatenate([half, half], axis=0)
    y_sb, wo, wg, wu, wd = _attention(q, k, v, w2, sb_norm_gain[None, :],
                                      (w_out, w_gate, w_up, w_down), batch, seq)

    return _out_ffn(x2, y_ssd, y_sb, wo, post_mix_gain[None, :], pre_ffn_gain[None, :], wg, wu,
                    wd, post_ffn_gain[None, :])


def kernel(x, pre_mix_gain, w_in, conv_w, conv_b, dt_bias, a_log, d_skip, ssd_norm_gain,
           sb_norm_gain, w_out, post_mix_gain, pre_ffn_gain, w_gate, w_up, w_down,
           post_ffn_gain):
    batch, seq, d = x.shape
    x2 = x.reshape(batch * seq, d)
    params = (pre_mix_gain, conv_w, conv_b, dt_bias, a_log, d_skip, ssd_norm_gain,
              sb_norm_gain, w_out, post_mix_gain, pre_ffn_gain, w_gate, w_up, w_down,
              post_ffn_gain)
    for layer in range(pre_mix_gain.shape[0]):
        x2 = _layer(x2, batch, seq, w_in, layer, *(p[layer] for p in params))
    return x2.reshape(batch, seq, d)
```

```python
import functools
import itertools
import math

import jax
import jax.numpy as jnp
from jax import lax
from jax.experimental import pallas as pl
from jax.experimental.pallas import tpu as pltpu

F32 = jnp.float32
BF16 = jnp.bfloat16

EPS = 1e-6
LANES = 128
BF16_SUBLANES = 16

D_MODEL = 1024
N_HEADS = 8
HEAD_DIM = 64
D_HEADS = N_HEADS * HEAD_DIM
N_PAIRS = N_HEADS // 2
SSD_GROUPS = 2
SSD_STATE = 128
CONV_WIDTH = 4
D_CONV = D_HEADS + 2 * SSD_GROUPS * SSD_STATE
D_FF = 2816

C_Z = 0
C_XBC = C_Z + D_HEADS
C_DT = C_XBC + D_CONV
C_Q = C_DT + LANES
C_K = C_Q + D_HEADS
C_V = C_K + D_HEADS
C_END = C_V + D_HEADS

TM_PROJ = 512
PROJ_PIECE = 256
T_SSD = 128
SSD_SEGMENTS = 2 + N_PAIRS
T_ATT = 128
TM_FFN = 512
SUB_FFN = 256
FF_CHUNK = 256
CONV_TAIL = 8

VMEM_LIMIT = 56 * 1024 * 1024


def _dot(a, b):
    return jnp.dot(a, b, preferred_element_type=F32)


def _dot_nt(a, b):
    return lax.dot_general(a, b, (((1,), (1,)), ((), ())), preferred_element_type=F32)


def _dot_tn(a, b):
    return lax.dot_general(a, b, (((0,), (0,)), ((), ())), preferred_element_type=F32)


def _split3(x):
    hi = x.astype(BF16)
    r = x - hi.astype(F32)
    mid = r.astype(BF16)
    lo = (r - mid.astype(F32)).astype(BF16)
    return hi, mid, lo


def _silu(x):
    return x / (1.0 + jnp.exp(-x))


def _softplus(x):
    e = jnp.exp(-jnp.abs(x))
    u = 1.0 + e
    tiny = u == 1.0
    log1p_e = jnp.where(tiny, e, jnp.log(u) * (e / jnp.where(tiny, 1.0, u - 1.0)))
    return jnp.maximum(x, 0.0) + log1p_e


def _projection_pieces(x_ref, g_ref, w_ref, dtb_ref, h_ref, sz_ref, xbc_ref, dt_ref, q_ref,
                       k_ref, v_ref):
    x = x_ref[...]
    ms = jnp.mean(x * x, axis=-1, keepdims=True)
    h_ref[...] = (x * lax.rsqrt(ms + EPS) * g_ref[...]).astype(BF16)
    to_bf16 = lambda y: y.astype(BF16)
    segments = [(sz_ref, C_Z, C_XBC, _silu), (xbc_ref, C_XBC, C_DT, lambda y: y),
                (dt_ref, C_DT, C_Q, lambda y: _softplus(y + dtb_ref[...])),
                (q_ref, C_Q, C_K, to_bf16), (k_ref, C_K, C_V, to_bf16),
                (v_ref, C_V, C_END, to_bf16)]

    def piece(out_ref, c0, lo, hi, post):
        def run():
            out_ref[:, lo:hi] = post(_dot(h_ref[...], w_ref[:, c0 + lo:c0 + hi]))
        return run

    return [piece(out_ref, c0, lo, min(lo + PROJ_PIECE, c1 - c0), post)
            for out_ref, c0, c1, post in segments for lo in range(0, c1 - c0, PROJ_PIECE)]


def _prepare_weight(wt_ref, w_ref):
    o_dt = D_HEADS + D_CONV
    o_q = o_dt + N_HEADS
    scale = 1.0 / math.sqrt(HEAD_DIM)
    lane = lax.broadcasted_iota(jnp.int32, (D_MODEL, LANES), 1)

    def block(row0):
        return wt_ref[row0:row0 + LANES, :].T

    for c in range(0, o_dt, LANES):
        w_ref[:, C_Z + c:C_Z + c + LANES] = block(c).astype(BF16)
    w_ref[:, C_DT:C_Q] = jnp.where(lane < N_HEADS, block(o_dt), 0.0).astype(BF16)
    for c in range(0, D_HEADS, LANES):
        w_ref[:, C_Q + c:C_Q + c + LANES] = (block(o_q + c) * scale).astype(BF16)
    for c in range(D_HEADS, 3 * D_HEADS, LANES):
        w_ref[:, C_Q + c:C_Q + c + LANES] = block(o_q + c).astype(BF16)


def _proj_ssd_kernel(blocks_per_seq, n_blocks, x_ref, g_ref, wt_ref, dtb_ref, cw_ref, cb_ref,
                     a_ref, dskip_ref, gain_ref, utri_ref, q_ref, k_ref, v_ref, y_ref,
                     w_ref, h_ref, sz_ref, xbc_ref, dt_ref, ext_ref, state_ref):
    g = pl.program_id(0)

    @pl.when(g == 0)
    def _():
        _prepare_weight(wt_ref, w_ref)

    @pl.when(lax.rem(g + blocks_per_seq - 1, blocks_per_seq) == 0)
    def _():
        ext_ref[0:CONV_TAIL, :] = jnp.zeros((CONV_TAIL, D_CONV), F32)
        state_ref[...] = jnp.zeros(state_ref.shape, F32)

    def step(new, old, project=True, scan=True):
        pieces = _projection_pieces(
            x_ref, g_ref, w_ref, dtb_ref, h_ref, sz_ref.at[new], xbc_ref.at[new],
            dt_ref.at[new], q_ref, k_ref, v_ref) if project else []
        chunks = [pl.ds(c * T_SSD, T_SSD) for c in range(TM_PROJ // T_SSD)] if scan else []
        segments = itertools.chain.from_iterable(
            _ssd_chunk(sz_ref.at[old, rows], xbc_ref.at[old, rows], dt_ref.at[old, rows],
                       cw_ref, cb_ref, a_ref, dskip_ref, gain_ref, utri_ref, y_ref.at[rows],
                       ext_ref, state_ref) for rows in chunks)
        if not scan:
            for run in pieces:
                run()
            return
        n_segments = len(chunks) * SSD_SEGMENTS
        assert len(pieces) <= n_segments
        slot = {(n * n_segments) // len(pieces): run for n, run in enumerate(pieces)}
        for s in range(n_segments):
            if s in slot:
                slot[s]()
            next(segments)

    even = lax.rem(g, 2) == 0

    @pl.when(g == 0)
    def _():
        step(0, 1, scan=False)

    @pl.when((g > 0) & (g < n_blocks) & even)
    def _():
        step(0, 1)

    @pl.when((g < n_blocks) & jnp.logical_not(even))
    def _():
        step(1, 0)

    @pl.when(g == n_blocks)
    def _():
        step(n_blocks % 2, 1 - n_blocks % 2, project=False)


def _proj_ssd(x2, gain, w_in_t, layer, dtb, conv_w, conv_b, a_row, dskip, ssd_gain, utri, seq):
    m = x2.shape[0]
    n_blocks = m // TM_PROJ
    assert seq % TM_PROJ == 0 and TM_PROJ % T_SSD == 0
    assert w_in_t.shape[1:] == (C_END - (LANES - N_HEADS), D_MODEL)
    cur = lambda g: (jnp.minimum(g, n_blocks - 1), 0)
    prev = lambda g: (jnp.maximum(g - 1, 0), 0)
    const = lambda g: (0, 0)
    qkv_spec = pl.BlockSpec((TM_PROJ, D_HEADS), cur)
    qkv_shape = jax.ShapeDtypeStruct((m, D_HEADS), BF16)
    return pl.pallas_call(
        functools.partial(_proj_ssd_kernel, seq // TM_PROJ, n_blocks),
        grid=(n_blocks + 1,),
        in_specs=[
            pl.BlockSpec((TM_PROJ, D_MODEL), cur),
            pl.BlockSpec((1, D_MODEL), const),
            pl.BlockSpec((None,) + w_in_t.shape[1:], lambda g: (layer, 0, 0),
                         pipeline_mode=pl.Buffered(1)),
            pl.BlockSpec((1, LANES), const),
            pl.BlockSpec((CONV_WIDTH, D_CONV), const),
            pl.BlockSpec((1, D_CONV), const),
            pl.BlockSpec((1, LANES), const),
            pl.BlockSpec((1, D_HEADS), const),
            pl.BlockSpec((1, D_HEADS), const),
            pl.BlockSpec((T_SSD, T_SSD), const),
        ],
        out_specs=[qkv_spec, qkv_spec, qkv_spec, pl.BlockSpec((TM_PROJ, D_HEADS), prev)],
        out_shape=[qkv_shape, qkv_shape, qkv_shape, qkv_shape],
        scratch_shapes=[
            pltpu.VMEM((D_MODEL, C_END), BF16),
            pltpu.VMEM((TM_PROJ, D_MODEL), BF16),
            pltpu.VMEM((2, TM_PROJ, D_HEADS), F32),
            pltpu.VMEM((2, TM_PROJ, D_CONV), F32),
            pltpu.VMEM((2, TM_PROJ, LANES), F32),
            pltpu.VMEM((T_SSD + CONV_TAIL, D_CONV), F32),
            pltpu.VMEM((N_PAIRS, SSD_STATE, LANES), F32),
        ],
        compiler_params=pltpu.CompilerParams(
            dimension_semantics=("arbitrary",), vmem_limit_bytes=VMEM_LIMIT),
        name="proj_ssd",
    )(x2, gain, w_in_t, dtb, conv_w, conv_b, a_row, dskip, ssd_gain, utri)


def _ssd_chunk(sz_ref, xbc_ref, dt_ref, cw_ref, cb_ref, a_ref, dskip_ref, gain_ref, utri_ref,
               y_ref, ext_ref, state_ref):
    t = T_SSD

    ext_ref[CONV_TAIL:CONV_TAIL + t, :] = xbc_ref[...]
    conv = cb_ref[...]
    for k in range(CONV_WIDTH):
        off = CONV_TAIL - (CONV_WIDTH - 1) + k
        conv = conv + ext_ref[off:off + t, :] * cw_ref[k:k + 1, :]
    ext_ref[0:CONV_TAIL, :] = ext_ref[t:t + CONV_TAIL, :]
    xa = _silu(conv)
    yield

    dtv = dt_ref[...]
    adt = dtv * a_ref[...]
    utri = utri_ref[...]
    acs_t = sum(_dot(part, utri) for part in _split3(adt.T))
    acs = acs_t.T

    lane = lax.broadcasted_iota(jnp.int32, (t, LANES), 1)
    first_head = lane < HEAD_DIM
    li = lax.broadcasted_iota(jnp.int32, (t, t), 0)
    si = lax.broadcasted_iota(jnp.int32, (t, t), 1)
    causal = li >= si

    def col(v, h):
        return jnp.broadcast_to(v[:, h:h + 1], (t, LANES))

    y_blocks = []
    cb_mats = []
    for g in range(SSD_GROUPS):
        bm = xa[:, D_HEADS + g * SSD_STATE:D_HEADS + (g + 1) * SSD_STATE].astype(BF16)
        cm = xa[:, D_HEADS + (SSD_GROUPS + g) * SSD_STATE:
                D_HEADS + (SSD_GROUPS + g + 1) * SSD_STATE].astype(BF16)
        cb_mats.append((bm, cm, _dot_nt(cm, bm)))
    yield

    for p in range(N_PAIRS):
        ha, hb = 2 * p, 2 * p + 1
        bm, cm, cbm = cb_mats[p // (N_PAIRS // SSD_GROUPS)]
        x2 = xa[:, p * LANES:(p + 1) * LANES]
        dt2 = jnp.where(first_head, col(dtv, ha), col(dtv, hb))
        acs2 = jnp.where(first_head, col(acs, ha), col(acs, hb))
        xdt2 = x2 * dt2

        def decay(h):
            seg = col(acs, h) - jnp.broadcast_to(acs_t[h:h + 1, :], (t, t))
            return (cbm * jnp.exp(jnp.where(causal, seg, -jnp.inf))).astype(BF16)

        m2 = jnp.concatenate([decay(ha), decay(hb)], axis=1)
        xdt_a = jnp.where(first_head, xdt2, 0.0).astype(BF16)
        xdt_b = jnp.where(first_head, 0.0, xdt2).astype(BF16)
        y_diag = _dot(m2, jnp.concatenate([xdt_a, xdt_b], axis=0))

        prev = state_ref[p]
        y_off = _dot(cm, prev.astype(BF16)) * jnp.exp(acs2)
        last = acs2[t - 1:t, :]
        xs = (xdt2 * jnp.exp(last - acs2)).astype(BF16)
        state_ref[p] = prev * jnp.exp(last) + _dot_tn(bm, xs)

        y_blocks.append(y_diag + y_off + dskip_ref[:, p * LANES:(p + 1) * LANES] * x2)
        if p + 1 < N_PAIRS:
            yield

    per_group = N_PAIRS // SSD_GROUPS
    for g in range(SSD_GROUPS):
        ys = []
        for p in range(g * per_group, (g + 1) * per_group):
            ys.append(y_blocks[p] * sz_ref[:, p * LANES:(p + 1) * LANES])
        ss = sum(jnp.sum(y * y, axis=-1, keepdims=True) for y in ys)
        inv = lax.rsqrt(ss * (1.0 / (per_group * LANES)) + EPS)
        for j, y in enumerate(ys):
            p = g * per_group + j
            y_ref[:, p * LANES:(p + 1) * LANES] = (
                y * inv * gain_ref[:, p * LANES:(p + 1) * LANES]).astype(BF16)
    yield


LOG2E = 1.4426950408889634
ATT_DEPTH = 4
ATT_NEAR_DIAGONALS = 3
ATT_UNDERFLOW_LOG2 = -160.0


def _attn_kernel(n_cast, ni_ref, nj_ref, fi_ref, fj_ref, q_ref, k_ref, v_ref, w2_ref, gain_ref,
                 *refs):
    cast_in, (o_ref, *cast_out) = refs[:n_cast], refs[n_cast:2 * n_cast + 1]
    q2_ref, kst_ref, vst_ref, r_ref, acc_ref, z_ref, arg_ref = refs[2 * n_cast + 1:]
    for src_ref, dst_ref in zip(cast_in, cast_out, strict=True):
        dst_ref[...] = src_ref[...].astype(BF16)

    t = T_ATT
    n_blocks = q2_ref.shape[0] - 1
    depth = ATT_DEPTH

    lane_v = lax.broadcasted_iota(jnp.int32, (t, LANES), 1)
    first_v = lane_v < HEAD_DIM

    for j in range(n_blocks):
        kb = k_ref[j * t:(j + 1) * t, :]
        vb = v_ref[j * t:(j + 1) * t, :]
        zero = jnp.zeros_like(kb)
        q2_ref[j] = q_ref[j * t:(j + 1) * t, :]
        kst_ref[j, 0:t, :] = jnp.where(first_v, kb, zero)
        kst_ref[j, t:2 * t, :] = jnp.where(first_v, zero, kb)
        vst_ref[j, 0:t, :] = jnp.where(first_v, vb, zero)
        vst_ref[j, t:2 * t, :] = jnp.where(first_v, zero, vb)
    q2_ref[n_blocks] = jnp.zeros(q2_ref.shape[1:], BF16)
    r_ref[n_blocks] = jnp.zeros(r_ref.shape[1:], F32)
    acc_ref[n_blocks] = jnp.zeros(acc_ref.shape[1:], F32)

    ti = lax.broadcasted_iota(jnp.int32, (t, 2 * t), 0)
    si = lax.broadcasted_iota(jnp.int32, (t, 2 * t), 1)
    strict = jnp.where(si >= t, si - t, si) < ti

    def stage_scores(tiles, z_buf):
        for u, (i, j) in enumerate(tiles):
            z_buf[u] = _dot_nt(q2_ref[i], kst_ref[j])

    def stage_suffix_sums(tiles, z_buf, diag):
        for u, (i, _) in enumerate(tiles):
            zn = z_buf[u] * (-LOG2E)
            l1 = jnp.minimum(zn, 0.0) - jnp.log2(1.0 + jnp.exp2(-jnp.abs(zn)))
            if diag:
                l1 = jnp.where(strict, l1, 0.0)
            hi = l1.astype(BF16)
            lo = (l1 - hi.astype(F32)).astype(BF16)
            res = [_dot(jnp.concatenate([hi[:, h * t:(h + 1) * t], lo[:, h * t:(h + 1) * t]],
                                        axis=1), w2_ref[...]) for h in range(2)]
            arg = jnp.concatenate([r[:, 0:t] for r in res], axis=1) - zn
            tot = jnp.concatenate([r[:, t:2 * t] for r in res], axis=1)
            if diag:
                arg = jnp.where(strict, arg, -jnp.inf)
            else:
                r_old = r_ref[i]
                arg = arg + r_old
                tot = tot + r_old
            arg_ref[u] = arg
            r_ref[i] = tot

    def stage_values(tiles, diag):
        for u, (i, j) in enumerate(tiles):
            contrib = _dot(jnp.exp2(arg_ref[u]).astype(BF16), vst_ref[j])
            if diag:
                acc_ref[i] = contrib
            else:
                acc_ref[i] += contrib

    def sweep(i_ref, j_ref, diagonals, n_diag_groups):
        n_groups = i_ref.shape[0] // depth
        n_real = sum(n_blocks - d for d in diagonals)
        n_static = n_diag_groups + 2
        assert n_static % 2 == 0 and n_groups % 2 == 0 and n_groups >= n_static

        def group(g, static):
            count = min(depth, n_real - g * depth) if static else depth
            return [(i_ref[g * depth + u], j_ref[g * depth + u]) for u in range(count)]

        def iteration(m, parity, static):
            if not static or m < n_groups:
                stage_scores(group(m, static), z_ref.at[parity])
            if not static or 0 <= m - 2 < n_groups:
                stage_values(group(m - 2, static), static and m - 2 < n_diag_groups)
            if not static or 0 <= m - 1 < n_groups:
                stage_suffix_sums(group(m - 1, static), z_ref.at[1 - parity],
                                  static and m - 1 < n_diag_groups)

        for m in range(n_static):
            iteration(m, m % 2, True)

        def body(mm, carry):
            iteration(2 * mm, 0, False)
            iteration(2 * mm + 1, 1, False)
            return carry

        lax.fori_loop(n_static // 2, n_groups // 2, body, 0)
        for m in (n_groups, n_groups + 1):
            iteration(m, m % 2, True)

    sweep(ni_ref, nj_ref, range(ATT_NEAR_DIAGONALS), pl.cdiv(n_blocks, depth))

    far_blocks = range(ATT_NEAR_DIAGONALS, n_blocks)
    r_max = functools.reduce(jnp.maximum, [r_ref[i] for i in far_blocks])

    @pl.when(jnp.max(r_max) >= ATT_UNDERFLOW_LOG2)
    def _():
        sweep(fi_ref, fj_ref, range(ATT_NEAR_DIAGONALS, n_blocks), 0)

    gain = gain_ref[...]
    for i in range(n_blocks):
        o = acc_ref[i]
        o2 = o * o
        ss_a = jnp.sum(jnp.where(first_v, o2, 0.0), axis=-1, keepdims=True)
        ss_b = jnp.sum(jnp.where(first_v, 0.0, o2), axis=-1, keepdims=True)
        ms = jnp.where(first_v, ss_a, ss_b) * (1.0 / HEAD_DIM)
        o_ref[i * t:(i + 1) * t, :] = (o * lax.rsqrt(ms + EPS) * gain).astype(BF16)


def _rms(x, g):
    ms = jnp.mean(x * x, axis=-1, keepdims=True)
    return x * lax.rsqrt(ms + EPS) * g


def _attention(q, k, v, w2, gain, to_cast, batch, seq):
    n_q = seq // T_ATT
    assert n_q % ATT_DEPTH == 0
    n_steps = batch * N_PAIRS

    def slab_spec(w):
        rows = w.shape[0]
        slab = next(s for s in range(BF16_SUBLANES, rows + 1, BF16_SUBLANES)
                    if rows % s == 0 and rows // s <= n_steps)
        last = rows // slab - 1
        return pl.BlockSpec((slab, w.shape[1]),
                            lambda b, p, *_: (jnp.minimum(b * N_PAIRS + p, last), 0))

    cast_specs = [slab_spec(w) for w in to_cast]

    def tile_list(diagonals):
        tiles = [(i, i - d) for d in diagonals for i in range(d, n_q)]
        tiles += [(n_q, 0)] * (-len(tiles) % (2 * ATT_DEPTH))
        return (jnp.asarray([i for i, _ in tiles], jnp.int32),
                jnp.asarray([j for _, j in tiles], jnp.int32))

    near = tile_list(range(ATT_NEAR_DIAGONALS))
    far = tile_list(range(ATT_NEAR_DIAGONALS, n_q))
    seq_blk = lambda b, p, *_: (b, p)
    return pl.pallas_call(
        functools.partial(_attn_kernel, len(to_cast)),
        grid_spec=pltpu.PrefetchScalarGridSpec(
            num_scalar_prefetch=4,
            grid=(batch, N_PAIRS),
            in_specs=[
                pl.BlockSpec((seq, LANES), seq_blk),
                pl.BlockSpec((seq, LANES), seq_blk),
                pl.BlockSpec((seq, LANES), seq_blk),
                pl.BlockSpec((2 * T_ATT, 2 * T_ATT), lambda b, p, *_: (0, 0)),
                pl.BlockSpec((1, LANES), lambda b, p, *_: (0, p)),
            ] + cast_specs,
            out_specs=[pl.BlockSpec((seq, LANES), seq_blk)] + cast_specs,
            scratch_shapes=[
                pltpu.VMEM((n_q + 1, T_ATT, LANES), BF16),
                pltpu.VMEM((n_q, 2 * T_ATT, LANES), BF16),
                pltpu.VMEM((n_q, 2 * T_ATT, LANES), BF16),
                pltpu.VMEM((n_q + 1, T_ATT, 2 * T_ATT), F32),
                pltpu.VMEM((n_q + 1, T_ATT, LANES), F32),
                pltpu.VMEM((2, ATT_DEPTH, T_ATT, 2 * T_ATT), F32),
                pltpu.VMEM((ATT_DEPTH, T_ATT, 2 * T_ATT), F32),
            ],
        ),
        out_shape=[jax.ShapeDtypeStruct((batch * seq, D_HEADS), BF16)]
        + [jax.ShapeDtypeStruct(w.shape, BF16) for w in to_cast],
        compiler_params=pltpu.CompilerParams(
            dimension_semantics=("arbitrary", "arbitrary"),
            vmem_limit_bytes=VMEM_LIMIT),
        name="sb_attention",
    )(*near, *far, q, k, v, w2, gain, *to_cast)


def _ffn_kernel(x_ref, yssd_ref, ysb_ref, wo_ref, g_post_ref, g_pre_ref, wg_ref, wu_ref,
                wd_ref, g_out_ref, o_ref, x1_ref, h_ref, act_ref):
    chunks = [slice(c * FF_CHUNK, (c + 1) * FF_CHUNK) for c in range(D_FF // FF_CHUNK)]
    subs = [pl.ds(s * SUB_FFN, SUB_FFN) for s in range(TM_FFN // SUB_FFN)]
    for rows in subs:
        mix = (_dot(yssd_ref[rows, :], wo_ref[0:D_HEADS, :])
               + _dot(ysb_ref[rows, :], wo_ref[D_HEADS:2 * D_HEADS, :]))
        x1 = x_ref[rows, :] + _rms(mix, g_post_ref[...])
        x1_ref[rows, :] = x1
        h_ref[rows, :] = _rms(x1, g_pre_ref[...]).astype(BF16)
    for rows in subs:
        for cols in chunks:
            h = h_ref[rows, :]
            gate = _dot(h, wg_ref[:, cols])
            up = _dot(h, wu_ref[:, cols])
            act_ref[rows, cols] = (_silu(gate) * up).astype(BF16)
    for rows in subs:
        f = _dot(act_ref[rows, :], wd_ref[...])
        o_ref[rows, :] = x1_ref[rows, :] + _rms(f, g_out_ref[...])


def _out_ffn(x2, y_ssd, y_sb, w_out, g_post, g_pre, wg, wu, wd, g_out):
    m = x2.shape[0]
    row = lambda i: (i, 0)
    const2 = lambda i: (0, 0)
    single = pl.Buffered(1)
    return pl.pallas_call(
        _ffn_kernel,
        grid=(m // TM_FFN,),
        in_specs=[
            pl.BlockSpec((TM_FFN, D_MODEL), row),
            pl.BlockSpec((TM_FFN, D_HEADS), row),
            pl.BlockSpec((TM_FFN, D_HEADS), row),
            pl.BlockSpec((2 * D_HEADS, D_MODEL), const2, pipeline_mode=single),
            pl.BlockSpec((1, D_MODEL), const2),
            pl.BlockSpec((1, D_MODEL), const2),
            pl.BlockSpec((D_MODEL, D_FF), const2, pipeline_mode=single),
            pl.BlockSpec((D_MODEL, D_FF), const2, pipeline_mode=single),
            pl.BlockSpec((D_FF, D_MODEL), const2, pipeline_mode=single),
            pl.BlockSpec((1, D_MODEL), const2),
        ],
        out_specs=pl.BlockSpec((TM_FFN, D_MODEL), row),
        out_shape=jax.ShapeDtypeStruct((m, D_MODEL), F32),
        scratch_shapes=[
            pltpu.VMEM((TM_FFN, D_MODEL), F32),
            pltpu.VMEM((TM_FFN, D_MODEL), BF16),
            pltpu.VMEM((TM_FFN, D_FF), BF16),
        ],
        compiler_params=pltpu.CompilerParams(
            dimension_semantics=("arbitrary",), vmem_limit_bytes=VMEM_LIMIT),
        name="out_ffn",
    )(x2, y_ssd, y_sb, w_out, g_post, g_pre, wg, wu, wd, g_out)


def _expand_heads(v):
    return jnp.repeat(v.astype(F32), HEAD_DIM)[None, :]


def _pad_lanes(v):
    return jnp.pad(v.astype(F32), (0, LANES - v.shape[0]))[None, :]


def _layer(x2, batch, seq, w_in, layer, pre_mix_gain, conv_w, conv_b, dt_bias, a_log, d_skip,
           ssd_norm_gain, sb_norm_gain, w_out, post_mix_gain, pre_ffn_gain, w_gate, w_up,
           w_down, post_ffn_gain):
    utri = jnp.triu(jnp.ones((T_SSD, T_SSD), BF16))
    a_row = _pad_lanes(-jnp.exp(a_log.astype(F32)))
    q, k, v, y_ssd = _proj_ssd(x2, pre_mix_gain[None, :], jnp.swapaxes(w_in, 1, 2), layer,
                               _pad_lanes(dt_bias), conv_w, conv_b[None, :], a_row,
                               _expand_heads(d_skip), ssd_norm_gain[None, :], utri, seq)

    jj = jnp.arange(T_ATT)
    later = (jj[:, None] >= jj[None, :]).astype(BF16)
    half = jnp.concatenate([later, jnp.ones((T_ATT, T_ATT), BF16)], axis=1)
    w2 = jnp.concatenate([half, half], axis=0)
    y_sb, wo, wg, wu, wd = _attention(q, k, v, w2, sb_norm_gain[None, :],
                                      (w_out, w_gate, w_up, w_down), batch, seq)

    return _out_ffn(x2, y_ssd, y_sb, wo, post_mix_gain[None, :], pre_ffn_gain[None, :], wg, wu,
                    wd, post_ffn_gain[None, :])


def kernel(x, pre_mix_gain, w_in, conv_w, conv_b, dt_bias, a_log, d_skip, ssd_norm_gain,
           sb_norm_gain, w_out, post_mix_gain, pre_ffn_gain, w_gate, w_up, w_down,
           post_ffn_gain):
    batch, seq, d = x.shape
    x2 = x.reshape(batch * seq, d)
    params = (pre_mix_gain, conv_w, conv_b, dt_bias, a_log, d_skip, ssd_norm_gain,
              sb_norm_gain, w_out, post_mix_gain, pre_ffn_gain, w_gate, w_up, w_down,
              post_ffn_gain)
    for layer in range(pre_mix_gain.shape[0]):
        x2 = _layer(x2, batch, seq, w_in, layer, *(p[layer] for p in params))
    return x2.reshape(batch, seq, d)
```

```python
import functools
import itertools
import math

import jax
import jax.numpy as jnp
from jax import lax
from jax.experimental import pallas as pl
from jax.experimental.pallas import tpu as pltpu

F32 = jnp.float32
BF16 = jnp.bfloat16

EPS = 1e-6
LANES = 128
BF16_SUBLANES = 16

D_MODEL = 1024
N_HEADS = 8
HEAD_DIM = 64
D_HEADS = N_HEADS * HEAD_DIM
N_PAIRS = N_HEADS // 2
SSD_GROUPS = 2
SSD_STATE = 128
CONV_WIDTH = 4
D_CONV = D_HEADS + 2 * SSD_GROUPS * SSD_STATE
D_FF = 2816

C_Z = 0
C_XBC = C_Z + D_HEADS
C_DT = C_XBC + D_CONV
C_Q = C_DT + LANES
C_K = C_Q + D_HEADS
C_V = C_K + D_HEADS
C_END = C_V + D_HEADS

TM_PROJ = 512
PROJ_PIECE = 256
T_SSD = 128
SSD_SEGMENTS = 2 + N_PAIRS
T_ATT = 128
TM_FFN = 512
SUB_FFN = 256
FF_CHUNK = 256
CONV_TAIL = 8

VMEM_LIMIT = 56 * 1024 * 1024


def _dot(a, b):
    return jnp.dot(a, b, preferred_element_type=F32)


def _dot_nt(a, b):
    return lax.dot_general(a, b, (((1,), (1,)), ((), ())), preferred_element_type=F32)


def _dot_tn(a, b):
    return lax.dot_general(a, b, (((0,), (0,)), ((), ())), preferred_element_type=F32)


def _split3(x):
    hi = x.astype(BF16)
    r = x - hi.astype(F32)
    mid = r.astype(BF16)
    lo = (r - mid.astype(F32)).astype(BF16)
    return hi, mid, lo


def _silu(x):
    return x / (1.0 + jnp.exp(-x))


def _softplus(x):
    e = jnp.exp(-jnp.abs(x))
    u = 1.0 + e
    tiny = u == 1.0
    log1p_e = jnp.where(tiny, e, jnp.log(u) * (e / jnp.where(tiny, 1.0, u - 1.0)))
    return jnp.maximum(x, 0.0) + log1p_e


def _projection_pieces(x_ref, g_ref, w_ref, dtb_ref, h_ref, sz_ref, xbc_ref, dt_ref, q_ref,
                       k_ref, v_ref):
    x = x_ref[...]
    ms = jnp.mean(x * x, axis=-1, keepdims=True)
    h_ref[...] = (x * lax.rsqrt(ms + EPS) * g_ref[...]).astype(BF16)
    to_bf16 = lambda y: y.astype(BF16)
    segments = [(sz_ref, C_Z, C_XBC, _silu), (xbc_ref, C_XBC, C_DT, lambda y: y),
                (dt_ref, C_DT, C_Q, lambda y: _softplus(y + dtb_ref[...])),
                (q_ref, C_Q, C_K, to_bf16), (k_ref, C_K, C_V, to_bf16),
                (v_ref, C_V, C_END, to_bf16)]

    def piece(out_ref, c0, lo, hi, post):
        def run():
            out_ref[:, lo:hi] = post(_dot(h_ref[...], w_ref[:, c0 + lo:c0 + hi]))
        return run

    return [piece(out_ref, c0, lo, min(lo + PROJ_PIECE, c1 - c0), post)
            for out_ref, c0, c1, post in segments for lo in range(0, c1 - c0, PROJ_PIECE)]


def _prepare_weight(wt_ref, w_ref):
    o_dt = D_HEADS + D_CONV
    o_q = o_dt + N_HEADS
    scale = 1.0 / math.sqrt(HEAD_DIM)
    lane = lax.broadcasted_iota(jnp.int32, (D_MODEL, LANES), 1)

    def block(row0):
        return wt_ref[row0:row0 + LANES, :].T

    for c in range(0, o_dt, LANES):
        w_ref[:, C_Z + c:C_Z + c + LANES] = block(c).astype(BF16)
    w_ref[:, C_DT:C_Q] = jnp.where(lane < N_HEADS, block(o_dt), 0.0).astype(BF16)
    for c in range(0, D_HEADS, LANES):
        w_ref[:, C_Q + c:C_Q + c + LANES] = (block(o_q + c) * scale).astype(BF16)
    for c in range(D_HEADS, 3 * D_HEADS, LANES):
        w_ref[:, C_Q + c:C_Q + c + LANES] = block(o_q + c).astype(BF16)


def _proj_ssd_kernel(blocks_per_seq, n_blocks, x_ref, g_ref, wt_ref, dtb_ref, cw_ref, cb_ref,
                     a_ref, dskip_ref, gain_ref, utri_ref, q_ref, k_ref, v_ref, y_ref,
                     w_ref, h_ref, sz_ref, xbc_ref, dt_ref, ext_ref, state_ref):
    g = pl.program_id(0)

    @pl.when(g == 0)
    def _():
        _prepare_weight(wt_ref, w_ref)

    @pl.when(lax.rem(g + blocks_per_seq - 1, blocks_per_seq) == 0)
    def _():
        ext_ref[0:CONV_TAIL, :] = jnp.zeros((CONV_TAIL, D_CONV), F32)
        state_ref[...] = jnp.zeros(state_ref.shape, F32)

    def step(new, old, project=True, scan=True):
        pieces = _projection_pieces(
            x_ref, g_ref, w_ref, dtb_ref, h_ref, sz_ref.at[new], xbc_ref.at[new],
            dt_ref.at[new], q_ref, k_ref, v_ref) if project else []
        chunks = [pl.ds(c * T_SSD, T_SSD) for c in range(TM_PROJ // T_SSD)] if scan else []
        segments = itertools.chain.from_iterable(
            _ssd_chunk(sz_ref.at[old, rows], xbc_ref.at[old, rows], dt_ref.at[old, rows],
                       cw_ref, cb_ref, a_ref, dskip_ref, gain_ref, utri_ref, y_ref.at[rows],
                       ext_ref, state_ref) for rows in chunks)
        if not scan:
            for run in pieces:
                run()
            return
        n_segments = len(chunks) * SSD_SEGMENTS
        assert len(pieces) <= n_segments
        slot = {(n * n_segments) // len(pieces): run for n, run in enumerate(pieces)}
        for s in range(n_segments):
            if s in slot:
                slot[s]()
            next(segments)

    even = lax.rem(g, 2) == 0

    @pl.when(g == 0)
    def _():
        step(0, 1, scan=False)

    @pl.when((g > 0) & (g < n_blocks) & even)
    def _():
        step(0, 1)

    @pl.when((g < n_blocks) & jnp.logical_not(even))
    def _():
        step(1, 0)

    @pl.when(g == n_blocks)
    def _():
        step(n_blocks % 2, 1 - n_blocks % 2, project=False)


def _proj_ssd(x2, gain, w_in_t, layer, dtb, conv_w, conv_b, a_row, dskip, ssd_gain, utri, seq):
    m = x2.shape[0]
    n_blocks = m // TM_PROJ
    assert seq % TM_PROJ == 0 and TM_PROJ % T_SSD == 0
    assert w_in_t.shape[1:] == (C_END - (LANES - N_HEADS), D_MODEL)
    cur = lambda g: (jnp.minimum(g, n_blocks - 1), 0)
    prev = lambda g: (jnp.maximum(g - 1, 0), 0)
    const = lambda g: (0, 0)
    qkv_spec = pl.BlockSpec((TM_PROJ, D_HEADS), cur)
    qkv_shape = jax.ShapeDtypeStruct((m, D_HEADS), BF16)
    return pl.pallas_call(
        functools.partial(_proj_ssd_kernel, seq // TM_PROJ, n_blocks),
        grid=(n_blocks + 1,),
        in_specs=[
            pl.BlockSpec((TM_PROJ, D_MODEL), cur),
            pl.BlockSpec((1, D_MODEL), const),
            pl.BlockSpec((None,) + w_in_t.shape[1:], lambda g: (layer, 0, 0),
                         pipeline_mode=pl.Buffered(1)),
            pl.BlockSpec((1, LANES), const),
            pl.BlockSpec((CONV_WIDTH, D_CONV), const),
            pl.BlockSpec((1, D_CONV), const),
            pl.BlockSpec((1, LANES), const),
            pl.BlockSpec((1, D_HEADS), const),
            pl.BlockSpec((1, D_HEADS), const),
            pl.BlockSpec((T_SSD, T_SSD), const),
        ],
        out_specs=[qkv_spec, qkv_spec, qkv_spec, pl.BlockSpec((TM_PROJ, D_HEADS), prev)],
        out_shape=[qkv_shape, qkv_shape, qkv_shape, qkv_shape],
        scratch_shapes=[
            pltpu.VMEM((D_MODEL, C_END), BF16),
            pltpu.VMEM((TM_PROJ, D_MODEL), BF16),
            pltpu.VMEM((2, TM_PROJ, D_HEADS), F32),
            pltpu.VMEM((2, TM_PROJ, D_CONV), F32),
            pltpu.VMEM((2, TM_PROJ, LANES), F32),
            pltpu.VMEM((T_SSD + CONV_TAIL, D_CONV), F32),
            pltpu.VMEM((N_PAIRS, SSD_STATE, LANES), F32),
        ],
        compiler_params=pltpu.CompilerParams(
            dimension_semantics=("arbitrary",), vmem_limit_bytes=VMEM_LIMIT),
        name="proj_ssd",
    )(x2, gain, w_in_t, dtb, conv_w, conv_b, a_row, dskip, ssd_gain, utri)


def _ssd_chunk(sz_ref, xbc_ref, dt_ref, cw_ref, cb_ref, a_ref, dskip_ref, gain_ref, utri_ref,
               y_ref, ext_ref, state_ref):
    t = T_SSD

    ext_ref[CONV_TAIL:CONV_TAIL + t, :] = xbc_ref[...]
    conv = cb_ref[...]
    for k in range(CONV_WIDTH):
        off = CONV_TAIL - (CONV_WIDTH - 1) + k
        conv = conv + ext_ref[off:off + t, :] * cw_ref[k:k + 1, :]
    ext_ref[0:CONV_TAIL, :] = ext_ref[t:t + CONV_TAIL, :]
    xa = _silu(conv)
    yield

    dtv = dt_ref[...]
    adt = dtv * a_ref[...]
    utri = utri_ref[...]
    acs_t = sum(_dot(part, utri) for part in _split3(adt.T))
    acs = acs_t.T

    lane = lax.broadcasted_iota(jnp.int32, (t, LANES), 1)
    first_head = lane < HEAD_DIM
    li = lax.broadcasted_iota(jnp.int32, (t, t), 0)
    si = lax.broadcasted_iota(jnp.int32, (t, t), 1)
    causal = li >= si

    def col(v, h):
        return jnp.broadcast_to(v[:, h:h + 1], (t, LANES))

    y_blocks = []
    cb_mats = []
    for g in range(SSD_GROUPS):
        bm = xa[:, D_HEADS + g * SSD_STATE:D_HEADS + (g + 1) * SSD_STATE].astype(BF16)
        cm = xa[:, D_HEADS + (SSD_GROUPS + g) * SSD_STATE:
                D_HEADS + (SSD_GROUPS + g + 1) * SSD_STATE].astype(BF16)
        cb_mats.append((bm, cm, _dot_nt(cm, bm)))
    yield

    for p in range(N_PAIRS):
        ha, hb = 2 * p, 2 * p + 1
        bm, cm, cbm = cb_mats[p // (N_PAIRS // SSD_GROUPS)]
        x2 = xa[:, p * LANES:(p + 1) * LANES]
        dt2 = jnp.where(first_head, col(dtv, ha), col(dtv, hb))
        acs2 = jnp.where(first_head, col(acs, ha), col(acs, hb))
        xdt2 = x2 * dt2

        def decay(h):
            seg = col(acs, h) - jnp.broadcast_to(acs_t[h:h + 1, :], (t, t))
            return (cbm * jnp.exp(jnp.where(causal, seg, -jnp.inf))).astype(BF16)

        m2 = jnp.concatenate([decay(ha), decay(hb)], axis=1)
        xdt_a = jnp.where(first_head, xdt2, 0.0).astype(BF16)
        xdt_b = jnp.where(first_head, 0.0, xdt2).astype(BF16)
        y_diag = _dot(m2, jnp.concatenate([xdt_a, xdt_b], axis=0))

        prev = state_ref[p]
        y_off = _dot(cm, prev.astype(BF16)) * jnp.exp(acs2)
        last = acs2[t - 1:t, :]
        xs = (xdt2 * jnp.exp(last - acs2)).astype(BF16)
        state_ref[p] = prev * jnp.exp(last) + _dot_tn(bm, xs)

        y_blocks.append(y_diag + y_off + dskip_ref[:, p * LANES:(p + 1) * LANES] * x2)
        if p + 1 < N_PAIRS:
            yield

    per_group = N_PAIRS // SSD_GROUPS
    for g in range(SSD_GROUPS):
        ys = []
        for p in range(g * per_group, (g + 1) * per_group):
            ys.append(y_blocks[p] * sz_ref[:, p * LANES:(p + 1) * LANES])
        ss = sum(jnp.sum(y * y, axis=-1, keepdims=True) for y in ys)
        inv = lax.rsqrt(ss * (1.0 / (per_group * LANES)) + EPS)
        for j, y in enumerate(ys):
            p = g * per_group + j
            y_ref[:, p * LANES:(p + 1) * LANES] = (
                y * inv * gain_ref[:, p * LANES:(p + 1) * LANES]).astype(BF16)
    yield


LOG2E = 1.4426950408889634
ATT_DEPTH = 4
ATT_NEAR_DIAGONALS = 3
ATT_UNDERFLOW_LOG2 = -160.0


def _attn_kernel(n_cast, ni_ref, nj_ref, fi_ref, fj_ref, q_ref, k_ref, v_ref, w2_ref, *refs):
    cast_in, (o_ref, *cast_out) = refs[:n_cast], refs[n_cast:2 * n_cast + 1]
    q2_ref, kst_ref, vst_ref, r_ref, acc_ref, z_ref, arg_ref = refs[2 * n_cast + 1:]
    for src_ref, dst_ref in zip(cast_in, cast_out, strict=True):
        dst_ref[...] = src_ref[...].astype(BF16)

    t = T_ATT
    n_blocks = q2_ref.shape[0] - 1
    depth = ATT_DEPTH

    lane_v = lax.broadcasted_iota(jnp.int32, (t, LANES), 1)
    first_v = lane_v < HEAD_DIM

    for j in range(n_blocks):
        kb = k_ref[j * t:(j + 1) * t, :]
        vb = v_ref[j * t:(j + 1) * t, :]
        zero = jnp.zeros_like(kb)
        q2_ref[j] = q_ref[j * t:(j + 1) * t, :]
        kst_ref[j, 0:t, :] = jnp.where(first_v, kb, zero)
        kst_ref[j, t:2 * t, :] = jnp.where(first_v, zero, kb)
        vst_ref[j, 0:t, :] = jnp.where(first_v, vb, zero)
        vst_ref[j, t:2 * t, :] = jnp.where(first_v, zero, vb)
    q2_ref[n_blocks] = jnp.zeros(q2_ref.shape[1:], BF16)
    r_ref[n_blocks] = jnp.zeros(r_ref.shape[1:], F32)
    acc_ref[n_blocks] = jnp.zeros(acc_ref.shape[1:], F32)

    ti = lax.broadcasted_iota(jnp.int32, (t, 2 * t), 0)
    si = lax.broadcasted_iota(jnp.int32, (t, 2 * t), 1)
    strict = jnp.where(si >= t, si - t, si) < ti

    def stage_scores(tiles, z_buf):
        for u, (i, j) in enumerate(tiles):
            z_buf[u] = _dot_nt(q2_ref[i], kst_ref[j])

    def stage_suffix_sums(tiles, z_buf, diag):
        for u, (i, _) in enumerate(tiles):
            zn = z_buf[u] * (-LOG2E)
            l1 = jnp.minimum(zn, 0.0) - jnp.log2(1.0 + jnp.exp2(-jnp.abs(zn)))
            if diag:
                l1 = jnp.where(strict, l1, 0.0)
            hi = l1.astype(BF16)
            lo = (l1 - hi.astype(F32)).astype(BF16)
            res = [_dot(jnp.concatenate([hi[:, h * t:(h + 1) * t], lo[:, h * t:(h + 1) * t]],
                                        axis=1), w2_ref[...]) for h in range(2)]
            arg = jnp.concatenate([r[:, 0:t] for r in res], axis=1) - zn
            tot = jnp.concatenate([r[:, t:2 * t] for r in res], axis=1)
            if diag:
                arg = jnp.where(strict, arg, -jnp.inf)
            else:
                r_old = r_ref[i]
                arg = arg + r_old
                tot = tot + r_old
            arg_ref[u] = arg
            r_ref[i] = tot

    def stage_values(tiles, diag):
        for u, (i, j) in enumerate(tiles):
            contrib = _dot(jnp.exp2(arg_ref[u]).astype(BF16), vst_ref[j])
            if diag:
                acc_ref[i] = contrib
            else:
                acc_ref[i] += contrib

    def sweep(i_ref, j_ref, diagonals, n_diag_groups):
        n_groups = i_ref.shape[0] // depth
        n_real = sum(n_blocks - d for d in diagonals)
        n_static = n_diag_groups + 2
        assert n_static % 2 == 0 and n_groups % 2 == 0 and n_groups >= n_static

        def group(g, static):
            count = min(depth, n_real - g * depth) if static else depth
            return [(i_ref[g * depth + u], j_ref[g * depth + u]) for u in range(count)]

        def iteration(m, parity, static):
            if not static or m < n_groups:
                stage_scores(group(m, static), z_ref.at[parity])
            if not static or 0 <= m - 2 < n_groups:
                stage_values(group(m - 2, static), static and m - 2 < n_diag_groups)
            if not static or 0 <= m - 1 < n_groups:
                stage_suffix_sums(group(m - 1, static), z_ref.at[1 - parity],
                                  static and m - 1 < n_diag_groups)

        for m in range(n_static):
            iteration(m, m % 2, True)

        def body(mm, carry):
            iteration(2 * mm, 0, False)
            iteration(2 * mm + 1, 1, False)
            return carry

        lax.fori_loop(n_static // 2, n_groups // 2, body, 0)
        for m in (n_groups, n_groups + 1):
            iteration(m, m % 2, True)

    sweep(ni_ref, nj_ref, range(ATT_NEAR_DIAGONALS), pl.cdiv(n_blocks, depth))

    far_blocks = range(ATT_NEAR_DIAGONALS, n_blocks)
    r_max = functools.reduce(jnp.maximum, [r_ref[i] for i in far_blocks])

    @pl.when(jnp.max(r_max) >= ATT_UNDERFLOW_LOG2)
    def _():
        sweep(fi_ref, fj_ref, range(ATT_NEAR_DIAGONALS, n_blocks), 0)

    for i in range(n_blocks):
        o_ref[i * t:(i + 1) * t, :] = acc_ref[i]


def _rms(x, g):
    ms = jnp.mean(x * x, axis=-1, keepdims=True)
    return x * lax.rsqrt(ms + EPS) * g


def _head_rms(o, g):
    first = lax.broadcasted_iota(jnp.int32, o.shape, 1) < HEAD_DIM
    o2 = o * o
    ss_a = jnp.sum(jnp.where(first, o2, 0.0), axis=-1, keepdims=True)
    ss_b = jnp.sum(jnp.where(first, 0.0, o2), axis=-1, keepdims=True)
    ms = jnp.where(first, ss_a, ss_b) * (1.0 / HEAD_DIM)
    return o * lax.rsqrt(ms + EPS) * g


def _attention(q, k, v, w2, to_cast, batch, seq):
    n_q = seq // T_ATT
    assert n_q % ATT_DEPTH == 0
    n_steps = batch * N_PAIRS

    def slab_spec(w):
        rows = w.shape[0]
        slab = next(s for s in range(BF16_SUBLANES, rows + 1, BF16_SUBLANES)
                    if rows % s == 0 and rows // s <= n_steps)
        last = rows // slab - 1
        return pl.BlockSpec((slab, w.shape[1]),
                            lambda b, p, *_: (jnp.minimum(b * N_PAIRS + p, last), 0))

    cast_specs = [slab_spec(w) for w in to_cast]

    def tile_list(diagonals):
        tiles = [(i, i - d) for d in diagonals for i in range(d, n_q)]
        tiles += [(n_q, 0)] * (-len(tiles) % (2 * ATT_DEPTH))
        return (jnp.asarray([i for i, _ in tiles], jnp.int32),
                jnp.asarray([j for _, j in tiles], jnp.int32))

    near = tile_list(range(ATT_NEAR_DIAGONALS))
    far = tile_list(range(ATT_NEAR_DIAGONALS, n_q))
    seq_blk = lambda b, p, *_: (b, p)
    return pl.pallas_call(
        functools.partial(_attn_kernel, len(to_cast)),
        grid_spec=pltpu.PrefetchScalarGridSpec(
            num_scalar_prefetch=4,
            grid=(batch, N_PAIRS),
            in_specs=[
                pl.BlockSpec((seq, LANES), seq_blk),
                pl.BlockSpec((seq, LANES), seq_blk),
                pl.BlockSpec((seq, LANES), seq_blk),
                pl.BlockSpec((2 * T_ATT, 2 * T_ATT), lambda b, p, *_: (0, 0)),
            ] + cast_specs,
            out_specs=[pl.BlockSpec((seq, LANES), seq_blk)] + cast_specs,
            scratch_shapes=[
                pltpu.VMEM((n_q + 1, T_ATT, LANES), BF16),
                pltpu.VMEM((n_q, 2 * T_ATT, LANES), BF16),
                pltpu.VMEM((n_q, 2 * T_ATT, LANES), BF16),
                pltpu.VMEM((n_q + 1, T_ATT, 2 * T_ATT), F32),
                pltpu.VMEM((n_q + 1, T_ATT, LANES), F32),
                pltpu.VMEM((2, ATT_DEPTH, T_ATT, 2 * T_ATT), F32),
                pltpu.VMEM((ATT_DEPTH, T_ATT, 2 * T_ATT), F32),
            ],
        ),
        out_shape=[jax.ShapeDtypeStruct((batch * seq, D_HEADS), F32)]
        + [jax.ShapeDtypeStruct(w.shape, BF16) for w in to_cast],
        compiler_params=pltpu.CompilerParams(
            dimension_semantics=("arbitrary", "arbitrary"),
            vmem_limit_bytes=VMEM_LIMIT),
        name="sb_attention",
    )(*near, *far, q, k, v, w2, *to_cast)


def _ffn_kernel(x_ref, yssd_ref, osb_ref, g_sb_ref, wo_ref, g_post_ref, g_pre_ref, wg_ref,
                wu_ref, wd_ref, g_out_ref, o_ref, x1_ref, h_ref, act_ref):
    chunks = [slice(c * FF_CHUNK, (c + 1) * FF_CHUNK) for c in range(D_FF // FF_CHUNK)]
    subs = [pl.ds(s * SUB_FFN, SUB_FFN) for s in range(TM_FFN // SUB_FFN)]
    pairs = [slice(p * LANES, (p + 1) * LANES) for p in range(N_PAIRS)]
    for rows in subs:
        y_sb = jnp.concatenate([_head_rms(osb_ref[rows, cols], g_sb_ref[:, cols])
                                for cols in pairs], axis=1).astype(BF16)
        mix = (_dot(yssd_ref[rows, :], wo_ref[0:D_HEADS, :])
               + _dot(y_sb, wo_ref[D_HEADS:2 * D_HEADS, :]))
        x1 = x_ref[rows, :] + _rms(mix, g_post_ref[...])
        x1_ref[rows, :] = x1
        h_ref[rows, :] = _rms(x1, g_pre_ref[...]).astype(BF16)
    for rows in subs:
        for cols in chunks:
            h = h_ref[rows, :]
            gate = _dot(h, wg_ref[:, cols])
            up = _dot(h, wu_ref[:, cols])
            act_ref[rows, cols] = (_silu(gate) * up).astype(BF16)
    for rows in subs:
        f = _dot(act_ref[rows, :], wd_ref[...])
        o_ref[rows, :] = x1_ref[rows, :] + _rms(f, g_out_ref[...])


def _out_ffn(x2, y_ssd, o_sb, g_sb, w_out, g_post, g_pre, wg, wu, wd, g_out):
    m = x2.shape[0]
    row = lambda i: (i, 0)
    const2 = lambda i: (0, 0)
    single = pl.Buffered(1)
    return pl.pallas_call(
        _ffn_kernel,
        grid=(m // TM_FFN,),
        in_specs=[
            pl.BlockSpec((TM_FFN, D_MODEL), row),
            pl.BlockSpec((TM_FFN, D_HEADS), row),
            pl.BlockSpec((TM_FFN, D_HEADS), row),
            pl.BlockSpec((1, D_HEADS), const2),
            pl.BlockSpec((2 * D_HEADS, D_MODEL), const2, pipeline_mode=single),
            pl.BlockSpec((1, D_MODEL), const2),
            pl.BlockSpec((1, D_MODEL), const2),
            pl.BlockSpec((D_MODEL, D_FF), const2, pipeline_mode=single),
            pl.BlockSpec((D_MODEL, D_FF), const2, pipeline_mode=single),
            pl.BlockSpec((D_FF, D_MODEL), const2, pipeline_mode=single),
            pl.BlockSpec((1, D_MODEL), const2),
        ],
        out_specs=pl.BlockSpec((TM_FFN, D_MODEL), row),
        out_shape=jax.ShapeDtypeStruct((m, D_MODEL), F32),
        scratch_shapes=[
            pltpu.VMEM((TM_FFN, D_MODEL), F32),
            pltpu.VMEM((TM_FFN, D_MODEL), BF16),
            pltpu.VMEM((TM_FFN, D_FF), BF16),
        ],
        compiler_params=pltpu.CompilerParams(
            dimension_semantics=("arbitrary",), vmem_limit_bytes=VMEM_LIMIT),
        name="out_ffn",
    )(x2, y_ssd, o_sb, g_sb, w_out, g_post, g_pre, wg, wu, wd, g_out)


def _expand_heads(v):
    return jnp.repeat(v.astype(F32), HEAD_DIM)[None, :]


def _pad_lanes(v):
    return jnp.pad(v.astype(F32), (0, LANES - v.shape[0]))[None, :]


def _layer(x2, batch, seq, w_in, layer, pre_mix_gain, conv_w, conv_b, dt_bias, a_log, d_skip,
           ssd_norm_gain, sb_norm_gain, w_out, post_mix_gain, pre_ffn_gain, w_gate, w_up,
           w_down, post_ffn_gain):
    utri = jnp.triu(jnp.ones((T_SSD, T_SSD), BF16))
    a_row = _pad_lanes(-jnp.exp(a_log.astype(F32)))
    q, k, v, y_ssd = _proj_ssd(x2, pre_mix_gain[None, :], jnp.swapaxes(w_in, 1, 2), layer,
                               _pad_lanes(dt_bias), conv_w, conv_b[None, :], a_row,
                               _expand_heads(d_skip), ssd_norm_gain[None, :], utri, seq)

    jj = jnp.arange(T_ATT)
    later = (jj[:, None] >= jj[None, :]).astype(BF16)
    half = jnp.concatenate([later, jnp.ones((T_ATT, T_ATT), BF16)], axis=1)
    w2 = jnp.concatenate([half, half], axis=0)
    o_sb, wo, wg, wu, wd = _attention(q, k, v, w2, (w_out, w_gate, w_up, w_down), batch, seq)

    return _out_ffn(x2, y_ssd, o_sb, sb_norm_gain[None, :], wo, post_mix_gain[None, :],
                    pre_ffn_gain[None, :], wg, wu, wd, post_ffn_gain[None, :])


def kernel(x, pre_mix_gain, w_in, conv_w, conv_b, dt_bias, a_log, d_skip, ssd_norm_gain,
           sb_norm_gain, w_out, post_mix_gain, pre_ffn_gain, w_gate, w_up, w_down,
           post_ffn_gain):
    batch, seq, d = x.shape
    x2 = x.reshape(batch * seq, d)
    params = (pre_mix_gain, conv_w, conv_b, dt_bias, a_log, d_skip, ssd_norm_gain,
              sb_norm_gain, w_out, post_mix_gain, pre_ffn_gain, w_gate, w_up, w_down,
              post_ffn_gain)
    for layer in range(pre_mix_gain.shape[0]):
        x2 = _layer(x2, batch, seq, w_in, layer, *(p[layer] for p in params))
    return x2.reshape(batch, seq, d)
```

```python
import functools
import itertools
import math

import jax
import jax.numpy as jnp
from jax import lax
from jax.experimental import pallas as pl
from jax.experimental.pallas import tpu as pltpu

F32 = jnp.float32
BF16 = jnp.bfloat16

EPS = 1e-6
LANES = 128
BF16_SUBLANES = 16

D_MODEL = 1024
N_HEADS = 8
HEAD_DIM = 64
D_HEADS = N_HEADS * HEAD_DIM
N_PAIRS = N_HEADS // 2
SSD_GROUPS = 2
SSD_STATE = 128
CONV_WIDTH = 4
D_CONV = D_HEADS + 2 * SSD_GROUPS * SSD_STATE
D_FF = 2816

C_Z = 0
C_XBC = C_Z + D_HEADS
C_DT = C_XBC + D_CONV
C_Q = C_DT + LANES
C_K = C_Q + D_HEADS
C_V = C_K + D_HEADS
C_END = C_V + D_HEADS

TM_PROJ = 512
PROJ_PIECE = 256
T_SSD = 128
SSD_SEGMENTS = 2 + N_PAIRS
T_ATT = 128
TM_FFN = 512
SUB_FFN = 256
FF_CHUNK = 256
CONV_TAIL = 8

VMEM_LIMIT = 56 * 1024 * 1024


def _dot(a, b):
    return jnp.dot(a, b, preferred_element_type=F32)


def _dot_nt(a, b):
    return lax.dot_general(a, b, (((1,), (1,)), ((), ())), preferred_element_type=F32)


def _dot_tn(a, b):
    return lax.dot_general(a, b, (((0,), (0,)), ((), ())), preferred_element_type=F32)


def _split3(x):
    hi = x.astype(BF16)
    r = x - hi.astype(F32)
    mid = r.astype(BF16)
    lo = (r - mid.astype(F32)).astype(BF16)
    return hi, mid, lo


def _silu(x):
    return x / (1.0 + jnp.exp(-x))


def _softplus(x):
    e = jnp.exp(-jnp.abs(x))
    u = 1.0 + e
    tiny = u == 1.0
    log1p_e = jnp.where(tiny, e, jnp.log(u) * (e / jnp.where(tiny, 1.0, u - 1.0)))
    return jnp.maximum(x, 0.0) + log1p_e


def _projection_pieces(x_ref, g_ref, w_ref, dtb_ref, h_ref, sz_ref, xbc_ref, dt_ref, q_ref,
                       k_ref, v_ref):
    x = x_ref[...]
    ms = jnp.mean(x * x, axis=-1, keepdims=True)
    h_ref[...] = (x * lax.rsqrt(ms + EPS) * g_ref[...]).astype(BF16)
    to_bf16 = lambda y: y.astype(BF16)
    segments = [(sz_ref, C_Z, C_XBC, _silu), (xbc_ref, C_XBC, C_DT, lambda y: y),
                (dt_ref, C_DT, C_Q, lambda y: _softplus(y + dtb_ref[...])),
                (q_ref, C_Q, C_K, to_bf16), (k_ref, C_K, C_V, to_bf16),
                (v_ref, C_V, C_END, to_bf16)]

    def piece(out_ref, c0, lo, hi, post):
        def run():
            out_ref[:, lo:hi] = post(_dot(h_ref[...], w_ref[:, c0 + lo:c0 + hi]))
        return run

    return [piece(out_ref, c0, lo, min(lo + PROJ_PIECE, c1 - c0), post)
            for out_ref, c0, c1, post in segments for lo in range(0, c1 - c0, PROJ_PIECE)]


def _prepare_weight(wt_ref, w_ref):
    o_dt = D_HEADS + D_CONV
    o_q = o_dt + N_HEADS
    scale = 1.0 / math.sqrt(HEAD_DIM)
    lane = lax.broadcasted_iota(jnp.int32, (D_MODEL, LANES), 1)

    def block(row0):
        return wt_ref[row0:row0 + LANES, :].T

    for c in range(0, o_dt, LANES):
        w_ref[:, C_Z + c:C_Z + c + LANES] = block(c).astype(BF16)
    w_ref[:, C_DT:C_Q] = jnp.where(lane < N_HEADS, block(o_dt), 0.0).astype(BF16)
    for c in range(0, D_HEADS, LANES):
        w_ref[:, C_Q + c:C_Q + c + LANES] = (block(o_q + c) * scale).astype(BF16)
    for c in range(D_HEADS, 3 * D_HEADS, LANES):
        w_ref[:, C_Q + c:C_Q + c + LANES] = block(o_q + c).astype(BF16)


def _proj_ssd_kernel(blocks_per_seq, n_blocks, x_ref, g_ref, wt_ref, dtb_ref, cw_ref, cb_ref,
                     a_ref, dskip_ref, gain_ref, utri_ref, q_ref, k_ref, v_ref, y_ref,
                     w_ref, h_ref, sz_ref, xbc_ref, dt_ref, ext_ref, state_ref):
    g = pl.program_id(0)

    @pl.when(g == 0)
    def _():
        _prepare_weight(wt_ref, w_ref)

    @pl.when(lax.rem(g + blocks_per_seq - 1, blocks_per_seq) == 0)
    def _():
        ext_ref[0:CONV_TAIL, :] = jnp.zeros((CONV_TAIL, D_CONV), F32)
        state_ref[...] = jnp.zeros(state_ref.shape, F32)

    def step(new, old, project=True, scan=True):
        pieces = _projection_pieces(
            x_ref, g_ref, w_ref, dtb_ref, h_ref, sz_ref.at[new], xbc_ref.at[new],
            dt_ref.at[new], q_ref, k_ref, v_ref) if project else []
        chunks = [pl.ds(c * T_SSD, T_SSD) for c in range(TM_PROJ // T_SSD)] if scan else []
        segments = itertools.chain.from_iterable(
            _ssd_chunk(sz_ref.at[old, rows], xbc_ref.at[old, rows], dt_ref.at[old, rows],
                       cw_ref, cb_ref, a_ref, dskip_ref, gain_ref, utri_ref, y_ref.at[rows],
                       ext_ref, state_ref) for rows in chunks)
        if not scan:
            for run in pieces:
                run()
            return
        n_segments = len(chunks) * SSD_SEGMENTS
        assert len(pieces) <= n_segments
        slot = {(n * n_segments) // len(pieces): run for n, run in enumerate(pieces)}
        for s in range(n_segments):
            next(segments)
            if s in slot:
                slot[s]()

    even = lax.rem(g, 2) == 0

    @pl.when(g == 0)
    def _():
        step(0, 1, scan=False)

    @pl.when((g > 0) & (g < n_blocks) & even)
    def _():
        step(0, 1)

    @pl.when((g < n_blocks) & jnp.logical_not(even))
    def _():
        step(1, 0)

    @pl.when(g == n_blocks)
    def _():
        step(n_blocks % 2, 1 - n_blocks % 2, project=False)


def _proj_ssd(x2, gain, w_in_t, layer, dtb, conv_w, conv_b, a_row, dskip, ssd_gain, utri, seq):
    m = x2.shape[0]
    n_blocks = m // TM_PROJ
    assert seq % TM_PROJ == 0 and TM_PROJ % T_SSD == 0
    assert w_in_t.shape[1:] == (C_END - (LANES - N_HEADS), D_MODEL)
    cur = lambda g: (jnp.minimum(g, n_blocks - 1), 0)
    prev = lambda g: (jnp.maximum(g - 1, 0), 0)
    const = lambda g: (0, 0)
    qkv_spec = pl.BlockSpec((TM_PROJ, D_HEADS), cur)
    qkv_shape = jax.ShapeDtypeStruct((m, D_HEADS), BF16)
    return pl.pallas_call(
        functools.partial(_proj_ssd_kernel, seq // TM_PROJ, n_blocks),
        grid=(n_blocks + 1,),
        in_specs=[
            pl.BlockSpec((TM_PROJ, D_MODEL), cur),
            pl.BlockSpec((1, D_MODEL), const),
            pl.BlockSpec((None,) + w_in_t.shape[1:], lambda g: (layer, 0, 0),
                         pipeline_mode=pl.Buffered(1)),
            pl.BlockSpec((1, LANES), const),
            pl.BlockSpec((CONV_WIDTH, D_CONV), const),
            pl.BlockSpec((1, D_CONV), const),
            pl.BlockSpec((1, LANES), const),
            pl.BlockSpec((1, D_HEADS), const),
            pl.BlockSpec((1, D_HEADS), const),
            pl.BlockSpec((T_SSD, T_SSD), const),
        ],
        out_specs=[qkv_spec, qkv_spec, qkv_spec, pl.BlockSpec((TM_PROJ, D_HEADS), prev)],
        out_shape=[qkv_shape, qkv_shape, qkv_shape, qkv_shape],
        scratch_shapes=[
            pltpu.VMEM((D_MODEL, C_END), BF16),
            pltpu.VMEM((TM_PROJ, D_MODEL), BF16),
            pltpu.VMEM((2, TM_PROJ, D_HEADS), F32),
            pltpu.VMEM((2, TM_PROJ, D_CONV), F32),
            pltpu.VMEM((2, TM_PROJ, LANES), F32),
            pltpu.VMEM((T_SSD + CONV_TAIL, D_CONV), F32),
            pltpu.VMEM((N_PAIRS, SSD_STATE, LANES), F32),
        ],
        compiler_params=pltpu.CompilerParams(
            dimension_semantics=("arbitrary",), vmem_limit_bytes=VMEM_LIMIT),
        name="proj_ssd",
    )(x2, gain, w_in_t, dtb, conv_w, conv_b, a_row, dskip, ssd_gain, utri)


def _ssd_chunk(sz_ref, xbc_ref, dt_ref, cw_ref, cb_ref, a_ref, dskip_ref, gain_ref, utri_ref,
               y_ref, ext_ref, state_ref):
    t = T_SSD

    ext_ref[CONV_TAIL:CONV_TAIL + t, :] = xbc_ref[...]
    conv = cb_ref[...]
    for k in range(CONV_WIDTH):
        off = CONV_TAIL - (CONV_WIDTH - 1) + k
        conv = conv + ext_ref[off:off + t, :] * cw_ref[k:k + 1, :]
    ext_ref[0:CONV_TAIL, :] = ext_ref[t:t + CONV_TAIL, :]
    xa = _silu(conv)
    yield

    dtv = dt_ref[...]
    adt = dtv * a_ref[...]
    utri = utri_ref[...]
    acs_t = sum(_dot(part, utri) for part in _split3(adt.T))
    acs = acs_t.T

    lane = lax.broadcasted_iota(jnp.int32, (t, LANES), 1)
    first_head = lane < HEAD_DIM
    li = lax.broadcasted_iota(jnp.int32, (t, t), 0)
    si = lax.broadcasted_iota(jnp.int32, (t, t), 1)
    causal = li >= si

    def col(v, h):
        return jnp.broadcast_to(v[:, h:h + 1], (t, LANES))

    y_blocks = []
    cb_mats = []
    for g in range(SSD_GROUPS):
        bm = xa[:, D_HEADS + g * SSD_STATE:D_HEADS + (g + 1) * SSD_STATE].astype(BF16)
        cm = xa[:, D_HEADS + (SSD_GROUPS + g) * SSD_STATE:
                D_HEADS + (SSD_GROUPS + g + 1) * SSD_STATE].astype(BF16)
        cb_mats.append((bm, cm, _dot_nt(cm, bm)))
    yield

    for p in range(N_PAIRS):
        ha, hb = 2 * p, 2 * p + 1
        bm, cm, cbm = cb_mats[p // (N_PAIRS // SSD_GROUPS)]
        x2 = xa[:, p * LANES:(p + 1) * LANES]
        dt2 = jnp.where(first_head, col(dtv, ha), col(dtv, hb))
        acs2 = jnp.where(first_head, col(acs, ha), col(acs, hb))
        xdt2 = x2 * dt2

        def decay(h):
            seg = col(acs, h) - jnp.broadcast_to(acs_t[h:h + 1, :], (t, t))
            return (cbm * jnp.exp(jnp.where(causal, seg, -jnp.inf))).astype(BF16)

        m2 = jnp.concatenate([decay(ha), decay(hb)], axis=1)
        xdt_a = jnp.where(first_head, xdt2, 0.0).astype(BF16)
        xdt_b = jnp.where(first_head, 0.0, xdt2).astype(BF16)
        y_diag = _dot(m2, jnp.concatenate([xdt_a, xdt_b], axis=0))

        prev = state_ref[p]
        y_off = _dot(cm, prev.astype(BF16)) * jnp.exp(acs2)
        last = acs2[t - 1:t, :]
        xs = (xdt2 * jnp.exp(last - acs2)).astype(BF16)
        state_ref[p] = prev * jnp.exp(last) + _dot_tn(bm, xs)

        y_blocks.append(y_diag + y_off + dskip_ref[:, p * LANES:(p + 1) * LANES] * x2)
        if p + 1 < N_PAIRS:
            yield

    per_group = N_PAIRS // SSD_GROUPS
    for g in range(SSD_GROUPS):
        ys = []
        for p in range(g * per_group, (g + 1) * per_group):
            ys.append(y_blocks[p] * sz_ref[:, p * LANES:(p + 1) * LANES])
        ss = sum(jnp.sum(y * y, axis=-1, keepdims=True) for y in ys)
        inv = lax.rsqrt(ss * (1.0 / (per_group * LANES)) + EPS)
        for j, y in enumerate(ys):
            p = g * per_group + j
            y_ref[:, p * LANES:(p + 1) * LANES] = (
                y * inv * gain_ref[:, p * LANES:(p + 1) * LANES]).astype(BF16)
    yield


LOG2E = 1.4426950408889634
ATT_DEPTH = 4
ATT_NEAR_DIAGONALS = 3
ATT_UNDERFLOW_LOG2 = -160.0


def _attn_kernel(n_cast, ni_ref, nj_ref, fi_ref, fj_ref, q_ref, k_ref, v_ref, w2_ref, *refs):
    cast_in, (o_ref, *cast_out) = refs[:n_cast], refs[n_cast:2 * n_cast + 1]
    q2_ref, kst_ref, vst_ref, r_ref, acc_ref, z_ref, arg_ref = refs[2 * n_cast + 1:]
    for src_ref, dst_ref in zip(cast_in, cast_out, strict=True):
        dst_ref[...] = src_ref[...].astype(BF16)

    t = T_ATT
    n_blocks = q2_ref.shape[0] - 1
    depth = ATT_DEPTH

    lane_v = lax.broadcasted_iota(jnp.int32, (t, LANES), 1)
    first_v = lane_v < HEAD_DIM

    for j in range(n_blocks):
        kb = k_ref[j * t:(j + 1) * t, :]
        vb = v_ref[j * t:(j + 1) * t, :]
        zero = jnp.zeros_like(kb)
        q2_ref[j] = q_ref[j * t:(j + 1) * t, :]
        kst_ref[j, 0:t, :] = jnp.where(first_v, kb, zero)
        kst_ref[j, t:2 * t, :] = jnp.where(first_v, zero, kb)
        vst_ref[j, 0:t, :] = jnp.where(first_v, vb, zero)
        vst_ref[j, t:2 * t, :] = jnp.where(first_v, zero, vb)
    q2_ref[n_blocks] = jnp.zeros(q2_ref.shape[1:], BF16)
    r_ref[n_blocks] = jnp.zeros(r_ref.shape[1:], F32)
    acc_ref[n_blocks] = jnp.zeros(acc_ref.shape[1:], F32)

    ti = lax.broadcasted_iota(jnp.int32, (t, 2 * t), 0)
    si = lax.broadcasted_iota(jnp.int32, (t, 2 * t), 1)
    strict = jnp.where(si >= t, si - t, si) < ti

    def stage_scores(tiles, z_buf):
        for u, (i, j) in enumerate(tiles):
            z_buf[u] = _dot_nt(q2_ref[i], kst_ref[j])

    def stage_suffix_sums(tiles, z_buf, diag):
        for u, (i, _) in enumerate(tiles):
            zn = z_buf[u] * (-LOG2E)
            l1 = jnp.minimum(zn, 0.0) - jnp.log2(1.0 + jnp.exp2(-jnp.abs(zn)))
            if diag:
                l1 = jnp.where(strict, l1, 0.0)
            hi = l1.astype(BF16)
            lo = (l1 - hi.astype(F32)).astype(BF16)
            res = [_dot(jnp.concatenate([hi[:, h * t:(h + 1) * t], lo[:, h * t:(h + 1) * t]],
                                        axis=1), w2_ref[...]) for h in range(2)]
            arg = jnp.concatenate([r[:, 0:t] for r in res], axis=1) - zn
            tot = jnp.concatenate([r[:, t:2 * t] for r in res], axis=1)
            if diag:
                arg = jnp.where(strict, arg, -jnp.inf)
            else:
                r_old = r_ref[i]
                arg = arg + r_old
                tot = tot + r_old
            arg_ref[u] = arg
            r_ref[i] = tot

    def stage_values(tiles, diag):
        for u, (i, j) in enumerate(tiles):
            contrib = _dot(jnp.exp2(arg_ref[u]).astype(BF16), vst_ref[j])
            if diag:
                acc_ref[i] = contrib
            else:
                acc_ref[i] += contrib

    def sweep(i_ref, j_ref, diagonals, n_diag_groups):
        n_groups = i_ref.shape[0] // depth
        n_real = sum(n_blocks - d for d in diagonals)
        n_static = n_diag_groups + 2
        assert n_static % 2 == 0 and n_groups % 2 == 0 and n_groups >= n_static

        def group(g, static):
            count = min(depth, n_real - g * depth) if static else depth
            return [(i_ref[g * depth + u], j_ref[g * depth + u]) for u in range(count)]

        def iteration(m, parity, static):
            if not static or m < n_groups:
                stage_scores(group(m, static), z_ref.at[parity])
            if not static or 0 <= m - 2 < n_groups:
                stage_values(group(m - 2, static), static and m - 2 < n_diag_groups)
            if not static or 0 <= m - 1 < n_groups:
                stage_suffix_sums(group(m - 1, static), z_ref.at[1 - parity],
                                  static and m - 1 < n_diag_groups)

        for m in range(n_static):
            iteration(m, m % 2, True)

        def body(mm, carry):
            iteration(2 * mm, 0, False)
            iteration(2 * mm + 1, 1, False)
            return carry

        lax.fori_loop(n_static // 2, n_groups // 2, body, 0)
        for m in (n_groups, n_groups + 1):
            iteration(m, m % 2, True)

    sweep(ni_ref, nj_ref, range(ATT_NEAR_DIAGONALS), pl.cdiv(n_blocks, depth))

    far_blocks = range(ATT_NEAR_DIAGONALS, n_blocks)
    r_max = functools.reduce(jnp.maximum, [r_ref[i] for i in far_blocks])

    @pl.when(jnp.max(r_max) >= ATT_UNDERFLOW_LOG2)
    def _():
        sweep(fi_ref, fj_ref, range(ATT_NEAR_DIAGONALS, n_blocks), 0)

    for i in range(n_blocks):
        o_ref[i * t:(i + 1) * t, :] = acc_ref[i]


def _rms(x, g):
    ms = jnp.mean(x * x, axis=-1, keepdims=True)
    return x * lax.rsqrt(ms + EPS) * g


def _head_rms(o, g):
    first = lax.broadcasted_iota(jnp.int32, o.shape, 1) < HEAD_DIM
    o2 = o * o
    ss_a = jnp.sum(jnp.where(first, o2, 0.0), axis=-1, keepdims=True)
    ss_b = jnp.sum(jnp.where(first, 0.0, o2), axis=-1, keepdims=True)
    ms = jnp.where(first, ss_a, ss_b) * (1.0 / HEAD_DIM)
    return o * lax.rsqrt(ms + EPS) * g


def _attention(q, k, v, w2, to_cast, batch, seq):
    n_q = seq // T_ATT
    assert n_q % ATT_DEPTH == 0
    n_steps = batch * N_PAIRS

    def slab_spec(w):
        rows = w.shape[0]
        slab = next(s for s in range(BF16_SUBLANES, rows + 1, BF16_SUBLANES)
                    if rows % s == 0 and rows // s <= n_steps)
        last = rows // slab - 1
        return pl.BlockSpec((slab, w.shape[1]),
                            lambda b, p, *_: (jnp.minimum(b * N_PAIRS + p, last), 0))

    cast_specs = [slab_spec(w) for w in to_cast]

    def tile_list(diagonals):
        tiles = [(i, i - d) for d in diagonals for i in range(d, n_q)]
        tiles += [(n_q, 0)] * (-len(tiles) % (2 * ATT_DEPTH))
        return (jnp.asarray([i for i, _ in tiles], jnp.int32),
                jnp.asarray([j for _, j in tiles], jnp.int32))

    near = tile_list(range(ATT_NEAR_DIAGONALS))
    far = tile_list(range(ATT_NEAR_DIAGONALS, n_q))
    seq_blk = lambda b, p, *_: (b, p)
    return pl.pallas_call(
        functools.partial(_attn_kernel, len(to_cast)),
        grid_spec=pltpu.PrefetchScalarGridSpec(
            num_scalar_prefetch=4,
            grid=(batch, N_PAIRS),
            in_specs=[
                pl.BlockSpec((seq, LANES), seq_blk),
                pl.BlockSpec((seq, LANES), seq_blk),
                pl.BlockSpec((seq, LANES), seq_blk),
                pl.BlockSpec((2 * T_ATT, 2 * T_ATT), lambda b, p, *_: (0, 0)),
            ] + cast_specs,
            out_specs=[pl.BlockSpec((seq, LANES), seq_blk)] + cast_specs,
            scratch_shapes=[
                pltpu.VMEM((n_q + 1, T_ATT, LANES), BF16),
                pltpu.VMEM((n_q, 2 * T_ATT, LANES), BF16),
                pltpu.VMEM((n_q, 2 * T_ATT, LANES), BF16),
                pltpu.VMEM((n_q + 1, T_ATT, 2 * T_ATT), F32),
                pltpu.VMEM((n_q + 1, T_ATT, LANES), F32),
                pltpu.VMEM((2, ATT_DEPTH, T_ATT, 2 * T_ATT), F32),
                pltpu.VMEM((ATT_DEPTH, T_ATT, 2 * T_ATT), F32),
            ],
        ),
        out_shape=[jax.ShapeDtypeStruct((batch * seq, D_HEADS), F32)]
        + [jax.ShapeDtypeStruct(w.shape, BF16) for w in to_cast],
        compiler_params=pltpu.CompilerParams(
            dimension_semantics=("arbitrary", "arbitrary"),
            vmem_limit_bytes=VMEM_LIMIT),
        name="sb_attention",
    )(*near, *far, q, k, v, w2, *to_cast)


def _ffn_kernel(x_ref, yssd_ref, osb_ref, g_sb_ref, wo_ref, g_post_ref, g_pre_ref, wg_ref,
                wu_ref, wd_ref, g_out_ref, o_ref, x1_ref, h_ref, act_ref):
    chunks = [slice(c * FF_CHUNK, (c + 1) * FF_CHUNK) for c in range(D_FF // FF_CHUNK)]
    subs = [pl.ds(s * SUB_FFN, SUB_FFN) for s in range(TM_FFN // SUB_FFN)]
    pairs = [slice(p * LANES, (p + 1) * LANES) for p in range(N_PAIRS)]
    for rows in subs:
        y_sb = jnp.concatenate([_head_rms(osb_ref[rows, cols], g_sb_ref[:, cols])
                                for cols in pairs], axis=1).astype(BF16)
        mix = (_dot(yssd_ref[rows, :], wo_ref[0:D_HEADS, :])
               + _dot(y_sb, wo_ref[D_HEADS:2 * D_HEADS, :]))
        x1 = x_ref[rows, :] + _rms(mix, g_post_ref[...])
        x1_ref[rows, :] = x1
        h_ref[rows, :] = _rms(x1, g_pre_ref[...]).astype(BF16)
    for rows in subs:
        for cols in chunks:
            h = h_ref[rows, :]
            gate = _dot(h, wg_ref[:, cols])
            up = _dot(h, wu_ref[:, cols])
            act_ref[rows, cols] = (_silu(gate) * up).astype(BF16)
    for rows in subs:
        f = _dot(act_ref[rows, :], wd_ref[...])
        o_ref[rows, :] = x1_ref[rows, :] + _rms(f, g_out_ref[...])


def _out_ffn(x2, y_ssd, o_sb, g_sb, w_out, g_post, g_pre, wg, wu, wd, g_out):
    m = x2.shape[0]
    row = lambda i: (i, 0)
    const2 = lambda i: (0, 0)
    single = pl.Buffered(1)
    return pl.pallas_call(
        _ffn_kernel,
        grid=(m // TM_FFN,),
        in_specs=[
            pl.BlockSpec((TM_FFN, D_MODEL), row),
            pl.BlockSpec((TM_FFN, D_HEADS), row),
            pl.BlockSpec((TM_FFN, D_HEADS), row),
            pl.BlockSpec((1, D_HEADS), const2),
            pl.BlockSpec((2 * D_HEADS, D_MODEL), const2, pipeline_mode=single),
            pl.BlockSpec((1, D_MODEL), const2),
            pl.BlockSpec((1, D_MODEL), const2),
            pl.BlockSpec((D_MODEL, D_FF), const2, pipeline_mode=single),
            pl.BlockSpec((D_MODEL, D_FF), const2, pipeline_mode=single),
            pl.BlockSpec((D_FF, D_MODEL), const2, pipeline_mode=single),
            pl.BlockSpec((1, D_MODEL), const2),
        ],
        out_specs=pl.BlockSpec((TM_FFN, D_MODEL), row),
        out_shape=jax.ShapeDtypeStruct((m, D_MODEL), F32),
        scratch_shapes=[
            pltpu.VMEM((TM_FFN, D_MODEL), F32),
            pltpu.VMEM((TM_FFN, D_MODEL), BF16),
            pltpu.VMEM((TM_FFN, D_FF), BF16),
        ],
        compiler_params=pltpu.CompilerParams(
            dimension_semantics=("arbitrary",), vmem_limit_bytes=VMEM_LIMIT),
        name="out_ffn",
    )(x2, y_ssd, o_sb, g_sb, w_out, g_post, g_pre, wg, wu, wd, g_out)


def _expand_heads(v):
    return jnp.repeat(v.astype(F32), HEAD_DIM)[None, :]


def _pad_lanes(v):
    return jnp.pad(v.astype(F32), (0, LANES - v.shape[0]))[None, :]


def _layer(x2, batch, seq, w_in, layer, pre_mix_gain, conv_w, conv_b, dt_bias, a_log, d_skip,
           ssd_norm_gain, sb_norm_gain, w_out, post_mix_gain, pre_ffn_gain, w_gate, w_up,
           w_down, post_ffn_gain):
    utri = jnp.triu(jnp.ones((T_SSD, T_SSD), BF16))
    a_row = _pad_lanes(-jnp.exp(a_log.astype(F32)))
    q, k, v, y_ssd = _proj_ssd(x2, pre_mix_gain[None, :], jnp.swapaxes(w_in, 1, 2), layer,
                               _pad_lanes(dt_bias), conv_w, conv_b[None, :], a_row,
                               _expand_heads(d_skip), ssd_norm_gain[None, :], utri, seq)

    jj = jnp.arange(T_ATT)
    later = (jj[:, None] >= jj[None, :]).astype(BF16)
    half = jnp.concatenate([later, jnp.ones((T_ATT, T_ATT), BF16)], axis=1)
    w2 = jnp.concatenate([half, half], axis=0)
    o_sb, wo, wg, wu, wd = _attention(q, k, v, w2, (w_out, w_gate, w_up, w_down), batch, seq)

    return _out_ffn(x2, y_ssd, o_sb, sb_norm_gain[None, :], wo, post_mix_gain[None, :],
                    pre_ffn_gain[None, :], wg, wu, wd, post_ffn_gain[None, :])


def kernel(x, pre_mix_gain, w_in, conv_w, conv_b, dt_bias, a_log, d_skip, ssd_norm_gain,
           sb_norm_gain, w_out, post_mix_gain, pre_ffn_gain, w_gate, w_up, w_down,
           post_ffn_gain):
    batch, seq, d = x.shape
    x2 = x.reshape(batch * seq, d)
    params = (pre_mix_gain, conv_w, conv_b, dt_bias, a_log, d_skip, ssd_norm_gain,
              sb_norm_gain, w_out, post_mix_gain, pre_ffn_gain, w_gate, w_up, w_down,
              post_ffn_gain)
    for layer in range(pre_mix_gain.shape[0]):
        x2 = _layer(x2, batch, seq, w_in, layer, *(p[layer] for p in params))
    return x2.reshape(batch, seq, d)
```

```python
import functools
import itertools
import math

import jax
import jax.numpy as jnp
from jax import lax
from jax.experimental import pallas as pl
from jax.experimental.pallas import tpu as pltpu

F32 = jnp.float32
BF16 = jnp.bfloat16

EPS = 1e-6
LANES = 128
BF16_SUBLANES = 16

D_MODEL = 1024
N_HEADS = 8
HEAD_DIM = 64
D_HEADS = N_HEADS * HEAD_DIM
N_PAIRS = N_HEADS // 2
SSD_GROUPS = 2
SSD_STATE = 128
CONV_WIDTH = 4
D_CONV = D_HEADS + 2 * SSD_GROUPS * SSD_STATE
D_FF = 2816

C_Z = 0
C_XBC = C_Z + D_HEADS
C_DT = C_XBC + D_CONV
C_Q = C_DT + LANES
C_K = C_Q + D_HEADS
C_V = C_K + D_HEADS
C_END = C_V + D_HEADS

TM_PROJ = 512
PROJ_PIECE = 256
T_SSD = 128
SSD_SEGMENTS = 2 + 2 * N_PAIRS
T_ATT = 128
TM_FFN = 512
SUB_FFN = 256
FF_CHUNK = 256
CONV_TAIL = 8

VMEM_LIMIT = 56 * 1024 * 1024


def _dot(a, b):
    return jnp.dot(a, b, preferred_element_type=F32)


def _dot_nt(a, b):
    return lax.dot_general(a, b, (((1,), (1,)), ((), ())), preferred_element_type=F32)


def _dot_tn(a, b):
    return lax.dot_general(a, b, (((0,), (0,)), ((), ())), preferred_element_type=F32)


def _split3(x):
    hi = x.astype(BF16)
    r = x - hi.astype(F32)
    mid = r.astype(BF16)
    lo = (r - mid.astype(F32)).astype(BF16)
    return hi, mid, lo


def _silu(x):
    return x / (1.0 + jnp.exp(-x))


def _softplus(x):
    e = jnp.exp(-jnp.abs(x))
    u = 1.0 + e
    tiny = u == 1.0
    log1p_e = jnp.where(tiny, e, jnp.log(u) * (e / jnp.where(tiny, 1.0, u - 1.0)))
    return jnp.maximum(x, 0.0) + log1p_e


def _projection_pieces(x_ref, g_ref, w_ref, dtb_ref, h_ref, sz_ref, xbc_ref, dt_ref, q_ref,
                       k_ref, v_ref):
    x = x_ref[...]
    ms = jnp.mean(x * x, axis=-1, keepdims=True)
    h_ref[...] = (x * lax.rsqrt(ms + EPS) * g_ref[...]).astype(BF16)
    to_bf16 = lambda y: y.astype(BF16)
    segments = [(sz_ref, C_Z, C_XBC, _silu), (xbc_ref, C_XBC, C_DT, lambda y: y),
                (dt_ref, C_DT, C_Q, lambda y: _softplus(y + dtb_ref[...])),
                (q_ref, C_Q, C_K, to_bf16), (k_ref, C_K, C_V, to_bf16),
                (v_ref, C_V, C_END, to_bf16)]

    def piece(out_ref, c0, lo, hi, post):
        def run():
            out_ref[:, lo:hi] = post(_dot(h_ref[...], w_ref[:, c0 + lo:c0 + hi]))
        return run

    return [piece(out_ref, c0, lo, min(lo + PROJ_PIECE, c1 - c0), post)
            for out_ref, c0, c1, post in segments for lo in range(0, c1 - c0, PROJ_PIECE)]


def _prepare_weight(wt_ref, w_ref):
    o_dt = D_HEADS + D_CONV
    o_q = o_dt + N_HEADS
    scale = 1.0 / math.sqrt(HEAD_DIM)
    lane = lax.broadcasted_iota(jnp.int32, (D_MODEL, LANES), 1)

    def block(row0):
        return wt_ref[row0:row0 + LANES, :].T

    for c in range(0, o_dt, LANES):
        w_ref[:, C_Z + c:C_Z + c + LANES] = block(c).astype(BF16)
    w_ref[:, C_DT:C_Q] = jnp.where(lane < N_HEADS, block(o_dt), 0.0).astype(BF16)
    for c in range(0, D_HEADS, LANES):
        w_ref[:, C_Q + c:C_Q + c + LANES] = (block(o_q + c) * scale).astype(BF16)
    for c in range(D_HEADS, 3 * D_HEADS, LANES):
        w_ref[:, C_Q + c:C_Q + c + LANES] = block(o_q + c).astype(BF16)


def _proj_ssd_kernel(blocks_per_seq, n_blocks, x_ref, g_ref, wt_ref, dtb_ref, cw_ref, cb_ref,
                     a_ref, dskip_ref, gain_ref, utri_ref, q_ref, k_ref, v_ref, y_ref,
                     w_ref, h_ref, sz_ref, xbc_ref, dt_ref, ext_ref, state_ref):
    g = pl.program_id(0)

    @pl.when(g == 0)
    def _():
        _prepare_weight(wt_ref, w_ref)

    @pl.when(lax.rem(g + blocks_per_seq - 1, blocks_per_seq) == 0)
    def _():
        ext_ref[0:CONV_TAIL, :] = jnp.zeros((CONV_TAIL, D_CONV), F32)
        state_ref[...] = jnp.zeros(state_ref.shape, F32)

    def step(new, old, project=True, scan=True):
        pieces = _projection_pieces(
            x_ref, g_ref, w_ref, dtb_ref, h_ref, sz_ref.at[new], xbc_ref.at[new],
            dt_ref.at[new], q_ref, k_ref, v_ref) if project else []
        chunks = [pl.ds(c * T_SSD, T_SSD) for c in range(TM_PROJ // T_SSD)] if scan else []
        segments = itertools.chain.from_iterable(
            _ssd_chunk(sz_ref.at[old, rows], xbc_ref.at[old, rows], dt_ref.at[old, rows],
                       cw_ref, cb_ref, a_ref, dskip_ref, gain_ref, utri_ref, y_ref.at[rows],
                       ext_ref, state_ref) for rows in chunks)
        if not scan:
            for run in pieces:
                run()
            return
        n_segments = len(chunks) * SSD_SEGMENTS
        assert len(pieces) <= n_segments
        slot = {(n * n_segments) // len(pieces): run for n, run in enumerate(pieces)}
        for s in range(n_segments):
            next(segments)
            if s in slot:
                slot[s]()

    even = lax.rem(g, 2) == 0

    @pl.when(g == 0)
    def _():
        step(0, 1, scan=False)

    @pl.when((g > 0) & (g < n_blocks) & even)
    def _():
        step(0, 1)

    @pl.when((g < n_blocks) & jnp.logical_not(even))
    def _():
        step(1, 0)

    @pl.when(g == n_blocks)
    def _():
        step(n_blocks % 2, 1 - n_blocks % 2, project=False)


def _proj_ssd(x2, gain, w_in_t, layer, dtb, conv_w, conv_b, a_row, dskip, ssd_gain, utri, seq):
    m = x2.shape[0]
    n_blocks = m // TM_PROJ
    assert seq % TM_PROJ == 0 and TM_PROJ % T_SSD == 0
    assert w_in_t.shape[1:] == (C_END - (LANES - N_HEADS), D_MODEL)
    cur = lambda g: (jnp.minimum(g, n_blocks - 1), 0)
    prev = lambda g: (jnp.maximum(g - 1, 0), 0)
    const = lambda g: (0, 0)
    qkv_spec = pl.BlockSpec((TM_PROJ, D_HEADS), cur)
    qkv_shape = jax.ShapeDtypeStruct((m, D_HEADS), BF16)
    return pl.pallas_call(
        functools.partial(_proj_ssd_kernel, seq // TM_PROJ, n_blocks),
        grid=(n_blocks + 1,),
        in_specs=[
            pl.BlockSpec((TM_PROJ, D_MODEL), cur),
            pl.BlockSpec((1, D_MODEL), const),
            pl.BlockSpec((None,) + w_in_t.shape[1:], lambda g: (layer, 0, 0),
                         pipeline_mode=pl.Buffered(1)),
            pl.BlockSpec((1, LANES), const),
            pl.BlockSpec((CONV_WIDTH, D_CONV), const),
            pl.BlockSpec((1, D_CONV), const),
            pl.BlockSpec((1, LANES), const),
            pl.BlockSpec((1, D_HEADS), const),
            pl.BlockSpec((1, D_HEADS), const),
            pl.BlockSpec((T_SSD, T_SSD), const),
        ],
        out_specs=[qkv_spec, qkv_spec, qkv_spec, pl.BlockSpec((TM_PROJ, D_HEADS), prev)],
        out_shape=[qkv_shape, qkv_shape, qkv_shape, qkv_shape],
        scratch_shapes=[
            pltpu.VMEM((D_MODEL, C_END), BF16),
            pltpu.VMEM((TM_PROJ, D_MODEL), BF16),
            pltpu.VMEM((2, TM_PROJ, D_HEADS), F32),
            pltpu.VMEM((2, TM_PROJ, D_CONV), F32),
            pltpu.VMEM((2, TM_PROJ, LANES), F32),
            pltpu.VMEM((T_SSD + CONV_TAIL, D_CONV), F32),
            pltpu.VMEM((N_PAIRS, SSD_STATE, LANES), F32),
        ],
        compiler_params=pltpu.CompilerParams(
            dimension_semantics=("arbitrary",), vmem_limit_bytes=VMEM_LIMIT),
        name="proj_ssd",
    )(x2, gain, w_in_t, dtb, conv_w, conv_b, a_row, dskip, ssd_gain, utri)


def _ssd_chunk(sz_ref, xbc_ref, dt_ref, cw_ref, cb_ref, a_ref, dskip_ref, gain_ref, utri_ref,
               y_ref, ext_ref, state_ref):
    t = T_SSD

    ext_ref[CONV_TAIL:CONV_TAIL + t, :] = xbc_ref[...]
    conv = cb_ref[...]
    for k in range(CONV_WIDTH):
        off = CONV_TAIL - (CONV_WIDTH - 1) + k
        conv = conv + ext_ref[off:off + t, :] * cw_ref[k:k + 1, :]
    ext_ref[0:CONV_TAIL, :] = ext_ref[t:t + CONV_TAIL, :]
    xa = _silu(conv)
    yield

    dtv = dt_ref[...]
    adt = dtv * a_ref[...]
    utri = utri_ref[...]
    acs_t = sum(_dot(part, utri) for part in _split3(adt.T))
    acs = acs_t.T

    lane = lax.broadcasted_iota(jnp.int32, (t, LANES), 1)
    first_head = lane < HEAD_DIM
    li = lax.broadcasted_iota(jnp.int32, (t, t), 0)
    si = lax.broadcasted_iota(jnp.int32, (t, t), 1)
    causal = li >= si

    def col(v, h):
        return jnp.broadcast_to(v[:, h:h + 1], (t, LANES))

    y_blocks = []
    cb_mats = []
    for g in range(SSD_GROUPS):
        bm = xa[:, D_HEADS + g * SSD_STATE:D_HEADS + (g + 1) * SSD_STATE].astype(BF16)
        cm = xa[:, D_HEADS + (SSD_GROUPS + g) * SSD_STATE:
                D_HEADS + (SSD_GROUPS + g + 1) * SSD_STATE].astype(BF16)
        cb_mats.append((bm, cm, _dot_nt(cm, bm)))
    yield

    for p in range(N_PAIRS):
        ha, hb = 2 * p, 2 * p + 1
        bm, cm, cbm = cb_mats[p // (N_PAIRS // SSD_GROUPS)]
        x2 = xa[:, p * LANES:(p + 1) * LANES]
        dt2 = jnp.where(first_head, col(dtv, ha), col(dtv, hb))
        acs2 = jnp.where(first_head, col(acs, ha), col(acs, hb))
        xdt2 = x2 * dt2

        def decay(h):
            seg = col(acs, h) - jnp.broadcast_to(acs_t[h:h + 1, :], (t, t))
            return (cbm * jnp.exp(jnp.where(causal, seg, -jnp.inf))).astype(BF16)

        m2 = jnp.concatenate([decay(ha), decay(hb)], axis=1)
        xdt_a = jnp.where(first_head, xdt2, 0.0).astype(BF16)
        xdt_b = jnp.where(first_head, 0.0, xdt2).astype(BF16)
        y_diag = _dot(m2, jnp.concatenate([xdt_a, xdt_b], axis=0))
        yield

        prev = state_ref[p]
        y_off = _dot(cm, prev.astype(BF16)) * jnp.exp(acs2)
        last = acs2[t - 1:t, :]
        xs = (xdt2 * jnp.exp(last - acs2)).astype(BF16)
        state_ref[p] = prev * jnp.exp(last) + _dot_tn(bm, xs)

        y_blocks.append(y_diag + y_off + dskip_ref[:, p * LANES:(p + 1) * LANES] * x2)
        if p + 1 < N_PAIRS:
            yield

    per_group = N_PAIRS // SSD_GROUPS
    for g in range(SSD_GROUPS):
        ys = []
        for p in range(g * per_group, (g + 1) * per_group):
            ys.append(y_blocks[p] * sz_ref[:, p * LANES:(p + 1) * LANES])
        ss = sum(jnp.sum(y * y, axis=-1, keepdims=True) for y in ys)
        inv = lax.rsqrt(ss * (1.0 / (per_group * LANES)) + EPS)
        for j, y in enumerate(ys):
            p = g * per_group + j
            y_ref[:, p * LANES:(p + 1) * LANES] = (
                y * inv * gain_ref[:, p * LANES:(p + 1) * LANES]).astype(BF16)
    yield


LOG2E = 1.4426950408889634
ATT_DEPTH = 4
ATT_NEAR_DIAGONALS = 3
ATT_UNDERFLOW_LOG2 = -160.0


def _attn_kernel(n_cast, ni_ref, nj_ref, fi_ref, fj_ref, q_ref, k_ref, v_ref, w2_ref, *refs):
    cast_in, (o_ref, *cast_out) = refs[:n_cast], refs[n_cast:2 * n_cast + 1]
    q2_ref, kst_ref, vst_ref, r_ref, acc_ref, z_ref, arg_ref = refs[2 * n_cast + 1:]
    for src_ref, dst_ref in zip(cast_in, cast_out, strict=True):
        dst_ref[...] = src_ref[...].astype(BF16)

    t = T_ATT
    n_blocks = q2_ref.shape[0] - 1
    depth = ATT_DEPTH

    lane_v = lax.broadcasted_iota(jnp.int32, (t, LANES), 1)
    first_v = lane_v < HEAD_DIM

    for j in range(n_blocks):
        kb = k_ref[j * t:(j + 1) * t, :]
        vb = v_ref[j * t:(j + 1) * t, :]
        zero = jnp.zeros_like(kb)
        q2_ref[j] = q_ref[j * t:(j + 1) * t, :]
        kst_ref[j, 0:t, :] = jnp.where(first_v, kb, zero)
        kst_ref[j, t:2 * t, :] = jnp.where(first_v, zero, kb)
        vst_ref[j, 0:t, :] = jnp.where(first_v, vb, zero)
        vst_ref[j, t:2 * t, :] = jnp.where(first_v, zero, vb)
    q2_ref[n_blocks] = jnp.zeros(q2_ref.shape[1:], BF16)
    r_ref[n_blocks] = jnp.zeros(r_ref.shape[1:], F32)
    acc_ref[n_blocks] = jnp.zeros(acc_ref.shape[1:], F32)

    ti = lax.broadcasted_iota(jnp.int32, (t, 2 * t), 0)
    si = lax.broadcasted_iota(jnp.int32, (t, 2 * t), 1)
    strict = jnp.where(si >= t, si - t, si) < ti

    def stage_scores(tiles, z_buf):
        for u, (i, j) in enumerate(tiles):
            z_buf[u] = _dot_nt(q2_ref[i], kst_ref[j])

    def stage_suffix_sums(tiles, z_buf, diag):
        for u, (i, _) in enumerate(tiles):
            zn = z_buf[u] * (-LOG2E)
            l1 = jnp.minimum(zn, 0.0) - jnp.log2(1.0 + jnp.exp2(-jnp.abs(zn)))
            if diag:
                l1 = jnp.where(strict, l1, 0.0)
            hi = l1.astype(BF16)
            lo = (l1 - hi.astype(F32)).astype(BF16)
            res = [_dot(jnp.concatenate([hi[:, h * t:(h + 1) * t], lo[:, h * t:(h + 1) * t]],
                                        axis=1), w2_ref[...]) for h in range(2)]
            arg = jnp.concatenate([r[:, 0:t] for r in res], axis=1) - zn
            tot = jnp.concatenate([r[:, t:2 * t] for r in res], axis=1)
            if diag:
                arg = jnp.where(strict, arg, -jnp.inf)
            else:
                r_old = r_ref[i]
                arg = arg + r_old
                tot = tot + r_old
            arg_ref[u] = arg
            r_ref[i] = tot

    def stage_values(tiles, diag):
        for u, (i, j) in enumerate(tiles):
            contrib = _dot(jnp.exp2(arg_ref[u]).astype(BF16), vst_ref[j])
            if diag:
                acc_ref[i] = contrib
            else:
                acc_ref[i] += contrib

    def sweep(i_ref, j_ref, diagonals, n_diag_groups):
        n_groups = i_ref.shape[0] // depth
        n_real = sum(n_blocks - d for d in diagonals)
        n_static = n_diag_groups + 2
        assert n_static % 2 == 0 and n_groups % 2 == 0 and n_groups >= n_static

        def group(g, static):
            count = min(depth, n_real - g * depth) if static else depth
            return [(i_ref[g * depth + u], j_ref[g * depth + u]) for u in range(count)]

        def iteration(m, parity, static):
            if not static or m < n_groups:
                stage_scores(group(m, static), z_ref.at[parity])
            if not static or 0 <= m - 2 < n_groups:
                stage_values(group(m - 2, static), static and m - 2 < n_diag_groups)
            if not static or 0 <= m - 1 < n_groups:
                stage_suffix_sums(group(m - 1, static), z_ref.at[1 - parity],
                                  static and m - 1 < n_diag_groups)

        for m in range(n_static):
            iteration(m, m % 2, True)

        def body(mm, carry):
            iteration(2 * mm, 0, False)
            iteration(2 * mm + 1, 1, False)
            return carry

        lax.fori_loop(n_static // 2, n_groups // 2, body, 0)
        for m in (n_groups, n_groups + 1):
            iteration(m, m % 2, True)

    sweep(ni_ref, nj_ref, range(ATT_NEAR_DIAGONALS), pl.cdiv(n_blocks, depth))

    far_blocks = range(ATT_NEAR_DIAGONALS, n_blocks)
    r_max = functools.reduce(jnp.maximum, [r_ref[i] for i in far_blocks])

    @pl.when(jnp.max(r_max) >= ATT_UNDERFLOW_LOG2)
    def _():
        sweep(fi_ref, fj_ref, range(ATT_NEAR_DIAGONALS, n_blocks), 0)

    for i in range(n_blocks):
        o_ref[i * t:(i + 1) * t, :] = acc_ref[i]


def _rms(x, g):
    ms = jnp.mean(x * x, axis=-1, keepdims=True)
    return x * lax.rsqrt(ms + EPS) * g


def _head_rms(o, g):
    first = lax.broadcasted_iota(jnp.int32, o.shape, 1) < HEAD_DIM
    o2 = o * o
    ss_a = jnp.sum(jnp.where(first, o2, 0.0), axis=-1, keepdims=True)
    ss_b = jnp.sum(jnp.where(first, 0.0, o2), axis=-1, keepdims=True)
    ms = jnp.where(first, ss_a, ss_b) * (1.0 / HEAD_DIM)
    return o * lax.rsqrt(ms + EPS) * g


def _attention(q, k, v, w2, to_cast, batch, seq):
    n_q = seq // T_ATT
    assert n_q % ATT_DEPTH == 0
    n_steps = batch * N_PAIRS

    def slab_spec(w):
        rows = w.shape[0]
        slab = next(s for s in range(BF16_SUBLANES, rows + 1, BF16_SUBLANES)
                    if rows % s == 0 and rows // s <= n_steps)
        last = rows // slab - 1
        return pl.BlockSpec((slab, w.shape[1]),
                            lambda b, p, *_: (jnp.minimum(b * N_PAIRS + p, last), 0))

    cast_specs = [slab_spec(w) for w in to_cast]

    def tile_list(diagonals):
        tiles = [(i, i - d) for d in diagonals for i in range(d, n_q)]
        tiles += [(n_q, 0)] * (-len(tiles) % (2 * ATT_DEPTH))
        return (jnp.asarray([i for i, _ in tiles], jnp.int32),
                jnp.asarray([j for _, j in tiles], jnp.int32))

    near = tile_list(range(ATT_NEAR_DIAGONALS))
    far = tile_list(range(ATT_NEAR_DIAGONALS, n_q))
    seq_blk = lambda b, p, *_: (b, p)
    return pl.pallas_call(
        functools.partial(_attn_kernel, len(to_cast)),
        grid_spec=pltpu.PrefetchScalarGridSpec(
            num_scalar_prefetch=4,
            grid=(batch, N_PAIRS),
            in_specs=[
                pl.BlockSpec((seq, LANES), seq_blk),
                pl.BlockSpec((seq, LANES), seq_blk),
                pl.BlockSpec((seq, LANES), seq_blk),
                pl.BlockSpec((2 * T_ATT, 2 * T_ATT), lambda b, p, *_: (0, 0)),
            ] + cast_specs,
            out_specs=[pl.BlockSpec((seq, LANES), seq_blk)] + cast_specs,
            scratch_shapes=[
                pltpu.VMEM((n_q + 1, T_ATT, LANES), BF16),
                pltpu.VMEM((n_q, 2 * T_ATT, LANES), BF16),
                pltpu.VMEM((n_q, 2 * T_ATT, LANES), BF16),
                pltpu.VMEM((n_q + 1, T_ATT, 2 * T_ATT), F32),
                pltpu.VMEM((n_q + 1, T_ATT, LANES), F32),
                pltpu.VMEM((2, ATT_DEPTH, T_ATT, 2 * T_ATT), F32),
                pltpu.VMEM((ATT_DEPTH, T_ATT, 2 * T_ATT), F32),
            ],
        ),
        out_shape=[jax.ShapeDtypeStruct((batch * seq, D_HEADS), F32)]
        + [jax.ShapeDtypeStruct(w.shape, BF16) for w in to_cast],
        compiler_params=pltpu.CompilerParams(
            dimension_semantics=("arbitrary", "arbitrary"),
            vmem_limit_bytes=VMEM_LIMIT),
        name="sb_attention",
    )(*near, *far, q, k, v, w2, *to_cast)


def _ffn_kernel(x_ref, yssd_ref, osb_ref, g_sb_ref, wo_ref, g_post_ref, g_pre_ref, wg_ref,
                wu_ref, wd_ref, g_out_ref, o_ref, x1_ref, h_ref, act_ref):
    chunks = [slice(c * FF_CHUNK, (c + 1) * FF_CHUNK) for c in range(D_FF // FF_CHUNK)]
    subs = [pl.ds(s * SUB_FFN, SUB_FFN) for s in range(TM_FFN // SUB_FFN)]
    pairs = [slice(p * LANES, (p + 1) * LANES) for p in range(N_PAIRS)]
    for rows in subs:
        y_sb = jnp.concatenate([_head_rms(osb_ref[rows, cols], g_sb_ref[:, cols])
                                for cols in pairs], axis=1).astype(BF16)
        mix = (_dot(yssd_ref[rows, :], wo_ref[0:D_HEADS, :])
               + _dot(y_sb, wo_ref[D_HEADS:2 * D_HEADS, :]))
        x1 = x_ref[rows, :] + _rms(mix, g_post_ref[...])
        x1_ref[rows, :] = x1
        h_ref[rows, :] = _rms(x1, g_pre_ref[...]).astype(BF16)
    for rows in subs:
        for cols in chunks:
            h = h_ref[rows, :]
            gate = _dot(h, wg_ref[:, cols])
            up = _dot(h, wu_ref[:, cols])
            act_ref[rows, cols] = (_silu(gate) * up).astype(BF16)
    for rows in subs:
        f = _dot(act_ref[rows, :], wd_ref[...])
        o_ref[rows, :] = x1_ref[rows, :] + _rms(f, g_out_ref[...])


def _out_ffn(x2, y_ssd, o_sb, g_sb, w_out, g_post, g_pre, wg, wu, wd, g_out):
    m = x2.shape[0]
    row = lambda i: (i, 0)
    const2 = lambda i: (0, 0)
    single = pl.Buffered(1)
    return pl.pallas_call(
        _ffn_kernel,
        grid=(m // TM_FFN,),
        in_specs=[
            pl.BlockSpec((TM_FFN, D_MODEL), row),
            pl.BlockSpec((TM_FFN, D_HEADS), row),
            pl.BlockSpec((TM_FFN, D_HEADS), row),
            pl.BlockSpec((1, D_HEADS), const2),
            pl.BlockSpec((2 * D_HEADS, D_MODEL), const2, pipeline_mode=single),
            pl.BlockSpec((1, D_MODEL), const2),
            pl.BlockSpec((1, D_MODEL), const2),
            pl.BlockSpec((D_MODEL, D_FF), const2, pipeline_mode=single),
            pl.BlockSpec((D_MODEL, D_FF), const2, pipeline_mode=single),
            pl.BlockSpec((D_FF, D_MODEL), const2, pipeline_mode=single),
            pl.BlockSpec((1, D_MODEL), const2),
        ],
        out_specs=pl.BlockSpec((TM_FFN, D_MODEL), row),
        out_shape=jax.ShapeDtypeStruct((m, D_MODEL), F32),
        scratch_shapes=[
            pltpu.VMEM((TM_FFN, D_MODEL), F32),
            pltpu.VMEM((TM_FFN, D_MODEL), BF16),
            pltpu.VMEM((TM_FFN, D_FF), BF16),
        ],
        compiler_params=pltpu.CompilerParams(
            dimension_semantics=("arbitrary",), vmem_limit_bytes=VMEM_LIMIT),
        name="out_ffn",
    )(x2, y_ssd, o_sb, g_sb, w_out, g_post, g_pre, wg, wu, wd, g_out)


def _expand_heads(v):
    return jnp.repeat(v.astype(F32), HEAD_DIM)[None, :]


def _pad_lanes(v):
    return jnp.pad(v.astype(F32), (0, LANES - v.shape[0]))[None, :]


def _layer(x2, batch, seq, w_in, layer, pre_mix_gain, conv_w, conv_b, dt_bias, a_log, d_skip,
           ssd_norm_gain, sb_norm_gain, w_out, post_mix_gain, pre_ffn_gain, w_gate, w_up,
           w_down, post_ffn_gain):
    utri = jnp.triu(jnp.ones((T_SSD, T_SSD), BF16))
    a_row = _pad_lanes(-jnp.exp(a_log.astype(F32)))
    q, k, v, y_ssd = _proj_ssd(x2, pre_mix_gain[None, :], jnp.swapaxes(w_in, 1, 2), layer,
                               _pad_lanes(dt_bias), conv_w, conv_b[None, :], a_row,
                               _expand_heads(d_skip), ssd_norm_gain[None, :], utri, seq)

    jj = jnp.arange(T_ATT)
    later = (jj[:, None] >= jj[None, :]).astype(BF16)
    half = jnp.concatenate([later, jnp.ones((T_ATT, T_ATT), BF16)], axis=1)
    w2 = jnp.concatenate([half, half], axis=0)
    o_sb, wo, wg, wu, wd = _attention(q, k, v, w2, (w_out, w_gate, w_up, w_down), batch, seq)

    return _out_ffn(x2, y_ssd, o_sb, sb_norm_gain[None, :], wo, post_mix_gain[None, :],
                    pre_ffn_gain[None, :], wg, wu, wd, post_ffn_gain[None, :])


def kernel(x, pre_mix_gain, w_in, conv_w, conv_b, dt_bias, a_log, d_skip, ssd_norm_gain,
           sb_norm_gain, w_out, post_mix_gain, pre_ffn_gain, w_gate, w_up, w_down,
           post_ffn_gain):
    batch, seq, d = x.shape
    x2 = x.reshape(batch * seq, d)
    params = (pre_mix_gain, conv_w, conv_b, dt_bias, a_log, d_skip, ssd_norm_gain,
              sb_norm_gain, w_out, post_mix_gain, pre_ffn_gain, w_gate, w_up, w_down,
              post_ffn_gain)
    for layer in range(pre_mix_gain.shape[0]):
        x2 = _layer(x2, batch, seq, w_in, layer, *(p[layer] for p in params))
    return x2.reshape(batch, seq, d)
```
